```python
import jax, jax.numpy as jnp
from jax import lax
import numpy as np


D_MODEL = 1024
BATCH = 8
SEQ = 2048
DEPTH = 2
DEC_BATCH = 128
DEC_SEQ = 1
PAST_LEN = 8192
PAGE_SIZE = 128

N_A_LAYERS = DEPTH // 2
N_B_LAYERS = DEPTH - N_A_LAYERS
N_DENSE = (DEPTH + 1) // 2
N_MOE = DEPTH // 2
CONV_WIDTH = 31
CONV_CTX = CONV_WIDTH - 1
HEAD_DIM = 64
N_HEADS = D_MODEL // HEAD_DIM
N_KV_HEADS = 4
GROUP = N_HEADS // N_KV_HEADS
WINDOW = 128
BLOCK = WINDOW
D_FF = 2816
N_EXPERTS = 8
TOP_K = 2
D_EXPERT = 3584
PLE_DIM = 256
EPS = 1e-6

kernel_name = 'yoco_conformer_swa_sink_moe_decode_step'


def rms_norm(x, g):
    xf = x.astype(jnp.float32)
    y = xf * lax.rsqrt(jnp.mean(xf * xf, axis=-1, keepdims=True) + EPS)
    return (y * g.astype(jnp.float32)).astype(x.dtype)


def layer_norm(x, g, b):
    xf = x.astype(jnp.float32)
    mu = jnp.mean(xf, axis=-1, keepdims=True)
    var = jnp.mean(jnp.square(xf - mu), axis=-1, keepdims=True)
    y = (xf - mu) * lax.rsqrt(var + EPS) * g.astype(jnp.float32) + b.astype(jnp.float32)
    return y.astype(x.dtype)


def conv_module(h, past, w_pw1, b_pw1, w_dw, b_dw, ln_g, ln_b, w_pw2, b_pw2):
    a, gate = jnp.split(h @ w_pw1 + b_pw1, 2, axis=-1)
    u = a * jax.nn.sigmoid(gate)
    u_ext = jnp.concatenate([past.astype(u.dtype), u], axis=1)
    c = lax.conv_general_dilated(u_ext, w_dw[:, None, :], window_strides=(1,), padding='VALID',
                                 dimension_numbers=('NWC', 'WIO', 'NWC'),
                                 feature_group_count=D_MODEL) + b_dw
    c = jax.nn.silu(layer_norm(c, ln_g, ln_b))
    return c @ w_pw2 + b_pw2, u_ext[:, -CONV_CTX:]


def swiglu(h, w_gate, w_up, w_down):
    return (jax.nn.silu(h @ w_gate) * (h @ w_up)) @ w_down


def moe_swiglu(h, w_router, b_router, w_gate, w_up, w_down):
    logits = (h @ w_router + b_router).astype(jnp.float32)
    top_vals, top_idx = lax.top_k(logits, TOP_K)
    top_w = jax.nn.softmax(top_vals, axis=-1)
    gates = jnp.sum(jax.nn.one_hot(top_idx, N_EXPERTS, dtype=jnp.float32) * top_w[..., None], axis=-2)
    gates = gates.astype(h.dtype)
    out = jnp.zeros_like(h)
    for e in range(N_EXPERTS):
        out = out + gates[..., e:e + 1] * swiglu(h, w_gate[e], w_up[e], w_down[e])
    return out


def ple_term(h, p_i, g, w_gate, w_proj):
    return jax.nn.sigmoid(rms_norm(h, g) @ w_gate) * (p_i @ w_proj)


def shared_kv(h, norm_kv, w_k, w_v, k_norm):
    s = rms_norm(h, norm_kv)
    k = (s @ w_k).reshape(*s.shape[:-1], N_KV_HEADS, HEAD_DIM)
    v = (s @ w_v).reshape(*s.shape[:-1], N_KV_HEADS, HEAD_DIM)
    return rms_norm(k, k_norm), v


def project_q(h, w_q, q_norm):
    q = (h @ w_q).reshape(*h.shape[:-1], N_KV_HEADS, GROUP, HEAD_DIM)
    return rms_norm(q, q_norm)


def sink_attend(q, k, v, mask, sinks):
    s = jnp.einsum('...qkgd,...jkd->...kgqj', q, k, preferred_element_type=jnp.float32) * (HEAD_DIM ** -0.5)
    s = jnp.where(mask, s, -jnp.inf)
    sink = sinks.astype(jnp.float32).reshape(N_KV_HEADS, GROUP, 1)
    m = jnp.maximum(jnp.max(s, axis=-1), sink)
    e = jnp.exp(s - m[..., None])
    den = jnp.sum(e, axis=-1) + jnp.exp(sink - m)
    probs = (e / den[..., None]).astype(v.dtype)
    return jnp.einsum('...kgqj,...jkd->...qkgd', probs, v)


def window_attn_prompt(q, k, v, sinks):
    n, t = q.shape[0], q.shape[1]
    nb = t // BLOCK
    qb = q.reshape(n, nb, BLOCK, N_KV_HEADS, GROUP, HEAD_DIM)
    pad = jnp.zeros((n, BLOCK, N_KV_HEADS, HEAD_DIM), k.dtype)
    kb = jnp.concatenate([pad, k], axis=1).reshape(n, nb + 1, BLOCK, N_KV_HEADS, HEAD_DIM)
    vb = jnp.concatenate([pad.astype(v.dtype), v], axis=1).reshape(n, nb + 1, BLOCK, N_KV_HEADS, HEAD_DIM)
    kk = jnp.concatenate([kb[:, :-1], kb[:, 1:]], axis=2)
    vv = jnp.concatenate([vb[:, :-1], vb[:, 1:]], axis=2)
    qpos = jnp.arange(nb)[:, None] * BLOCK + jnp.arange(BLOCK)[None, :]
    kpos = jnp.arange(nb)[:, None] * BLOCK - BLOCK + jnp.arange(2 * BLOCK)[None, :]
    d = qpos[:, :, None] - kpos[:, None, :]
    mask = (d >= 0) & (d <= WINDOW) & (kpos[:, None, :] >= 0)
    o = sink_attend(qb, kk, vv, mask[:, None, None], sinks)
    return o.reshape(n, t, N_HEADS * HEAD_DIM)


def window_attn_sample(q, k_all, v_all, sinks):
    n, t = q.shape[0], q.shape[1]
    qpos = PAST_LEN + jnp.arange(t)
    kpos = PAST_LEN - WINDOW + jnp.arange(WINDOW + t)
    d = qpos[:, None] - kpos[None, :]
    mask = (d >= 0) & (d <= WINDOW)
    o = sink_attend(q, k_all, v_all, mask, sinks)
    return o.reshape(n, t, N_HEADS * HEAD_DIM)


def setup_inputs(seed: int = 0) -> dict:
    key = jax.random.key(seed)
    ks = iter(jax.random.split(key, 64))
    f32 = jnp.float32

    def nrm(shape, scale):
        return jax.random.normal(next(ks), shape, f32) * scale

    def gain(shape):
        return 1.0 + nrm(shape, 0.02)

    qd = N_HEADS * HEAD_DIM
    kvd = N_KV_HEADS * HEAD_DIM
    return {
        'x_prompt': nrm((BATCH, SEQ, D_MODEL), 1.0),
        'x_sample': nrm((DEC_BATCH, DEC_SEQ, D_MODEL), 1.0),
        'state_conv': nrm((N_A_LAYERS, DEC_BATCH, CONV_CTX, D_MODEL), 0.5),
        'cache_k': nrm((DEC_BATCH, WINDOW, N_KV_HEADS, HEAD_DIM), 1.0),
        'cache_v': nrm((DEC_BATCH, WINDOW, N_KV_HEADS, HEAD_DIM), 1.0),
        'p_prompt': nrm((DEPTH, BATCH, SEQ, PLE_DIM), 1.0),
        'p_sample': nrm((DEPTH, DEC_BATCH, DEC_SEQ, PLE_DIM), 1.0),
        'norm_mix': gain((DEPTH, D_MODEL)),
        'norm_ffn': gain((DEPTH, D_MODEL)),
        'norm_ple': gain((DEPTH, D_MODEL)),
        'conv_w_pw1': nrm((N_A_LAYERS, D_MODEL, 2 * D_MODEL), D_MODEL ** -0.5),
        'conv_b_pw1': nrm((N_A_LAYERS, 2 * D_MODEL), 0.02),
        'conv_w_dw': nrm((N_A_LAYERS, CONV_WIDTH, D_MODEL), CONV_WIDTH ** -0.5),
        'conv_b_dw': nrm((N_A_LAYERS, D_MODEL), 0.02),
        'conv_ln_g': gain((N_A_LAYERS, D_MODEL)),
        'conv_ln_b': nrm((N_A_LAYERS, D_MODEL), 0.02),
        'conv_w_pw2': nrm((N_A_LAYERS, D_MODEL, D_MODEL), D_MODEL ** -0.5),
        'conv_b_pw2': nrm((N_A_LAYERS, D_MODEL), 0.02),
        'norm_kv': gain((D_MODEL,)),
        'w_k': nrm((D_MODEL, kvd), D_MODEL ** -0.5),
        'w_v': nrm((D_MODEL, kvd), D_MODEL ** -0.5),
        'k_norm': gain((HEAD_DIM,)),
        'w_q': nrm((N_B_LAYERS, D_MODEL, qd), D_MODEL ** -0.5),
        'q_norm': gain((N_B_LAYERS, HEAD_DIM)),
        'sinks': nrm((N_B_LAYERS, N_HEADS), 0.5),
        'w_o': nrm((N_B_LAYERS, qd, D_MODEL), qd ** -0.5),
        'ffn_w_gate': nrm((N_DENSE, D_MODEL, D_FF), D_MODEL ** -0.5),
        'ffn_w_up': nrm((N_DENSE, D_MODEL, D_FF), D_MODEL ** -0.5),
        'ffn_w_down': nrm((N_DENSE, D_FF, D_MODEL), D_FF ** -0.5),
        'moe_w_router': nrm((N_MOE, D_MODEL, N_EXPERTS), D_MODEL ** -0.5),
        'moe_b_router': nrm((N_MOE, N_EXPERTS), 0.01),
        'moe_w_gate': nrm((N_MOE, N_EXPERTS, D_MODEL, D_EXPERT), D_MODEL ** -0.5),
        'moe_w_up': nrm((N_MOE, N_EXPERTS, D_MODEL, D_EXPERT), D_MODEL ** -0.5),
        'moe_w_down': nrm((N_MOE, N_EXPERTS, D_EXPERT, D_MODEL), D_EXPERT ** -0.5),
        'ple_w_gate': nrm((DEPTH, D_MODEL, D_MODEL), D_MODEL ** -0.5),
        'ple_w_proj': nrm((DEPTH, PLE_DIM, D_MODEL), PLE_DIM ** -0.5),
    }


def reference(x_prompt, x_sample, state_conv, cache_k, cache_v, p_prompt, p_sample,
              norm_mix, norm_ffn, norm_ple,
              conv_w_pw1, conv_b_pw1, conv_w_dw, conv_b_dw, conv_ln_g, conv_ln_b, conv_w_pw2, conv_b_pw2,
              norm_kv, w_k, w_v, k_norm,
              w_q, q_norm, sinks, w_o,
              ffn_w_gate, ffn_w_up, ffn_w_down,
              moe_w_router, moe_b_router, moe_w_gate, moe_w_up, moe_w_down,
              ple_w_gate, ple_w_proj):

    def run(x, p, conv_past, kv_past):
        n = x.shape[0]
        h = x
        conv_states = []
        k_sh = v_sh = k_state = v_state = None
        for i in range(DEPTH):
            hn = rms_norm(h, norm_mix[i])
            if i < N_A_LAYERS:
                past = jnp.zeros((n, CONV_CTX, D_MODEL), hn.dtype) if conv_past is None else conv_past[i]
                y, st = conv_module(hn, past, conv_w_pw1[i], conv_b_pw1[i], conv_w_dw[i], conv_b_dw[i],
                                    conv_ln_g[i], conv_ln_b[i], conv_w_pw2[i], conv_b_pw2[i])
                conv_states.append(st)
            else:
                b = i - N_A_LAYERS
                q = project_q(hn, w_q[b], q_norm[b])
                if kv_past is None:
                    o = window_attn_prompt(q, k_sh, v_sh, sinks[b])
                else:
                    o = window_attn_sample(q, k_sh, v_sh, sinks[b])
                y = o @ w_o[b]
            h = h + y
            hn = rms_norm(h, norm_ffn[i])
            if i % 2 == 0:
                d = i // 2
                y = swiglu(hn, ffn_w_gate[d], ffn_w_up[d], ffn_w_down[d])
            else:
                m = i // 2
                y = moe_swiglu(hn, moe_w_router[m], moe_b_router[m], moe_w_gate[m], moe_w_up[m], moe_w_down[m])
            h = h + y
            h = h + ple_term(h, p[i], norm_ple[i], ple_w_gate[i], ple_w_proj[i])
            if i == N_A_LAYERS - 1:
                k_new, v_new = shared_kv(h, norm_kv, w_k, w_v, k_norm)
                if kv_past is None:
                    k_sh, v_sh = k_new, v_new
                else:
                    k_sh = jnp.concatenate([kv_past[0].astype(k_new.dtype), k_new], axis=1)
                    v_sh = jnp.concatenate([kv_past[1].astype(v_new.dtype), v_new], axis=1)
                k_state, v_state = k_sh[:, -WINDOW:], v_sh[:, -WINDOW:]
        return h, jnp.stack(conv_states), k_state, v_state

    y_prompt, conv_prompt, k_prompt, v_prompt = run(x_prompt, p_prompt, None, None)
    y_sample, conv_sample, k_sample, v_sample = run(x_sample, p_sample, state_conv, (cache_k, cache_v))
    return (y_prompt, y_sample, conv_prompt, k_prompt, v_prompt, conv_sample, k_sample, v_sample)
```

```python
import functools

import jax
import jax.numpy as jnp
from jax import lax
from jax.experimental import pallas as pl
from jax.experimental.pallas import tpu as pltpu

D_MODEL = 1024
CONV_WIDTH = 31
CONV_CTX = CONV_WIDTH - 1
HEAD_DIM = 64
N_HEADS = 16
N_KV_HEADS = 4
GROUP = N_HEADS // N_KV_HEADS
KV_DIM = N_KV_HEADS * HEAD_DIM
WINDOW = 128
D_FF = 2816
N_EXPERTS = 8
D_EXPERT = 3584
PLE_DIM = 256
EPS = 1e-6

BF16 = jnp.bfloat16
F32 = jnp.float32

LANES = 128
SUBLANES = 8
ROW_TILES = D_MODEL // LANES
VMEM_LIMIT = 56 * 1024 * 1024

CONV_PAD = 32
CONV_ROWS = 64
TOK_TILE = 512
MOE_TILE = 512
MOE_F_TILE = 512
COMBINE_TILE = 256
DISPATCH_TILE = 512
NEG_BIG = -1e30


def _params(*sem):
    return pltpu.CompilerParams(dimension_semantics=sem, vmem_limit_bytes=VMEM_LIMIT)


def _const(shape):
    nd = len(shape)
    return pl.BlockSpec(shape, lambda *_: (0,) * nd, pipeline_mode=pl.Buffered(1))


def _rms(x, g):
    return x * lax.rsqrt(jnp.mean(x * x, axis=-1, keepdims=True) + EPS) * g


def _dot(a, w):
    return jnp.dot(a.astype(BF16), w, preferred_element_type=F32)


def _sigmoid(x):
    return 1.0 / (1.0 + jnp.exp(-x))


def _silu(x):
    return x * _sigmoid(x)


def _layer_norm(x, g, b):
    mu = jnp.mean(x, axis=-1, keepdims=True)
    xc = x - mu
    var = jnp.mean(xc * xc, axis=-1, keepdims=True)
    return xc * lax.rsqrt(var + EPS) * g + b


def _head_rms(x, g):
    t, w = x.shape
    lane = lax.broadcasted_iota(jnp.int32, (t, LANES), 1)
    lo = lane < HEAD_DIM
    outs = []
    for c in range(w // LANES):
        blk = x[:, c * LANES:(c + 1) * LANES]
        sq = blk * blk
        s_lo = jnp.sum(jnp.where(lo, sq, 0.0), axis=-1, keepdims=True)
        s_hi = jnp.sum(jnp.where(lo, 0.0, sq), axis=-1, keepdims=True)
        inv = jnp.where(lo, lax.rsqrt(s_lo / HEAD_DIM + EPS), lax.rsqrt(s_hi / HEAD_DIM + EPS))
        outs.append(blk * inv)
    return jnp.concatenate(outs, axis=1) * g


def _rows_to_tiles(ref, x):
    t = x.shape[0]
    for s in range(ROW_TILES):
        ref[pl.ds(s, t, stride=ROW_TILES), :] = x[:, s * LANES:(s + 1) * LANES]


def _tiles_to_rows(ref, t):
    return jnp.concatenate([ref[pl.ds(s, t, stride=ROW_TILES), :] for s in range(ROW_TILES)], axis=1)


def _conv_taps(uext_ref, wdw_ref, bdw_ref, c_ref, tt):
    def chunk(ci, carry):
        r0 = pl.multiple_of(ci * CONV_ROWS, CONV_ROWS)
        for lt in range(ROW_TILES):
            cols = slice(lt * LANES, (lt + 1) * LANES)
            win = uext_ref[pl.ds(r0, CONV_ROWS + CONV_PAD), cols]
            acc = jnp.broadcast_to(bdw_ref[:, cols], (CONV_ROWS, LANES))
            for b in range(SUBLANES):
                wb = win if b == 0 else win[b:b + CONV_ROWS + CONV_PAD - SUBLANES]
                for a in range(CONV_PAD // SUBLANES + 1):
                    j = a * SUBLANES + b - (CONV_PAD - CONV_CTX)
                    if 0 <= j < CONV_WIDTH:
                        acc = acc + wdw_ref[j:j + 1, cols] * wb[a * SUBLANES:a * SUBLANES + CONV_ROWS]
            c_ref[pl.ds(r0, CONV_ROWS), cols] = acc
        return carry
    lax.fori_loop(0, tt // CONV_ROWS, chunk, 0)


def _conv_prompt_kernel(x_ref, nm_ref, w1_ref, b1_ref, wdw_ref, bdw_ref, lng_ref, lnb_ref, w2_ref, b2_ref,
                        h_ref, st_ref, uext_ref, c_ref, *, tt):
    t = pl.program_id(1)

    @pl.when(t == 0)
    def _():
        uext_ref[0:CONV_PAD, :] = jnp.zeros((CONV_PAD, D_MODEL), F32)

    @pl.when(t > 0)
    def _():
        uext_ref[0:CONV_PAD, :] = uext_ref[tt:tt + CONV_PAD, :]

    x = x_ref[0]
    z = _dot(_rms(x, nm_ref[...]), w1_ref[...]) + b1_ref[...]
    u = z[:, :D_MODEL] * _sigmoid(z[:, D_MODEL:])
    uext_ref[CONV_PAD:CONV_PAD + tt, :] = u
    _conv_taps(uext_ref, wdw_ref, bdw_ref, c_ref, tt)
    c = _silu(_layer_norm(c_ref[...], lng_ref[...], lnb_ref[...]))
    h_ref[0] = x + _dot(c, w2_ref[...]) + b2_ref[...]
    st_ref[0] = uext_ref[CONV_PAD + tt - CONV_CTX:CONV_PAD + tt, :]


def _conv_prompt(x, nm, w1, b1, wdw, bdw, lng, lnb, w2, b2):
    n, t, d = x.shape
    tt = TOK_TILE
    weights = (nm, w1, b1, wdw, bdw, lng, lnb, w2, b2)
    return pl.pallas_call(
        functools.partial(_conv_prompt_kernel, tt=tt),
        grid=(n, t // tt),
        in_specs=[pl.BlockSpec((1, tt, d), lambda b, i: (b, i, 0))] + [_const(w.shape) for w in weights],
        out_specs=[pl.BlockSpec((1, tt, d), lambda b, i: (b, i, 0)),
                   pl.BlockSpec((1, CONV_CTX, d), lambda b, i: (b, 0, 0))],
        out_shape=[jax.ShapeDtypeStruct((n, t, d), F32), jax.ShapeDtypeStruct((n, CONV_CTX, d), F32)],
        scratch_shapes=[pltpu.VMEM((CONV_PAD + tt, d), F32), pltpu.VMEM((tt, d), F32)],
        compiler_params=_params("arbitrary", "arbitrary"),
        name="conv_prompt",
    )(x, *weights)


def _conv_sample_kernel(x_ref, past_ref, nm_ref, w1_ref, b1_ref, wdw_ref, bdw_ref, lng_ref, lnb_ref,
                        w2_ref, b2_ref, h_ref, st_ref, c_ref, *, nb):
    x = x_ref[...]
    z = _dot(_rms(x, nm_ref[...]), w1_ref[...]) + b1_ref[...]
    u = z[:, :D_MODEL] * _sigmoid(z[:, D_MODEL:])
    w_past = wdw_ref[0:CONV_CTX, :]
    for i in range(nb):
        past = past_ref[i]
        c_ref[i:i + 1, :] = jnp.sum(past * w_past, axis=0, keepdims=True)
        st_ref[i, 0:CONV_CTX - 1, :] = past[1:CONV_CTX, :]
        st_ref[i, CONV_CTX - 1:CONV_CTX, :] = u[i:i + 1, :]
    c = c_ref[...] + wdw_ref[CONV_CTX:CONV_WIDTH, :] * u + bdw_ref[...]
    c = _silu(_layer_norm(c, lng_ref[...], lnb_ref[...]))
    h_ref[...] = x + _dot(c, w2_ref[...]) + b2_ref[...]


def _conv_sample(x, past, nm, w1, b1, wdw, bdw, lng, lnb, w2, b2):
    n, d = x.shape
    nb = 16
    weights = (nm, w1, b1, wdw, bdw, lng, lnb, w2, b2)
    return pl.pallas_call(
        functools.partial(_conv_sample_kernel, nb=nb),
        grid=(n // nb,),
        in_specs=[pl.BlockSpec((nb, d), lambda i: (i, 0)),
                  pl.BlockSpec((nb, CONV_CTX, d), lambda i: (i, 0, 0))] + [_const(w.shape) for w in weights],
        out_specs=[pl.BlockSpec((nb, d), lambda i: (i, 0)),
                   pl.BlockSpec((nb, CONV_CTX, d), lambda i: (i, 0, 0))],
        out_shape=[jax.ShapeDtypeStruct((n, d), F32), jax.ShapeDtypeStruct((n, CONV_CTX, d), F32)],
        scratch_shapes=[pltpu.VMEM((nb, d), F32)],
        compiler_params=_params("arbitrary"),
        name="conv_sample",
    )(x, past, *weights)


def _ffn_kernel(h_ref, g_ref, wg_ref, wu_ref, wd_ref, o_ref):
    h = h_ref[...]
    hn = _rms(h, g_ref[...]).astype(BF16)
    a = jnp.dot(hn, wg_ref[...], preferred_element_type=F32)
    b = jnp.dot(hn, wu_ref[...], preferred_element_type=F32)
    o_ref[...] = h + _dot(_silu(a) * b, wd_ref[...])


def _ffn(h, g, wg, wu, wd):
    m, d = h.shape
    tm = min(TOK_TILE // 2, m)
    weights = (g, wg, wu, wd)
    return pl.pallas_call(
        _ffn_kernel,
        grid=(m // tm,),
        in_specs=[pl.BlockSpec((tm, d), lambda i: (i, 0))] + [_const(w.shape) for w in weights],
        out_specs=pl.BlockSpec((tm, d), lambda i: (i, 0)),
        out_shape=jax.ShapeDtypeStruct((m, d), F32),
        compiler_params=_params("arbitrary"),
        name="dense_ffn",
    )(h, *weights)


def _ple(h, p, g, w_gate, w_proj):
    return h + _sigmoid(_dot(_rms(h, g), w_gate)) * _dot(p, w_proj)


def _ple_kvq_kernel(h_ref, p_ref, gple_ref, wpg_ref, wpp_ref, gkv_ref, wk_ref, wv_ref, kn_ref,
                    gmix_ref, wq_ref, qn_ref, h3_ref, k_ref, v_ref, q_ref):
    h3 = _ple(h_ref[...], p_ref[...], gple_ref[...], wpg_ref[...], wpp_ref[...])
    h3_ref[...] = h3
    s = _rms(h3, gkv_ref[...]).astype(BF16)
    k_ref[...] = _head_rms(jnp.dot(s, wk_ref[...], preferred_element_type=F32), kn_ref[...])
    v_ref[...] = jnp.dot(s, wv_ref[...], preferred_element_type=F32)
    q = _dot(_rms(h3, gmix_ref[...]), wq_ref[...])
    q_ref[...] = _head_rms(q, qn_ref[...]).astype(BF16)


def _ple_kvq(h, p, gple, wpg, wpp, gkv, wk, wv, kn, gmix, wq, qn):
    m, d = h.shape
    tm = min(TOK_TILE, m)
    weights = (gple, wpg, wpp, gkv, wk, wv, kn, gmix, wq, qn)
    row = lambda w: pl.BlockSpec((tm, w), lambda i: (i, 0))
    return pl.pallas_call(
        _ple_kvq_kernel,
        grid=(m // tm,),
        in_specs=[row(d), row(PLE_DIM)] + [_const(w.shape) for w in weights],
        out_specs=[row(d), row(KV_DIM), row(KV_DIM), row(d)],
        out_shape=[jax.ShapeDtypeStruct((m, d), F32), jax.ShapeDtypeStruct((m, KV_DIM), F32),
                   jax.ShapeDtypeStruct((m, KV_DIM), F32), jax.ShapeDtypeStruct((m, d), BF16)],
        compiler_params=_params("arbitrary"),
        name="ple_kv_q",
    )(h, p, *weights)


def _softmax_sink(s, sink):
    m = jnp.maximum(jnp.max(s, axis=-1, keepdims=True), sink)
    e = jnp.exp(s - m)
    den = jnp.sum(e, axis=-1, keepdims=True) + jnp.exp(sink - m)
    return e / den


def _attn_prompt_kernel(q_ref, kp_ref, kc_ref, vp_ref, vc_ref, sink_ref, o_ref):
    n = pl.program_id(1)
    q = q_ref[0]
    k = jnp.concatenate([kp_ref[0], kc_ref[0]], axis=0).astype(BF16)
    v = jnp.concatenate([vp_ref[0], vc_ref[0]], axis=0).astype(BF16)
    qi = lax.broadcasted_iota(jnp.int32, (WINDOW, 2 * WINDOW), 0)
    kj = lax.broadcasted_iota(jnp.int32, (WINDOW, 2 * WINDOW), 1)
    dist = qi + WINDOW - kj
    first_key = jnp.where(n > 0, 0, WINDOW)
    mask = (dist >= 0) & (dist <= WINDOW) & (kj >= first_key)
    outs = []
    for h in range(N_HEADS):
        kh = h // GROUP
        qh = q[:, h * HEAD_DIM:(h + 1) * HEAD_DIM]
        kk = k[:, kh * HEAD_DIM:(kh + 1) * HEAD_DIM]
        vv = v[:, kh * HEAD_DIM:(kh + 1) * HEAD_DIM]
        s = lax.dot_general(qh, kk, (((1,), (1,)), ((), ())), preferred_element_type=F32) * (HEAD_DIM ** -0.5)
        s = jnp.where(mask, s, -jnp.inf)
        p = _softmax_sink(s, sink_ref[h])
        outs.append(jnp.dot(p.astype(BF16), vv, preferred_element_type=F32))
    o_ref[0] = jnp.concatenate(outs, axis=1).astype(BF16)


def _attn_prompt(q, k, v, sinks):
    n, t, d = q.shape
    nb = t // WINDOW
    cur = lambda w: pl.BlockSpec((1, WINDOW, w), lambda b, i: (b, i, 0))
    prev = lambda w: pl.BlockSpec((1, WINDOW, w), lambda b, i: (b, jnp.maximum(i - 1, 0), 0))
    return pl.pallas_call(
        _attn_prompt_kernel,
        grid=(n, nb),
        in_specs=[cur(d), prev(KV_DIM), cur(KV_DIM), prev(KV_DIM), cur(KV_DIM),
                  pl.BlockSpec(memory_space=pltpu.SMEM)],
        out_specs=cur(d),
        out_shape=jax.ShapeDtypeStruct((n, t, d), BF16),
        compiler_params=_params("arbitrary", "arbitrary"),
        name="attn_prompt",
    )(q, k, k, v, v, sinks)


def _attn_sample_kernel(q_ref, kc_ref, vc_ref, kn_ref, vn_ref, sink_ref, o_ref, ks_ref, vs_ref, *, nb):
    for i in range(nb):
        kc = kc_ref[i]
        vc = vc_ref[i]
        kn = kn_ref[i:i + 1, :]
        vn = vn_ref[i:i + 1, :]
        ks_ref[i, 0:WINDOW - 1, :] = kc[1:WINDOW, :]
        ks_ref[i, WINDOW - 1:WINDOW, :] = kn
        vs_ref[i, 0:WINDOW - 1, :] = vc[1:WINDOW, :]
        vs_ref[i, WINDOW - 1:WINDOW, :] = vn
        kcb = kc.astype(BF16)
        vcb = vc.astype(BF16)
        knb = kn.astype(BF16).astype(F32)
        vnb = vn.astype(BF16).astype(F32)
        for kh in range(N_KV_HEADS):
            cols = slice(kh * HEAD_DIM, (kh + 1) * HEAD_DIM)
            q4 = jnp.concatenate(
                [q_ref[i:i + 1, (kh * GROUP + g) * HEAD_DIM:(kh * GROUP + g + 1) * HEAD_DIM] for g in range(GROUP)],
                axis=0)
            sink = sink_ref[kh * GROUP:(kh + 1) * GROUP, :]
            scale = HEAD_DIM ** -0.5
            s_c = lax.dot_general(q4, kcb[:, cols], (((1,), (1,)), ((), ())), preferred_element_type=F32) * scale
            s_n = jnp.sum(q4.astype(F32) * knb[:, cols], axis=-1, keepdims=True) * scale
            m = jnp.maximum(jnp.maximum(jnp.max(s_c, axis=-1, keepdims=True), s_n), sink)
            e_c = jnp.exp(s_c - m)
            e_n = jnp.exp(s_n - m)
            den = jnp.sum(e_c, axis=-1, keepdims=True) + e_n + jnp.exp(sink - m)
            o4 = jnp.dot((e_c / den).astype(BF16), vcb[:, cols], preferred_element_type=F32)
            o4 = o4 + (e_n / den).astype(BF16).astype(F32) * vnb[:, cols]
            for g in range(GROUP):
                h = kh * GROUP + g
                o_ref[i:i + 1, h * HEAD_DIM:(h + 1) * HEAD_DIM] = o4[g:g + 1, :].astype(BF16)


def _attn_sample(q, kc, vc, kn, vn, sinks_col):
    n, d = q.shape
    nb = 8
    row = lambda w: pl.BlockSpec((nb, w), lambda i: (i, 0))
    cache = pl.BlockSpec((nb, WINDOW, KV_DIM), lambda i: (i, 0, 0))
    return pl.pallas_call(
        functools.partial(_attn_sample_kernel, nb=nb),
        grid=(n // nb,),
        in_specs=[row(d), cache, cache, row(KV_DIM), row(KV_DIM), _const(sinks_col.shape)],
        out_specs=[row(d), cache, cache],
        out_shape=[jax.ShapeDtypeStruct((n, d), BF16),
                   jax.ShapeDtypeStruct((n, WINDOW, KV_DIM), F32),
                   jax.ShapeDtypeStruct((n, WINDOW, KV_DIM), F32)],
        compiler_params=_params("arbitrary"),
        name="attn_sample",
    )(q, kc, vc, kn, vn, sinks_col)


def _oproj_router_kernel(o_ref, h_ref, wo_ref, g_ref, wr_ref, br_ref,
                         h4_ref, hn_ref, route_ref, cnt_ref, carry_ref, *, tm):
    i = pl.program_id(0)

    @pl.when(i == 0)
    def _():
        carry_ref[...] = jnp.zeros((1, LANES), F32)

    h4 = h_ref[...] + jnp.dot(o_ref[...], wo_ref[...], preferred_element_type=F32)
    h4_ref[...] = h4
    hn = _rms(h4, g_ref[...])
    _rows_to_tiles(hn_ref, hn)
    logits = jnp.dot(hn, wr_ref[...], preferred_element_type=F32, precision=lax.Precision.HIGHEST) + br_ref[...]
    lane = lax.broadcasted_iota(jnp.int32, (tm, LANES), 1).astype(F32)
    m1 = jnp.max(logits, axis=-1, keepdims=True)
    i1 = jnp.min(jnp.where(logits == m1, lane, float(LANES)), axis=-1, keepdims=True)
    rest = jnp.where(lane == i1, -jnp.inf, logits)
    m2 = jnp.max(rest, axis=-1, keepdims=True)
    i2 = jnp.min(jnp.where(rest == m2, lane, float(LANES)), axis=-1, keepdims=True)
    e2 = jnp.exp(m2 - m1)
    w1 = 1.0 / (1.0 + e2)
    w2 = e2 / (1.0 + e2)
    oh1 = lane == i1
    oh2 = lane == i2
    sel = jnp.where(oh1, 1.0, 0.0) + jnp.where(oh2, 1.0, 0.0)
    row = lax.broadcasted_iota(jnp.int32, (tm, tm), 0)
    col = lax.broadcasted_iota(jnp.int32, (tm, tm), 1)
    tri = jnp.where(row > col, 1.0, 0.0).astype(BF16)
    before = jnp.dot(tri, sel.astype(BF16), preferred_element_type=F32) + carry_ref[...]
    r1 = jnp.sum(jnp.where(oh1, before, 0.0), axis=-1, keepdims=True)
    r2 = jnp.sum(jnp.where(oh2, before, 0.0), axis=-1, keepdims=True)
    carry_ref[...] = carry_ref[...] + jnp.sum(sel, axis=0, keepdims=True)
    cnt_ref[...] = carry_ref[...]
    fields = (i1, i2, r1, r2, w1, w2)
    route = jnp.zeros((tm, LANES), F32)
    for f, val in enumerate(fields):
        route = jnp.where(lane == float(f), val, route)
    route_ref[...] = route


def _oproj_router(o, h, wo, g, wr, br):
    m, d = h.shape
    tm = min(TOK_TILE, m)
    weights = (wo, g, wr, br)
    row = lambda w: pl.BlockSpec((tm, w), lambda i: (i, 0))
    return pl.pallas_call(
        functools.partial(_oproj_router_kernel, tm=tm),
        grid=(m // tm,),
        in_specs=[row(d), row(d)] + [_const(w.shape) for w in weights],
        out_specs=[row(d), pl.BlockSpec((tm * ROW_TILES, LANES), lambda i: (i, 0)), row(LANES),
                   pl.BlockSpec((1, LANES), lambda i: (0, 0))],
        out_shape=[jax.ShapeDtypeStruct((m, d), F32), jax.ShapeDtypeStruct((m * ROW_TILES, LANES), F32),
                   jax.ShapeDtypeStruct((m, LANES), F32), jax.ShapeDtypeStruct((1, LANES), F32)],
        scratch_shapes=[pltpu.VMEM((1, LANES), F32)],
        compiler_params=_params("arbitrary"),
        name="oproj_router",
    )(o, h, *weights)


def _row(ref, r):
    return ref.at[pl.ds(pl.multiple_of(r * ROW_TILES, ROW_TILES), ROW_TILES), :]


def _dispatch_kernel(pos_ref, src_ref, dst_in_ref, dst_ref, sem, *, tb, nsteps):
    del dst_in_ref
    i = pl.program_id(0)

    def issue(t, carry):
        tok = i * tb + t
        for k in range(2):
            pltpu.make_async_copy(_row(src_ref, tok), _row(dst_ref, pos_ref[2 * tok + k]), sem).start()
        return carry
    lax.fori_loop(0, tb, issue, 0)

    def drain(t, carry):
        for k in range(2):
            pltpu.make_async_copy(_row(src_ref, 0), _row(dst_ref, 0), sem).wait()
        return carry
    lax.fori_loop(0, tb, drain, 0)


def _dispatch(pos, src, dst):
    m = src.shape[0] // ROW_TILES
    tb = min(DISPATCH_TILE, m)
    nsteps = m // tb
    return pl.pallas_call(
        functools.partial(_dispatch_kernel, tb=tb, nsteps=nsteps),
        grid_spec=pltpu.PrefetchScalarGridSpec(
            num_scalar_prefetch=1,
            grid=(nsteps,),
            in_specs=[pl.BlockSpec(memory_space=pl.ANY), pl.BlockSpec(memory_space=pl.ANY)],
            out_specs=pl.BlockSpec(memory_space=pl.ANY),
            scratch_shapes=[pltpu.SemaphoreType.DMA(())],
        ),
        out_shape=jax.ShapeDtypeStruct(dst.shape, dst.dtype),
        input_output_aliases={2: 0},
        compiler_params=_params("arbitrary"),
        name="moe_dispatch",
    )(pos, src, dst)


def _moe_kernel(te_ref, nv_ref, x_ref, wg_ref, wu_ref, wd_ref, y_ref, xb_ref, acc_ref, *, tm, nf):
    i = pl.program_id(0)
    f = pl.program_id(1)
    valid = i < nv_ref[0]

    @pl.when(jnp.logical_and(valid, f == 0))
    def _():
        xb_ref[...] = _tiles_to_rows(x_ref, tm).astype(BF16)
        acc_ref[...] = jnp.zeros((tm, D_MODEL), F32)

    @pl.when(valid)
    def _():
        x = xb_ref[...]
        a = jnp.dot(x, wg_ref[0], preferred_element_type=F32)
        b = jnp.dot(x, wu_ref[0], preferred_element_type=F32)
        acc_ref[...] += _dot(_silu(a) * b, wd_ref[0])

    @pl.when(f == nf - 1)
    def _():
        _rows_to_tiles(y_ref, jnp.where(valid, acc_ref[...], 0.0))


def _moe(tile_expert, n_valid, x_sorted, wg, wu, wd):
    rows = x_sorted.shape[0] // ROW_TILES
    tm, tf = MOE_TILE, MOE_F_TILE
    nt, nf = rows // tm, D_EXPERT // tf

    def fidx(i, f, nv):
        return jnp.where(i < nv[0], f, nf - 1)
    return pl.pallas_call(
        functools.partial(_moe_kernel, tm=tm, nf=nf),
        grid_spec=pltpu.PrefetchScalarGridSpec(
            num_scalar_prefetch=2,
            grid=(nt, nf),
            in_specs=[pl.BlockSpec((tm * ROW_TILES, LANES), lambda i, f, te, nv: (i, 0)),
                      pl.BlockSpec((1, D_MODEL, tf), lambda i, f, te, nv: (te[i], 0, fidx(i, f, nv))),
                      pl.BlockSpec((1, D_MODEL, tf), lambda i, f, te, nv: (te[i], 0, fidx(i, f, nv))),
                      pl.BlockSpec((1, tf, D_MODEL), lambda i, f, te, nv: (te[i], fidx(i, f, nv), 0))],
            out_specs=pl.BlockSpec((tm * ROW_TILES, LANES), lambda i, f, te, nv: (i, 0)),
            scratch_shapes=[pltpu.VMEM((tm, D_MODEL), BF16), pltpu.VMEM((tm, D_MODEL), F32)],
        ),
        out_shape=jax.ShapeDtypeStruct(x_sorted.shape, F32),
        compiler_params=_params("arbitrary", "arbitrary"),
        name="moe_grouped",
    )(tile_expert, n_valid, x_sorted, wg, wu, wd)


def _combine_kernel(pos_ref, y_ref, h_ref, route_ref, p_ref, g_ref, wpg_ref, wpp_ref, o_ref,
                    buf_ref, sem, *, tb, nsteps):
    i = pl.program_id(0)

    def issue(step, slot):
        def body(t, carry):
            tok = step * tb + t
            for k in range(2):
                pltpu.make_async_copy(_row(y_ref, pos_ref[2 * tok + k]), _row(buf_ref.at[slot, k], t),
                                      sem.at[slot]).start()
            return carry
        lax.fori_loop(0, tb, body, 0)

    def drain(slot):
        def body(t, carry):
            for k in range(2):
                pltpu.make_async_copy(_row(y_ref, 0), _row(buf_ref.at[slot, k], 0), sem.at[slot]).wait()
            return carry
        lax.fori_loop(0, tb, body, 0)

    slot = i % 2

    @pl.when(i == 0)
    def _():
        issue(0, 0)

    @pl.when(i + 1 < nsteps)
    def _():
        issue(i + 1, 1 - slot)

    drain(slot)
    route = route_ref[...]
    w1 = route[:, 4:5]
    w2 = route[:, 5:6]
    y1 = _tiles_to_rows(buf_ref.at[slot, 0], tb)
    y2 = _tiles_to_rows(buf_ref.at[slot, 1], tb)
    h5 = h_ref[...] + (w1 * y1 + w2 * y2)
    o_ref[...] = _ple(h5, p_ref[...], g_ref[...], wpg_ref[...], wpp_ref[...])


def _combine(pos, y_sorted, h, route, p, g, wpg, wpp):
    m, d = h.shape
    tb = min(COMBINE_TILE, m)
    nsteps = m // tb
    weights = (g, wpg, wpp)
    row = lambda w: pl.BlockSpec((tb, w), lambda i, pos: (i, 0))
    return pl.pallas_call(
        functools.partial(_combine_kernel, tb=tb, nsteps=nsteps),
        grid_spec=pltpu.PrefetchScalarGridSpec(
            num_scalar_prefetch=1,
            grid=(nsteps,),
            in_specs=[pl.BlockSpec(memory_space=pl.ANY), row(d), row(LANES), row(PLE_DIM)]
                     + [pl.BlockSpec(w.shape, lambda i, pos: (0, 0), pipeline_mode=pl.Buffered(1)) for w in weights],
            out_specs=row(d),
            scratch_shapes=[pltpu.VMEM((2, 2, tb * ROW_TILES, LANES), F32), pltpu.SemaphoreType.DMA((2,))],
        ),
        out_shape=jax.ShapeDtypeStruct((m, d), F32),
        compiler_params=_params("arbitrary"),
        name="moe_combine",
    )(pos, y_sorted, h, route, p, *weights)


def _row2(v):
    return v.reshape(1, -1).astype(F32)


def kernel(x_prompt, x_sample, state_conv, cache_k, cache_v, p_prompt, p_sample, norm_mix, norm_ffn, norm_ple,
           conv_w_pw1, conv_b_pw1, conv_w_dw, conv_b_dw, conv_ln_g, conv_ln_b, conv_w_pw2, conv_b_pw2,
           norm_kv, w_k, w_v, k_norm, w_q, q_norm, sinks, w_o, ffn_w_gate, ffn_w_up, ffn_w_down,
           moe_w_router, moe_b_router, moe_w_gate, moe_w_up, moe_w_down, ple_w_gate, ple_w_proj):
    nb, t, d = x_prompt.shape
    ns = x_sample.shape[0]
    mp = nb * t
    bf = lambda w: w.astype(BF16)

    conv_w = (_row2(norm_mix[0]), bf(conv_w_pw1[0]), _row2(conv_b_pw1[0]), conv_w_dw[0], _row2(conv_b_dw[0]),
              _row2(conv_ln_g[0]), _row2(conv_ln_b[0]), bf(conv_w_pw2[0]), _row2(conv_b_pw2[0]))
    ffn_w = (_row2(norm_ffn[0]), bf(ffn_w_gate[0]), bf(ffn_w_up[0]), bf(ffn_w_down[0]))
    kvq_w = (_row2(norm_ple[0]), bf(ple_w_gate[0]), bf(ple_w_proj[0]), _row2(norm_kv), bf(w_k), bf(w_v),
             _row2(jnp.tile(k_norm, N_KV_HEADS)), _row2(norm_mix[1]), bf(w_q[0]), _row2(jnp.tile(q_norm[0], N_HEADS)))
    wr = jnp.zeros((d, LANES), F32).at[:, :N_EXPERTS].set(moe_w_router[0])
    br = jnp.full((1, LANES), NEG_BIG, F32).at[0, :N_EXPERTS].set(moe_b_router[0])
    router_w = (bf(w_o[0]), _row2(norm_ffn[1]), wr, br)
    ple1_w = (_row2(norm_ple[1]), bf(ple_w_gate[1]), bf(ple_w_proj[1]))

    h1p, conv_p = _conv_prompt(x_prompt, *conv_w)
    h3p, kp, vp, qp = _ple_kvq(_ffn(h1p.reshape(mp, d), *ffn_w), p_prompt[0].reshape(mp, PLE_DIM), *kvq_w)
    h1s, conv_s = _conv_sample(x_sample.reshape(ns, d), state_conv[0], *conv_w)
    h3s, ks, vs, qs = _ple_kvq(_ffn(h1s, *ffn_w), p_sample[0].reshape(ns, PLE_DIM), *kvq_w)

    kp3 = kp.reshape(nb, t, KV_DIM)
    vp3 = vp.reshape(nb, t, KV_DIM)
    op = _attn_prompt(qp.reshape(nb, t, d), kp3, vp3, sinks[0].astype(F32))
    os_, k_state_s, v_state_s = _attn_sample(qs, cache_k.reshape(ns, WINDOW, KV_DIM),
                                             cache_v.reshape(ns, WINDOW, KV_DIM), ks, vs,
                                             sinks[0].astype(F32).reshape(N_HEADS, 1))

    h4p, hnp, route_p, cnt_p = _oproj_router(op.reshape(mp, d), h3p, *router_w)
    h4s, hns, route_s, cnt_s = _oproj_router(os_, h3s, *router_w)

    cnt_p = cnt_p[0, :N_EXPERTS].astype(jnp.int32)
    cnt_s = cnt_s[0, :N_EXPERTS].astype(jnp.int32)
    group = (cnt_p + cnt_s + MOE_TILE - 1) // MOE_TILE * MOE_TILE
    ends = jnp.cumsum(group)
    starts = ends - group
    n_tiles = (2 * (mp + ns) + N_EXPERTS * (MOE_TILE - 1)) // MOE_TILE
    n_valid = (ends[-1] // MOE_TILE).astype(jnp.int32)
    tile_start = jnp.minimum(jnp.arange(n_tiles, dtype=jnp.int32), n_valid - 1) * MOE_TILE
    tile_expert = jnp.sum(tile_start[:, None] >= ends[None, :], axis=1).astype(jnp.int32)

    def positions(route, base):
        idx = route[:, 0:2].astype(jnp.int32)
        rank = route[:, 2:4].astype(jnp.int32)
        return (starts[idx] + base[idx] + rank).reshape(-1)
    pos_p = positions(route_p, jnp.zeros_like(cnt_p))
    pos_s = positions(route_s, cnt_p)

    x_sorted = jnp.zeros((n_tiles * MOE_TILE * ROW_TILES, LANES), F32)
    x_sorted = _dispatch(pos_p, hnp, x_sorted)
    x_sorted = _dispatch(pos_s, hns, x_sorted)
    y_sorted = _moe(tile_expert, n_valid.reshape(1), x_sorted, bf(moe_w_gate[0]), bf(moe_w_up[0]), bf(moe_w_down[0]))

    y_p = _combine(pos_p, y_sorted, h4p, route_p, p_prompt[1].reshape(mp, PLE_DIM), *ple1_w)
    y_s = _combine(pos_s, y_sorted, h4s, route_s, p_sample[1].reshape(ns, PLE_DIM), *ple1_w)

    kv4 = lambda a, n: a.reshape(n, WINDOW, N_KV_HEADS, HEAD_DIM)
    return (y_p.reshape(nb, t, d), y_s.reshape(ns, 1, d), conv_p[None],
            kv4(kp3[:, t - WINDOW:], nb), kv4(vp3[:, t - WINDOW:], nb),
            conv_s[None], kv4(k_state_s, ns), kv4(v_state_s, ns))
```

```python
import functools

import jax
import jax.numpy as jnp
from jax import lax
from jax.experimental import pallas as pl
from jax.experimental.pallas import tpu as pltpu

D_MODEL = 1024
CONV_WIDTH = 31
CONV_CTX = CONV_WIDTH - 1
HEAD_DIM = 64
N_HEADS = 16
N_KV_HEADS = 4
GROUP = N_HEADS // N_KV_HEADS
KV_DIM = N_KV_HEADS * HEAD_DIM
WINDOW = 128
D_FF = 2816
N_EXPERTS = 8
D_EXPERT = 3584
PLE_DIM = 256
EPS = 1e-6

BF16 = jnp.bfloat16
F32 = jnp.float32

LANES = 128
SUBLANES = 8
ROW_TILES = D_MODEL // LANES
VMEM_LIMIT = 56 * 1024 * 1024

CONV_PAD = 32
CONV_ROWS = 64
TOK_TILE = 512
MOE_TILE = 512
MOE_F_TILE = 512
COMBINE_TILE = 256
DISPATCH_TILE = 512
NEG_BIG = -1e30


def _params(*sem):
    return pltpu.CompilerParams(dimension_semantics=sem, vmem_limit_bytes=VMEM_LIMIT)


def _const(shape):
    nd = len(shape)
    return pl.BlockSpec(shape, lambda *_: (0,) * nd, pipeline_mode=pl.Buffered(1))


def _rms(x, g):
    return x * lax.rsqrt(jnp.mean(x * x, axis=-1, keepdims=True) + EPS) * g


def _dot(a, w):
    return jnp.dot(a.astype(BF16), w, preferred_element_type=F32)


def _sigmoid(x):
    return 1.0 / (1.0 + jnp.exp(-x))


def _silu(x):
    return x * _sigmoid(x)


def _layer_norm(x, g, b):
    mu = jnp.mean(x, axis=-1, keepdims=True)
    xc = x - mu
    var = jnp.mean(xc * xc, axis=-1, keepdims=True)
    return xc * lax.rsqrt(var + EPS) * g + b


def _head_rms(x, g):
    t, w = x.shape
    lane = lax.broadcasted_iota(jnp.int32, (t, LANES), 1)
    lo = lane < HEAD_DIM
    outs = []
    for c in range(w // LANES):
        blk = x[:, c * LANES:(c + 1) * LANES]
        sq = blk * blk
        s_lo = jnp.sum(jnp.where(lo, sq, 0.0), axis=-1, keepdims=True)
        s_hi = jnp.sum(jnp.where(lo, 0.0, sq), axis=-1, keepdims=True)
        inv = jnp.where(lo, lax.rsqrt(s_lo / HEAD_DIM + EPS), lax.rsqrt(s_hi / HEAD_DIM + EPS))
        outs.append(blk * inv)
    return jnp.concatenate(outs, axis=1) * g


def _rows_to_tiles(ref, x):
    t = x.shape[0]
    for s in range(ROW_TILES):
        ref[pl.ds(s, t, stride=ROW_TILES), :] = x[:, s * LANES:(s + 1) * LANES]


def _tiles_to_rows(ref, t):
    return jnp.concatenate([ref[pl.ds(s, t, stride=ROW_TILES), :] for s in range(ROW_TILES)], axis=1)


def _conv_taps(uext_ref, wdw_ref, bdw_ref, c_ref, tt):
    win_rows = CONV_ROWS + CONV_PAD

    def chunk(ci, carry):
        r0 = pl.multiple_of(ci * CONV_ROWS, CONV_ROWS)
        for lt in range(ROW_TILES):
            cols = slice(lt * LANES, (lt + 1) * LANES)
            win = uext_ref[pl.ds(r0, win_rows), cols]
            acc = jnp.broadcast_to(bdw_ref[:, cols], (CONV_ROWS, LANES))
            for b in range(SUBLANES):
                wb = win if b == 0 else pltpu.roll(win, win_rows - b, axis=0)
                for a in range(CONV_PAD // SUBLANES + 1):
                    j = a * SUBLANES + b - (CONV_PAD - CONV_CTX)
                    if 0 <= j < CONV_WIDTH:
                        acc = acc + wdw_ref[j:j + 1, cols] * wb[a * SUBLANES:a * SUBLANES + CONV_ROWS]
            c_ref[pl.ds(r0, CONV_ROWS), cols] = acc
        return carry
    lax.fori_loop(0, tt // CONV_ROWS, chunk, 0)


def _conv_prompt_kernel(x_ref, nm_ref, w1_ref, b1_ref, wdw_ref, bdw_ref, lng_ref, lnb_ref, w2_ref, b2_ref,
                        h_ref, st_ref, uext_ref, c_ref, *, tt):
    t = pl.program_id(1)

    @pl.when(t == 0)
    def _():
        uext_ref[0:CONV_PAD, :] = jnp.zeros((CONV_PAD, D_MODEL), F32)

    @pl.when(t > 0)
    def _():
        uext_ref[0:CONV_PAD, :] = uext_ref[tt:tt + CONV_PAD, :]

    x = x_ref[0]
    z = _dot(_rms(x, nm_ref[...]), w1_ref[...]) + b1_ref[...]
    u = z[:, :D_MODEL] * _sigmoid(z[:, D_MODEL:])
    uext_ref[CONV_PAD:CONV_PAD + tt, :] = u
    _conv_taps(uext_ref, wdw_ref, bdw_ref, c_ref, tt)
    c = _silu(_layer_norm(c_ref[...], lng_ref[...], lnb_ref[...]))
    h_ref[0] = x + _dot(c, w2_ref[...]) + b2_ref[...]
    st_ref[0] = uext_ref[CONV_PAD + tt - CONV_CTX:CONV_PAD + tt, :]


def _conv_prompt(x, nm, w1, b1, wdw, bdw, lng, lnb, w2, b2):
    n, t, d = x.shape
    tt = TOK_TILE
    weights = (nm, w1, b1, wdw, bdw, lng, lnb, w2, b2)
    return pl.pallas_call(
        functools.partial(_conv_prompt_kernel, tt=tt),
        grid=(n, t // tt),
        in_specs=[pl.BlockSpec((1, tt, d), lambda b, i: (b, i, 0))] + [_const(w.shape) for w in weights],
        out_specs=[pl.BlockSpec((1, tt, d), lambda b, i: (b, i, 0)),
                   pl.BlockSpec((1, CONV_CTX, d), lambda b, i: (b, 0, 0))],
        out_shape=[jax.ShapeDtypeStruct((n, t, d), F32), jax.ShapeDtypeStruct((n, CONV_CTX, d), F32)],
        scratch_shapes=[pltpu.VMEM((CONV_PAD + tt, d), F32), pltpu.VMEM((tt, d), F32)],
        compiler_params=_params("arbitrary", "arbitrary"),
        name="conv_prompt",
    )(x, *weights)


def _conv_sample_kernel(x_ref, past_ref, nm_ref, w1_ref, b1_ref, wdw_ref, bdw_ref, lng_ref, lnb_ref,
                        w2_ref, b2_ref, h_ref, st_ref, c_ref, *, nb):
    x = x_ref[...]
    z = _dot(_rms(x, nm_ref[...]), w1_ref[...]) + b1_ref[...]
    u = z[:, :D_MODEL] * _sigmoid(z[:, D_MODEL:])
    w_past = wdw_ref[0:CONV_CTX, :]
    for i in range(nb):
        past = past_ref[i]
        c_ref[i:i + 1, :] = jnp.sum(past * w_past, axis=0, keepdims=True)
        st_ref[i, 0:CONV_CTX - 1, :] = past[1:CONV_CTX, :]
        st_ref[i, CONV_CTX - 1:CONV_CTX, :] = u[i:i + 1, :]
    c = c_ref[...] + wdw_ref[CONV_CTX:CONV_WIDTH, :] * u + bdw_ref[...]
    c = _silu(_layer_norm(c, lng_ref[...], lnb_ref[...]))
    h_ref[...] = x + _dot(c, w2_ref[...]) + b2_ref[...]


def _conv_sample(x, past, nm, w1, b1, wdw, bdw, lng, lnb, w2, b2):
    n, d = x.shape
    nb = 16
    weights = (nm, w1, b1, wdw, bdw, lng, lnb, w2, b2)
    return pl.pallas_call(
        functools.partial(_conv_sample_kernel, nb=nb),
        grid=(n // nb,),
        in_specs=[pl.BlockSpec((nb, d), lambda i: (i, 0)),
                  pl.BlockSpec((nb, CONV_CTX, d), lambda i: (i, 0, 0))] + [_const(w.shape) for w in weights],
        out_specs=[pl.BlockSpec((nb, d), lambda i: (i, 0)),
                   pl.BlockSpec((nb, CONV_CTX, d), lambda i: (i, 0, 0))],
        out_shape=[jax.ShapeDtypeStruct((n, d), F32), jax.ShapeDtypeStruct((n, CONV_CTX, d), F32)],
        scratch_shapes=[pltpu.VMEM((nb, d), F32)],
        compiler_params=_params("arbitrary"),
        name="conv_sample",
    )(x, past, *weights)


def _ffn_kernel(h_ref, g_ref, wg_ref, wu_ref, wd_ref, o_ref):
    h = h_ref[...]
    hn = _rms(h, g_ref[...]).astype(BF16)
    a = jnp.dot(hn, wg_ref[...], preferred_element_type=F32)
    b = jnp.dot(hn, wu_ref[...], preferred_element_type=F32)
    o_ref[...] = h + _dot(_silu(a) * b, wd_ref[...])


def _ffn(h, g, wg, wu, wd):
    m, d = h.shape
    tm = min(TOK_TILE // 2, m)
    weights = (g, wg, wu, wd)
    return pl.pallas_call(
        _ffn_kernel,
        grid=(m // tm,),
        in_specs=[pl.BlockSpec((tm, d), lambda i: (i, 0))] + [_const(w.shape) for w in weights],
        out_specs=pl.BlockSpec((tm, d), lambda i: (i, 0)),
        out_shape=jax.ShapeDtypeStruct((m, d), F32),
        compiler_params=_params("arbitrary"),
        name="dense_ffn",
    )(h, *weights)


def _ple(h, p, g, w_gate, w_proj):
    return h + _sigmoid(_dot(_rms(h, g), w_gate)) * _dot(p, w_proj)


def _ple_kvq_kernel(h_ref, p_ref, gple_ref, wpg_ref, wpp_ref, gkv_ref, wk_ref, wv_ref, kn_ref,
                    gmix_ref, wq_ref, qn_ref, h3_ref, k_ref, v_ref, q_ref):
    h3 = _ple(h_ref[...], p_ref[...], gple_ref[...], wpg_ref[...], wpp_ref[...])
    h3_ref[...] = h3
    s = _rms(h3, gkv_ref[...]).astype(BF16)
    k_ref[...] = _head_rms(jnp.dot(s, wk_ref[...], preferred_element_type=F32), kn_ref[...])
    v_ref[...] = jnp.dot(s, wv_ref[...], preferred_element_type=F32)
    q = _dot(_rms(h3, gmix_ref[...]), wq_ref[...])
    q_ref[...] = _head_rms(q, qn_ref[...]).astype(BF16)


def _ple_kvq(h, p, gple, wpg, wpp, gkv, wk, wv, kn, gmix, wq, qn):
    m, d = h.shape
    tm = min(TOK_TILE, m)
    weights = (gple, wpg, wpp, gkv, wk, wv, kn, gmix, wq, qn)
    row = lambda w: pl.BlockSpec((tm, w), lambda i: (i, 0))
    return pl.pallas_call(
        _ple_kvq_kernel,
        grid=(m // tm,),
        in_specs=[row(d), row(PLE_DIM)] + [_const(w.shape) for w in weights],
        out_specs=[row(d), row(KV_DIM), row(KV_DIM), row(d)],
        out_shape=[jax.ShapeDtypeStruct((m, d), F32), jax.ShapeDtypeStruct((m, KV_DIM), F32),
                   jax.ShapeDtypeStruct((m, KV_DIM), F32), jax.ShapeDtypeStruct((m, d), BF16)],
        compiler_params=_params("arbitrary"),
        name="ple_kv_q",
    )(h, p, *weights)


def _softmax_sink(s, sink):
    m = jnp.maximum(jnp.max(s, axis=-1, keepdims=True), sink)
    e = jnp.exp(s - m)
    den = jnp.sum(e, axis=-1, keepdims=True) + jnp.exp(sink - m)
    return e / den


def _attn_prompt_kernel(q_ref, kp_ref, kc_ref, vp_ref, vc_ref, sink_ref, o_ref):
    n = pl.program_id(1)
    q = q_ref[0]
    k = jnp.concatenate([kp_ref[0], kc_ref[0]], axis=0).astype(BF16)
    v = jnp.concatenate([vp_ref[0], vc_ref[0]], axis=0).astype(BF16)
    qi = lax.broadcasted_iota(jnp.int32, (WINDOW, 2 * WINDOW), 0)
    kj = lax.broadcasted_iota(jnp.int32, (WINDOW, 2 * WINDOW), 1)
    dist = qi + WINDOW - kj
    first_key = jnp.where(n > 0, 0, WINDOW)
    mask = (dist >= 0) & (dist <= WINDOW) & (kj >= first_key)
    outs = []
    for h in range(N_HEADS):
        kh = h % N_KV_HEADS
        qh = q[:, h * HEAD_DIM:(h + 1) * HEAD_DIM]
        kk = k[:, kh * HEAD_DIM:(kh + 1) * HEAD_DIM]
        vv = v[:, kh * HEAD_DIM:(kh + 1) * HEAD_DIM]
        s = lax.dot_general(qh, kk, (((1,), (1,)), ((), ())), preferred_element_type=F32) * (HEAD_DIM ** -0.5)
        s = jnp.where(mask, s, -jnp.inf)
        p = _softmax_sink(s, sink_ref[h])
        outs.append(jnp.dot(p.astype(BF16), vv, preferred_element_type=F32))
    o_ref[0] = jnp.concatenate(outs, axis=1).astype(BF16)


def _attn_prompt(q, k, v, sinks):
    n, t, d = q.shape
    nb = t // WINDOW
    cur = lambda w: pl.BlockSpec((1, WINDOW, w), lambda b, i: (b, i, 0))
    prev = lambda w: pl.BlockSpec((1, WINDOW, w), lambda b, i: (b, jnp.maximum(i - 1, 0), 0))
    return pl.pallas_call(
        _attn_prompt_kernel,
        grid=(n, nb),
        in_specs=[cur(d), prev(KV_DIM), cur(KV_DIM), prev(KV_DIM), cur(KV_DIM),
                  pl.BlockSpec(memory_space=pltpu.SMEM)],
        out_specs=cur(d),
        out_shape=jax.ShapeDtypeStruct((n, t, d), BF16),
        compiler_params=_params("arbitrary", "arbitrary"),
        name="attn_prompt",
    )(q, k, k, v, v, sinks)


def _attn_sample_kernel(q_ref, kc_ref, vc_ref, kn_ref, vn_ref, sink_ref, o_ref, ks_ref, vs_ref, *, nb):
    scale = HEAD_DIM ** -0.5
    lane_head = lax.broadcasted_iota(jnp.int32, (N_KV_HEADS, KV_DIM), 1) // HEAD_DIM
    own = lane_head == lax.broadcasted_iota(jnp.int32, (N_KV_HEADS, KV_DIM), 0)
    sink = sink_ref[...]
    rows = []
    for i in range(nb):
        kc = kc_ref[i]
        vc = vc_ref[i]
        kn = kn_ref[i:i + 1, :]
        vn = vn_ref[i:i + 1, :]
        ks_ref[i, 0:WINDOW - 1, :] = kc[1:WINDOW, :]
        ks_ref[i, WINDOW - 1:WINDOW, :] = kn
        vs_ref[i, 0:WINDOW - 1, :] = vc[1:WINDOW, :]
        vs_ref[i, WINDOW - 1:WINDOW, :] = vn
        qrow = q_ref[i:i + 1, :].astype(F32)
        qx = jnp.concatenate(
            [jnp.where(own, jnp.broadcast_to(qrow[:, g * KV_DIM:(g + 1) * KV_DIM], (N_KV_HEADS, KV_DIM)), 0.0)
             for g in range(GROUP)], axis=0)
        s_c = lax.dot_general(qx.astype(BF16), kc.astype(BF16), (((1,), (1,)), ((), ())),
                              preferred_element_type=F32) * scale
        s_n = jnp.sum(qx * kn.astype(BF16).astype(F32), axis=-1, keepdims=True) * scale
        m = jnp.maximum(jnp.maximum(jnp.max(s_c, axis=-1, keepdims=True), s_n), sink)
        e_c = jnp.exp(s_c - m)
        e_n = jnp.exp(s_n - m)
        den = jnp.sum(e_c, axis=-1, keepdims=True) + e_n + jnp.exp(sink - m)
        ox = jnp.dot((e_c / den).astype(BF16), vc.astype(BF16), preferred_element_type=F32)
        ox = ox + (e_n / den).astype(BF16).astype(F32) * vn.astype(BF16).astype(F32)
        rows.append(jnp.concatenate(
            [jnp.sum(jnp.where(own, ox[g * N_KV_HEADS:(g + 1) * N_KV_HEADS], 0.0), axis=0, keepdims=True)
             for g in range(GROUP)], axis=1))
    o_ref[...] = jnp.concatenate(rows, axis=0).astype(BF16)


def _attn_sample(q, kc, vc, kn, vn, sinks_col):
    n, d = q.shape
    nb = 16
    row = lambda w: pl.BlockSpec((nb, w), lambda i: (i, 0))
    cache = pl.BlockSpec((nb, WINDOW, KV_DIM), lambda i: (i, 0, 0))
    return pl.pallas_call(
        functools.partial(_attn_sample_kernel, nb=nb),
        grid=(n // nb,),
        in_specs=[row(d), cache, cache, row(KV_DIM), row(KV_DIM), _const(sinks_col.shape)],
        out_specs=[row(d), cache, cache],
        out_shape=[jax.ShapeDtypeStruct((n, d), BF16),
                   jax.ShapeDtypeStruct((n, WINDOW, KV_DIM), F32),
                   jax.ShapeDtypeStruct((n, WINDOW, KV_DIM), F32)],
        compiler_params=_params("arbitrary"),
        name="attn_sample",
    )(q, kc, vc, kn, vn, sinks_col)


def _oproj_router_kernel(o_ref, h_ref, wo_ref, g_ref, wr_ref, br_ref,
                         h4_ref, hn_ref, route_ref, cnt_ref, carry_ref, *, tm):
    i = pl.program_id(0)

    @pl.when(i == 0)
    def _():
        carry_ref[...] = jnp.zeros((1, LANES), F32)

    h4 = h_ref[...] + jnp.dot(o_ref[...], wo_ref[...], preferred_element_type=F32)
    h4_ref[...] = h4
    hn = _rms(h4, g_ref[...])
    _rows_to_tiles(hn_ref, hn)
    hn_hi = hn.astype(BF16)
    hn_lo = (hn - hn_hi.astype(F32)).astype(BF16)
    part_hi = jnp.dot(hn_hi, wr_ref[...], preferred_element_type=F32)
    part_lo = jnp.dot(hn_lo, wr_ref[...], preferred_element_type=F32)
    logits = (part_hi[:, :LANES] + (part_hi[:, LANES:] + part_lo[:, :LANES]) + part_lo[:, LANES:]) + br_ref[...]
    lane = lax.broadcasted_iota(jnp.int32, (tm, LANES), 1).astype(F32)
    m1 = jnp.max(logits, axis=-1, keepdims=True)
    i1 = jnp.min(jnp.where(logits == m1, lane, float(LANES)), axis=-1, keepdims=True)
    rest = jnp.where(lane == i1, -jnp.inf, logits)
    m2 = jnp.max(rest, axis=-1, keepdims=True)
    i2 = jnp.min(jnp.where(rest == m2, lane, float(LANES)), axis=-1, keepdims=True)
    e2 = jnp.exp(m2 - m1)
    w1 = 1.0 / (1.0 + e2)
    w2 = e2 / (1.0 + e2)
    oh1 = lane == i1
    oh2 = lane == i2
    sel = jnp.where(oh1, 1.0, 0.0) + jnp.where(oh2, 1.0, 0.0)
    row = lax.broadcasted_iota(jnp.int32, (tm, tm), 0)
    col = lax.broadcasted_iota(jnp.int32, (tm, tm), 1)
    tri = jnp.where(row > col, 1.0, 0.0).astype(BF16)
    before = jnp.dot(tri, sel.astype(BF16), preferred_element_type=F32) + carry_ref[...]
    r1 = jnp.sum(jnp.where(oh1, before, 0.0), axis=-1, keepdims=True)
    r2 = jnp.sum(jnp.where(oh2, before, 0.0), axis=-1, keepdims=True)
    carry_ref[...] = carry_ref[...] + jnp.sum(sel, axis=0, keepdims=True)
    cnt_ref[...] = carry_ref[...]
    fields = (i1, i2, r1, r2, w1, w2)
    route = jnp.zeros((tm, LANES), F32)
    for f, val in enumerate(fields):
        route = jnp.where(lane == float(f), val, route)
    route_ref[...] = route


def _oproj_router(o, h, wo, g, wr, br):
    m, d = h.shape
    tm = min(TOK_TILE, m)
    weights = (wo, g, wr, br)
    row = lambda w: pl.BlockSpec((tm, w), lambda i: (i, 0))
    return pl.pallas_call(
        functools.partial(_oproj_router_kernel, tm=tm),
        grid=(m // tm,),
        in_specs=[row(d), row(d)] + [_const(w.shape) for w in weights],
        out_specs=[row(d), pl.BlockSpec((tm * ROW_TILES, LANES), lambda i: (i, 0)), row(LANES),
                   pl.BlockSpec((1, LANES), lambda i: (0, 0))],
        out_shape=[jax.ShapeDtypeStruct((m, d), F32), jax.ShapeDtypeStruct((m * ROW_TILES, LANES), F32),
                   jax.ShapeDtypeStruct((m, LANES), F32), jax.ShapeDtypeStruct((1, LANES), F32)],
        scratch_shapes=[pltpu.VMEM((1, LANES), F32)],
        compiler_params=_params("arbitrary"),
        name="oproj_router",
    )(o, h, *weights)


def _row(ref, r):
    return ref.at[pl.ds(pl.multiple_of(r * ROW_TILES, ROW_TILES), ROW_TILES), :]


def _dispatch_kernel(pos_ref, src_ref, dst_in_ref, dst_ref, sem, *, tb, nsteps):
    del dst_in_ref
    i = pl.program_id(0)

    def issue(t, carry):
        tok = i * tb + t
        for k in range(2):
            pltpu.make_async_copy(_row(src_ref, t), _row(dst_ref, pos_ref[2 * tok + k]), sem).start()
        return carry
    lax.fori_loop(0, tb, issue, 0)

    def drain(t, carry):
        for k in range(2):
            pltpu.make_async_copy(_row(src_ref, 0), _row(dst_ref, 0), sem).wait()
        return carry
    lax.fori_loop(0, tb, drain, 0)


def _dispatch(pos, src, dst):
    m = src.shape[0] // ROW_TILES
    tb = min(DISPATCH_TILE, m)
    nsteps = m // tb
    return pl.pallas_call(
        functools.partial(_dispatch_kernel, tb=tb, nsteps=nsteps),
        grid_spec=pltpu.PrefetchScalarGridSpec(
            num_scalar_prefetch=1,
            grid=(nsteps,),
            in_specs=[pl.BlockSpec((tb * ROW_TILES, LANES), lambda i, pos: (i, 0)),
                      pl.BlockSpec(memory_space=pl.ANY)],
            out_specs=pl.BlockSpec(memory_space=pl.ANY),
            scratch_shapes=[pltpu.SemaphoreType.DMA(())],
        ),
        out_shape=jax.ShapeDtypeStruct(dst.shape, dst.dtype),
        input_output_aliases={2: 0},
        compiler_params=_params("arbitrary"),
        name="moe_dispatch",
    )(pos, src, dst)


def _moe_kernel(te_ref, nv_ref, x_ref, wg_ref, wu_ref, wd_ref, y_ref, xb_ref, acc_ref, *, tm, nf):
    i = pl.program_id(0)
    f = pl.program_id(1)
    valid = i < nv_ref[0]

    @pl.when(jnp.logical_and(valid, f == 0))
    def _():
        xb_ref[...] = _tiles_to_rows(x_ref, tm).astype(BF16)
        acc_ref[...] = jnp.zeros((tm, D_MODEL), F32)

    @pl.when(valid)
    def _():
        x = xb_ref[...]
        a = jnp.dot(x, wg_ref[0], preferred_element_type=F32)
        b = jnp.dot(x, wu_ref[0], preferred_element_type=F32)
        acc_ref[...] += _dot(_silu(a) * b, wd_ref[0])

    @pl.when(f == nf - 1)
    def _():
        _rows_to_tiles(y_ref, jnp.where(valid, acc_ref[...], 0.0))


def _moe(tile_expert, n_valid, x_sorted, wg, wu, wd):
    rows = x_sorted.shape[0] // ROW_TILES
    tm, tf = MOE_TILE, MOE_F_TILE
    nt, nf = rows // tm, D_EXPERT // tf

    def fidx(i, f, nv):
        return jnp.where(i < nv[0], f, nf - 1)
    return pl.pallas_call(
        functools.partial(_moe_kernel, tm=tm, nf=nf),
        grid_spec=pltpu.PrefetchScalarGridSpec(
            num_scalar_prefetch=2,
            grid=(nt, nf),
            in_specs=[pl.BlockSpec((tm * ROW_TILES, LANES), lambda i, f, te, nv: (i, 0)),
                      pl.BlockSpec((1, D_MODEL, tf), lambda i, f, te, nv: (te[i], 0, fidx(i, f, nv))),
                      pl.BlockSpec((1, D_MODEL, tf), lambda i, f, te, nv: (te[i], 0, fidx(i, f, nv))),
                      pl.BlockSpec((1, tf, D_MODEL), lambda i, f, te, nv: (te[i], fidx(i, f, nv), 0))],
            out_specs=pl.BlockSpec((tm * ROW_TILES, LANES), lambda i, f, te, nv: (i, 0)),
            scratch_shapes=[pltpu.VMEM((tm, D_MODEL), BF16), pltpu.VMEM((tm, D_MODEL), F32)],
        ),
        out_shape=jax.ShapeDtypeStruct(x_sorted.shape, F32),
        compiler_params=_params("arbitrary", "arbitrary"),
        name="moe_grouped",
    )(tile_expert, n_valid, x_sorted, wg, wu, wd)


def _combine_kernel(pos_ref, y_ref, h_ref, route_ref, p_ref, g_ref, wpg_ref, wpp_ref, o_ref,
                    buf_ref, sem, *, tb, nsteps):
    i = pl.program_id(0)

    def issue(step, slot):
        def body(t, carry):
            tok = step * tb + t
            for k in range(2):
                pltpu.make_async_copy(_row(y_ref, pos_ref[2 * tok + k]), _row(buf_ref.at[slot, k], t),
                                      sem.at[slot]).start()
            return carry
        lax.fori_loop(0, tb, body, 0)

    def drain(slot):
        def body(t, carry):
            for k in range(2):
                pltpu.make_async_copy(_row(y_ref, 0), _row(buf_ref.at[slot, k], 0), sem.at[slot]).wait()
            return carry
        lax.fori_loop(0, tb, body, 0)

    slot = i % 2

    @pl.when(i == 0)
    def _():
        issue(0, 0)

    @pl.when(i + 1 < nsteps)
    def _():
        issue(i + 1, 1 - slot)

    drain(slot)
    route = route_ref[...]
    w1 = route[:, 4:5]
    w2 = route[:, 5:6]
    y1 = _tiles_to_rows(buf_ref.at[slot, 0], tb)
    y2 = _tiles_to_rows(buf_ref.at[slot, 1], tb)
    h5 = h_ref[...] + (w1 * y1 + w2 * y2)
    o_ref[...] = _ple(h5, p_ref[...], g_ref[...], wpg_ref[...], wpp_ref[...])


def _combine(pos, y_sorted, h, route, p, g, wpg, wpp):
    m, d = h.shape
    tb = min(COMBINE_TILE, m)
    nsteps = m // tb
    weights = (g, wpg, wpp)
    row = lambda w: pl.BlockSpec((tb, w), lambda i, pos: (i, 0))
    return pl.pallas_call(
        functools.partial(_combine_kernel, tb=tb, nsteps=nsteps),
        grid_spec=pltpu.PrefetchScalarGridSpec(
            num_scalar_prefetch=1,
            grid=(nsteps,),
            in_specs=[pl.BlockSpec(memory_space=pl.ANY), row(d), row(LANES), row(PLE_DIM)]
                     + [pl.BlockSpec(w.shape, lambda i, pos: (0, 0), pipeline_mode=pl.Buffered(1)) for w in weights],
            out_specs=row(d),
            scratch_shapes=[pltpu.VMEM((2, 2, tb * ROW_TILES, LANES), F32), pltpu.SemaphoreType.DMA((2,))],
        ),
        out_shape=jax.ShapeDtypeStruct((m, d), F32),
        compiler_params=_params("arbitrary"),
        name="moe_combine",
    )(pos, y_sorted, h, route, p, *weights)


def _row2(v):
    return v.reshape(1, -1).astype(F32)


def kernel(x_prompt, x_sample, state_conv, cache_k, cache_v, p_prompt, p_sample, norm_mix, norm_ffn, norm_ple,
           conv_w_pw1, conv_b_pw1, conv_w_dw, conv_b_dw, conv_ln_g, conv_ln_b, conv_w_pw2, conv_b_pw2,
           norm_kv, w_k, w_v, k_norm, w_q, q_norm, sinks, w_o, ffn_w_gate, ffn_w_up, ffn_w_down,
           moe_w_router, moe_b_router, moe_w_gate, moe_w_up, moe_w_down, ple_w_gate, ple_w_proj):
    nb, t, d = x_prompt.shape
    ns = x_sample.shape[0]
    mp = nb * t
    bf = lambda w: w.astype(BF16)

    conv_w = (_row2(norm_mix[0]), bf(conv_w_pw1[0]), _row2(conv_b_pw1[0]), conv_w_dw[0], _row2(conv_b_dw[0]),
              _row2(conv_ln_g[0]), _row2(conv_ln_b[0]), bf(conv_w_pw2[0]), _row2(conv_b_pw2[0]))
    ffn_w = (_row2(norm_ffn[0]), bf(ffn_w_gate[0]), bf(ffn_w_up[0]), bf(ffn_w_down[0]))
    wq = w_q[0].reshape(d, N_KV_HEADS, GROUP, HEAD_DIM).transpose(0, 2, 1, 3).reshape(d, d)
    wo = w_o[0].reshape(N_KV_HEADS, GROUP, HEAD_DIM, d).transpose(1, 0, 2, 3).reshape(d, d)
    sinks_gk = sinks[0].astype(F32).reshape(N_KV_HEADS, GROUP).T.reshape(N_HEADS)
    kvq_w = (_row2(norm_ple[0]), bf(ple_w_gate[0]), bf(ple_w_proj[0]), _row2(norm_kv), bf(w_k), bf(w_v),
             _row2(jnp.tile(k_norm, N_KV_HEADS)), _row2(norm_mix[1]), bf(wq), _row2(jnp.tile(q_norm[0], N_HEADS)))
    wr = jnp.zeros((d, LANES), F32).at[:, :N_EXPERTS].set(moe_w_router[0])
    wr_hi = bf(wr)
    wr_lo = bf(wr - wr_hi.astype(F32))
    br = jnp.full((1, LANES), NEG_BIG, F32).at[0, :N_EXPERTS].set(moe_b_router[0])
    router_w = (bf(wo), _row2(norm_ffn[1]), jnp.concatenate([wr_hi, wr_lo], axis=1), br)
    ple1_w = (_row2(norm_ple[1]), bf(ple_w_gate[1]), bf(ple_w_proj[1]))

    h1p, conv_p = _conv_prompt(x_prompt, *conv_w)
    h3p, kp, vp, qp = _ple_kvq(_ffn(h1p.reshape(mp, d), *ffn_w), p_prompt[0].reshape(mp, PLE_DIM), *kvq_w)
    h1s, conv_s = _conv_sample(x_sample.reshape(ns, d), state_conv[0], *conv_w)
    h3s, ks, vs, qs = _ple_kvq(_ffn(h1s, *ffn_w), p_sample[0].reshape(ns, PLE_DIM), *kvq_w)

    kp3 = kp.reshape(nb, t, KV_DIM)
    vp3 = vp.reshape(nb, t, KV_DIM)
    op = _attn_prompt(qp.reshape(nb, t, d), kp3, vp3, sinks_gk)
    os_, k_state_s, v_state_s = _attn_sample(qs, cache_k.reshape(ns, WINDOW, KV_DIM),
                                             cache_v.reshape(ns, WINDOW, KV_DIM), ks, vs,
                                             sinks_gk.reshape(N_HEADS, 1))

    h4p, hnp, route_p, cnt_p = _oproj_router(op.reshape(mp, d), h3p, *router_w)
    h4s, hns, route_s, cnt_s = _oproj_router(os_, h3s, *router_w)

    cnt_p = cnt_p[0, :N_EXPERTS].astype(jnp.int32)
    cnt_s = cnt_s[0, :N_EXPERTS].astype(jnp.int32)
    group = (cnt_p + cnt_s + MOE_TILE - 1) // MOE_TILE * MOE_TILE
    ends = jnp.cumsum(group)
    starts = ends - group
    n_tiles = (2 * (mp + ns) + N_EXPERTS * (MOE_TILE - 1)) // MOE_TILE
    n_valid = (ends[-1] // MOE_TILE).astype(jnp.int32)
    tile_start = jnp.minimum(jnp.arange(n_tiles, dtype=jnp.int32), n_valid - 1) * MOE_TILE
    tile_expert = jnp.sum(tile_start[:, None] >= ends[None, :], axis=1).astype(jnp.int32)

    def positions(route, base):
        idx = route[:, 0:2].astype(jnp.int32)
        rank = route[:, 2:4].astype(jnp.int32)
        return (starts[idx] + base[idx] + rank).reshape(-1)
    pos_p = positions(route_p, jnp.zeros_like(cnt_p))
    pos_s = positions(route_s, cnt_p)

    x_sorted = jnp.zeros((n_tiles * MOE_TILE * ROW_TILES, LANES), F32)
    x_sorted = _dispatch(pos_p, hnp, x_sorted)
    x_sorted = _dispatch(pos_s, hns, x_sorted)
    y_sorted = _moe(tile_expert, n_valid.reshape(1), x_sorted, bf(moe_w_gate[0]), bf(moe_w_up[0]), bf(moe_w_down[0]))

    y_p = _combine(pos_p, y_sorted, h4p, route_p, p_prompt[1].reshape(mp, PLE_DIM), *ple1_w)
    y_s = _combine(pos_s, y_sorted, h4s, route_s, p_sample[1].reshape(ns, PLE_DIM), *ple1_w)

    kv4 = lambda a, n: a.reshape(n, WINDOW, N_KV_HEADS, HEAD_DIM)
    return (y_p.reshape(nb, t, d), y_s.reshape(ns, 1, d), conv_p[None],
            kv4(kp3[:, t - WINDOW:], nb), kv4(vp3[:, t - WINDOW:], nb),
            conv_s[None], kv4(k_state_s, ns), kv4(v_state_s, ns))
```

```python
import functools

import jax
import jax.numpy as jnp
from jax import lax
from jax.experimental import pallas as pl
from jax.experimental.pallas import tpu as pltpu

D_MODEL = 1024
CONV_WIDTH = 31
CONV_CTX = CONV_WIDTH - 1
HEAD_DIM = 64
N_HEADS = 16
N_KV_HEADS = 4
GROUP = N_HEADS // N_KV_HEADS
KV_DIM = N_KV_HEADS * HEAD_DIM
WINDOW = 128
D_FF = 2816
N_EXPERTS = 8
D_EXPERT = 3584
PLE_DIM = 256
EPS = 1e-6

BF16 = jnp.bfloat16
F32 = jnp.float32

LANES = 128
SUBLANES = 8
ROW_TILES = D_MODEL // LANES
VMEM_LIMIT = 56 * 1024 * 1024

CONV_PAD = 32
CONV_ROWS = 64
TOK_TILE = 512
MOE_TILE = 512
MOE_F_TILE = 1792
COMBINE_TILE = 512
COPY_UNROLL = 8
NEG_BIG = -1e30


def _params(*sem):
    return pltpu.CompilerParams(dimension_semantics=sem, vmem_limit_bytes=VMEM_LIMIT)


def _const(shape):
    nd = len(shape)
    return pl.BlockSpec(shape, lambda *_: (0,) * nd, pipeline_mode=pl.Buffered(1))


def _rms(x, g):
    return x * lax.rsqrt(jnp.mean(x * x, axis=-1, keepdims=True) + EPS) * g


def _dot(a, w):
    return jnp.dot(a.astype(BF16), w, preferred_element_type=F32)


def _sigmoid(x):
    return 1.0 / (1.0 + jnp.exp(-x))


def _silu(x):
    return x * _sigmoid(x)


def _layer_norm(x, g, b):
    mu = jnp.mean(x, axis=-1, keepdims=True)
    xc = x - mu
    var = jnp.mean(xc * xc, axis=-1, keepdims=True)
    return xc * lax.rsqrt(var + EPS) * g + b


def _head_rms(x, g):
    t, w = x.shape
    lane = lax.broadcasted_iota(jnp.int32, (t, LANES), 1)
    lo = lane < HEAD_DIM
    outs = []
    for c in range(w // LANES):
        blk = x[:, c * LANES:(c + 1) * LANES]
        sq = blk * blk
        s_lo = jnp.sum(jnp.where(lo, sq, 0.0), axis=-1, keepdims=True)
        s_hi = jnp.sum(jnp.where(lo, 0.0, sq), axis=-1, keepdims=True)
        inv = jnp.where(lo, lax.rsqrt(s_lo / HEAD_DIM + EPS), lax.rsqrt(s_hi / HEAD_DIM + EPS))
        outs.append(blk * inv)
    return jnp.concatenate(outs, axis=1) * g


def _rows_to_tiles(ref, x):
    t = x.shape[0]
    for s in range(ROW_TILES):
        ref[pl.ds(s, t, stride=ROW_TILES), :] = x[:, s * LANES:(s + 1) * LANES]


def _tiles_to_rows(ref, t):
    return jnp.concatenate([ref[pl.ds(s, t, stride=ROW_TILES), :] for s in range(ROW_TILES)], axis=1)


def _conv_taps(uext_ref, wdw_ref, bdw_ref, c_ref, tt):
    win_rows = CONV_ROWS + CONV_PAD

    def chunk(ci, carry):
        r0 = pl.multiple_of(ci * CONV_ROWS, CONV_ROWS)
        for lt in range(ROW_TILES):
            cols = slice(lt * LANES, (lt + 1) * LANES)
            win = uext_ref[pl.ds(r0, win_rows), cols]
            acc = jnp.broadcast_to(bdw_ref[:, cols], (CONV_ROWS, LANES))
            for b in range(SUBLANES):
                wb = win if b == 0 else pltpu.roll(win, win_rows - b, axis=0)
                for a in range(CONV_PAD // SUBLANES + 1):
                    j = a * SUBLANES + b - (CONV_PAD - CONV_CTX)
                    if 0 <= j < CONV_WIDTH:
                        acc = acc + wdw_ref[j:j + 1, cols] * wb[a * SUBLANES:a * SUBLANES + CONV_ROWS]
            c_ref[pl.ds(r0, CONV_ROWS), cols] = acc
        return carry
    lax.fori_loop(0, tt // CONV_ROWS, chunk, 0)


def _conv_prompt_kernel(x_ref, nm_ref, w1_ref, b1_ref, wdw_ref, bdw_ref, lng_ref, lnb_ref, w2_ref, b2_ref,
                        h_ref, st_ref, uext_ref, c_ref, *, tt):
    t = pl.program_id(1)

    @pl.when(t == 0)
    def _():
        uext_ref[0:CONV_PAD, :] = jnp.zeros((CONV_PAD, D_MODEL), F32)

    @pl.when(t > 0)
    def _():
        uext_ref[0:CONV_PAD, :] = uext_ref[tt:tt + CONV_PAD, :]

    x = x_ref[0]
    z = _dot(_rms(x, nm_ref[...]), w1_ref[...]) + b1_ref[...]
    u = z[:, :D_MODEL] * _sigmoid(z[:, D_MODEL:])
    uext_ref[CONV_PAD:CONV_PAD + tt, :] = u
    _conv_taps(uext_ref, wdw_ref, bdw_ref, c_ref, tt)
    c = _silu(_layer_norm(c_ref[...], lng_ref[...], lnb_ref[...]))
    h_ref[0] = x + _dot(c, w2_ref[...]) + b2_ref[...]
    st_ref[0] = uext_ref[CONV_PAD + tt - CONV_CTX:CONV_PAD + tt, :]


def _conv_prompt(x, nm, w1, b1, wdw, bdw, lng, lnb, w2, b2):
    n, t, d = x.shape
    tt = TOK_TILE
    weights = (nm, w1, b1, wdw, bdw, lng, lnb, w2, b2)
    return pl.pallas_call(
        functools.partial(_conv_prompt_kernel, tt=tt),
        grid=(n, t // tt),
        in_specs=[pl.BlockSpec((1, tt, d), lambda b, i: (b, i, 0))] + [_const(w.shape) for w in weights],
        out_specs=[pl.BlockSpec((1, tt, d), lambda b, i: (b, i, 0)),
                   pl.BlockSpec((1, CONV_CTX, d), lambda b, i: (b, 0, 0))],
        out_shape=[jax.ShapeDtypeStruct((n, t, d), F32), jax.ShapeDtypeStruct((n, CONV_CTX, d), F32)],
        scratch_shapes=[pltpu.VMEM((CONV_PAD + tt, d), F32), pltpu.VMEM((tt, d), F32)],
        compiler_params=_params("arbitrary", "arbitrary"),
        name="conv_prompt",
    )(x, *weights)


def _conv_sample_kernel(x_ref, past_ref, nm_ref, w1_ref, b1_ref, wdw_ref, bdw_ref, lng_ref, lnb_ref,
                        w2_ref, b2_ref, h_ref, st_ref, c_ref, *, nb):
    x = x_ref[...]
    z = _dot(_rms(x, nm_ref[...]), w1_ref[...]) + b1_ref[...]
    u = z[:, :D_MODEL] * _sigmoid(z[:, D_MODEL:])
    w_past = wdw_ref[0:CONV_CTX, :]
    for i in range(nb):
        past = past_ref[i]
        c_ref[i:i + 1, :] = jnp.sum(past * w_past, axis=0, keepdims=True)
        st_ref[i, 0:CONV_CTX - 1, :] = past[1:CONV_CTX, :]
        st_ref[i, CONV_CTX - 1:CONV_CTX, :] = u[i:i + 1, :]
    c = c_ref[...] + wdw_ref[CONV_CTX:CONV_WIDTH, :] * u + bdw_ref[...]
    c = _silu(_layer_norm(c, lng_ref[...], lnb_ref[...]))
    h_ref[...] = x + _dot(c, w2_ref[...]) + b2_ref[...]


def _conv_sample(x, past, nm, w1, b1, wdw, bdw, lng, lnb, w2, b2):
    n, d = x.shape
    nb = 16
    weights = (nm, w1, b1, wdw, bdw, lng, lnb, w2, b2)
    return pl.pallas_call(
        functools.partial(_conv_sample_kernel, nb=nb),
        grid=(n // nb,),
        in_specs=[pl.BlockSpec((nb, d), lambda i: (i, 0)),
                  pl.BlockSpec((nb, CONV_CTX, d), lambda i: (i, 0, 0))] + [_const(w.shape) for w in weights],
        out_specs=[pl.BlockSpec((nb, d), lambda i: (i, 0)),
                   pl.BlockSpec((nb, CONV_CTX, d), lambda i: (i, 0, 0))],
        out_shape=[jax.ShapeDtypeStruct((n, d), F32), jax.ShapeDtypeStruct((n, CONV_CTX, d), F32)],
        scratch_shapes=[pltpu.VMEM((nb, d), F32)],
        compiler_params=_params("arbitrary"),
        name="conv_sample",
    )(x, past, *weights)


def _ffn_kernel(h_ref, g_ref, wg_ref, wu_ref, wd_ref, o_ref):
    h = h_ref[...]
    hn = _rms(h, g_ref[...]).astype(BF16)
    a = jnp.dot(hn, wg_ref[...], preferred_element_type=F32)
    b = jnp.dot(hn, wu_ref[...], preferred_element_type=F32)
    o_ref[...] = h + _dot(_silu(a) * b, wd_ref[...])


def _ffn(h, g, wg, wu, wd):
    m, d = h.shape
    tm = min(TOK_TILE // 2, m)
    weights = (g, wg, wu, wd)
    return pl.pallas_call(
        _ffn_kernel,
        grid=(m // tm,),
        in_specs=[pl.BlockSpec((tm, d), lambda i: (i, 0))] + [_const(w.shape) for w in weights],
        out_specs=pl.BlockSpec((tm, d), lambda i: (i, 0)),
        out_shape=jax.ShapeDtypeStruct((m, d), F32),
        compiler_params=_params("arbitrary"),
        name="dense_ffn",
    )(h, *weights)


def _ple(h, p, g, w_gate, w_proj):
    return h + _sigmoid(_dot(_rms(h, g), w_gate)) * _dot(p, w_proj)


def _ple_kvq_kernel(h_ref, p_ref, gple_ref, wpg_ref, wpp_ref, gkv_ref, wk_ref, wv_ref, kn_ref,
                    gmix_ref, wq_ref, qn_ref, h3_ref, k_ref, v_ref, q_ref):
    h3 = _ple(h_ref[...], p_ref[...], gple_ref[...], wpg_ref[...], wpp_ref[...])
    h3_ref[...] = h3
    s = _rms(h3, gkv_ref[...]).astype(BF16)
    k_ref[...] = _head_rms(jnp.dot(s, wk_ref[...], preferred_element_type=F32), kn_ref[...])
    v_ref[...] = jnp.dot(s, wv_ref[...], preferred_element_type=F32)
    q = _dot(_rms(h3, gmix_ref[...]), wq_ref[...])
    q_ref[...] = _head_rms(q, qn_ref[...]).astype(BF16)


def _ple_kvq(h, p, gple, wpg, wpp, gkv, wk, wv, kn, gmix, wq, qn):
    m, d = h.shape
    tm = min(TOK_TILE, m)
    weights = (gple, wpg, wpp, gkv, wk, wv, kn, gmix, wq, qn)
    row = lambda w: pl.BlockSpec((tm, w), lambda i: (i, 0))
    return pl.pallas_call(
        _ple_kvq_kernel,
        grid=(m // tm,),
        in_specs=[row(d), row(PLE_DIM)] + [_const(w.shape) for w in weights],
        out_specs=[row(d), row(KV_DIM), row(KV_DIM), row(d)],
        out_shape=[jax.ShapeDtypeStruct((m, d), F32), jax.ShapeDtypeStruct((m, KV_DIM), F32),
                   jax.ShapeDtypeStruct((m, KV_DIM), F32), jax.ShapeDtypeStruct((m, d), BF16)],
        compiler_params=_params("arbitrary"),
        name="ple_kv_q",
    )(h, p, *weights)


def _softmax_sink(s, sink):
    m = jnp.maximum(jnp.max(s, axis=-1, keepdims=True), sink)
    e = jnp.exp(s - m)
    den = jnp.sum(e, axis=-1, keepdims=True) + jnp.exp(sink - m)
    return e / den


def _attn_prompt_kernel(q_ref, kp_ref, kc_ref, vp_ref, vc_ref, sink_ref, o_ref):
    n = pl.program_id(1)
    q = q_ref[0]
    k = jnp.concatenate([kp_ref[0], kc_ref[0]], axis=0).astype(BF16)
    v = jnp.concatenate([vp_ref[0], vc_ref[0]], axis=0).astype(BF16)
    qi = lax.broadcasted_iota(jnp.int32, (WINDOW, 2 * WINDOW), 0)
    kj = lax.broadcasted_iota(jnp.int32, (WINDOW, 2 * WINDOW), 1)
    dist = qi + WINDOW - kj
    first_key = jnp.where(n > 0, 0, WINDOW)
    mask = (dist >= 0) & (dist <= WINDOW) & (kj >= first_key)
    outs = []
    for h in range(N_HEADS):
        kh = h % N_KV_HEADS
        qh = q[:, h * HEAD_DIM:(h + 1) * HEAD_DIM]
        kk = k[:, kh * HEAD_DIM:(kh + 1) * HEAD_DIM]
        vv = v[:, kh * HEAD_DIM:(kh + 1) * HEAD_DIM]
        s = lax.dot_general(qh, kk, (((1,), (1,)), ((), ())), preferred_element_type=F32) * (HEAD_DIM ** -0.5)
        s = jnp.where(mask, s, -jnp.inf)
        p = _softmax_sink(s, sink_ref[h])
        outs.append(jnp.dot(p.astype(BF16), vv, preferred_element_type=F32))
    o_ref[0] = jnp.concatenate(outs, axis=1).astype(BF16)


def _attn_prompt(q, k, v, sinks):
    n, t, d = q.shape
    nb = t // WINDOW
    cur = lambda w: pl.BlockSpec((1, WINDOW, w), lambda b, i: (b, i, 0))
    prev = lambda w: pl.BlockSpec((1, WINDOW, w), lambda b, i: (b, jnp.maximum(i - 1, 0), 0))
    return pl.pallas_call(
        _attn_prompt_kernel,
        grid=(n, nb),
        in_specs=[cur(d), prev(KV_DIM), cur(KV_DIM), prev(KV_DIM), cur(KV_DIM),
                  pl.BlockSpec(memory_space=pltpu.SMEM)],
        out_specs=cur(d),
        out_shape=jax.ShapeDtypeStruct((n, t, d), BF16),
        compiler_params=_params("arbitrary", "arbitrary"),
        name="attn_prompt",
    )(q, k, k, v, v, sinks)


def _attn_sample_kernel(q_ref, kc_ref, vc_ref, kn_ref, vn_ref, sink_ref, o_ref, ks_ref, vs_ref, *, nb):
    scale = HEAD_DIM ** -0.5
    lane_head = lax.broadcasted_iota(jnp.int32, (N_KV_HEADS, KV_DIM), 1) // HEAD_DIM
    own = lane_head == lax.broadcasted_iota(jnp.int32, (N_KV_HEADS, KV_DIM), 0)
    sink = sink_ref[...]
    rows = []
    for i in range(nb):
        kc = kc_ref[i]
        vc = vc_ref[i]
        kn = kn_ref[i:i + 1, :]
        vn = vn_ref[i:i + 1, :]
        ks_ref[i, 0:WINDOW - 1, :] = kc[1:WINDOW, :]
        ks_ref[i, WINDOW - 1:WINDOW, :] = kn
        vs_ref[i, 0:WINDOW - 1, :] = vc[1:WINDOW, :]
        vs_ref[i, WINDOW - 1:WINDOW, :] = vn
        qrow = q_ref[i:i + 1, :].astype(F32)
        qx = jnp.concatenate(
            [jnp.where(own, jnp.broadcast_to(qrow[:, g * KV_DIM:(g + 1) * KV_DIM], (N_KV_HEADS, KV_DIM)), 0.0)
             for g in range(GROUP)], axis=0)
        s_c = lax.dot_general(qx.astype(BF16), kc.astype(BF16), (((1,), (1,)), ((), ())),
                              preferred_element_type=F32) * scale
        s_n = jnp.sum(qx * kn.astype(BF16).astype(F32), axis=-1, keepdims=True) * scale
        m = jnp.maximum(jnp.maximum(jnp.max(s_c, axis=-1, keepdims=True), s_n), sink)
        e_c = jnp.exp(s_c - m)
        e_n = jnp.exp(s_n - m)
        den = jnp.sum(e_c, axis=-1, keepdims=True) + e_n + jnp.exp(sink - m)
        ox = jnp.dot((e_c / den).astype(BF16), vc.astype(BF16), preferred_element_type=F32)
        ox = ox + (e_n / den).astype(BF16).astype(F32) * vn.astype(BF16).astype(F32)
        rows.append(jnp.concatenate(
            [jnp.sum(jnp.where(own, ox[g * N_KV_HEADS:(g + 1) * N_KV_HEADS], 0.0), axis=0, keepdims=True)
             for g in range(GROUP)], axis=1))
    o_ref[...] = jnp.concatenate(rows, axis=0).astype(BF16)


def _attn_sample(q, kc, vc, kn, vn, sinks_col):
    n, d = q.shape
    nb = 16
    row = lambda w: pl.BlockSpec((nb, w), lambda i: (i, 0))
    cache = pl.BlockSpec((nb, WINDOW, KV_DIM), lambda i: (i, 0, 0))
    return pl.pallas_call(
        functools.partial(_attn_sample_kernel, nb=nb),
        grid=(n // nb,),
        in_specs=[row(d), cache, cache, row(KV_DIM), row(KV_DIM), _const(sinks_col.shape)],
        out_specs=[row(d), cache, cache],
        out_shape=[jax.ShapeDtypeStruct((n, d), BF16),
                   jax.ShapeDtypeStruct((n, WINDOW, KV_DIM), F32),
                   jax.ShapeDtypeStruct((n, WINDOW, KV_DIM), F32)],
        compiler_params=_params("arbitrary"),
        name="attn_sample",
    )(q, kc, vc, kn, vn, sinks_col)


def _oproj_router_kernel(rows_in_ref, o_ref, h_ref, wo_ref, g_ref, wr_ref, br_ref,
                         h4_ref, hn_ref, route_ref, cnt_ref, carry_ref, *, tm):
    del rows_in_ref
    i = pl.program_id(0)

    @pl.when(i == 0)
    def _():
        carry_ref[...] = jnp.zeros((1, LANES), F32)

    h4 = h_ref[...] + jnp.dot(o_ref[...], wo_ref[...], preferred_element_type=F32)
    h4_ref[...] = h4
    hn = _rms(h4, g_ref[...])
    _rows_to_tiles(hn_ref, hn)
    hn_hi = hn.astype(BF16)
    hn_lo = (hn - hn_hi.astype(F32)).astype(BF16)
    part_hi = jnp.dot(hn_hi, wr_ref[...], preferred_element_type=F32)
    part_lo = jnp.dot(hn_lo, wr_ref[...], preferred_element_type=F32)
    logits = (part_hi[:, :LANES] + (part_hi[:, LANES:] + part_lo[:, :LANES]) + part_lo[:, LANES:]) + br_ref[...]
    lane = lax.broadcasted_iota(jnp.int32, (tm, LANES), 1).astype(F32)
    m1 = jnp.max(logits, axis=-1, keepdims=True)
    i1 = jnp.min(jnp.where(logits == m1, lane, float(LANES)), axis=-1, keepdims=True)
    rest = jnp.where(lane == i1, -jnp.inf, logits)
    m2 = jnp.max(rest, axis=-1, keepdims=True)
    i2 = jnp.min(jnp.where(rest == m2, lane, float(LANES)), axis=-1, keepdims=True)
    e2 = jnp.exp(m2 - m1)
    w1 = 1.0 / (1.0 + e2)
    w2 = e2 / (1.0 + e2)
    oh1 = lane == i1
    oh2 = lane == i2
    sel = jnp.where(oh1, 1.0, 0.0) + jnp.where(oh2, 1.0, 0.0)
    row = lax.broadcasted_iota(jnp.int32, (tm, tm), 0)
    col = lax.broadcasted_iota(jnp.int32, (tm, tm), 1)
    tri = jnp.where(row > col, 1.0, 0.0).astype(BF16)
    before = jnp.dot(tri, sel.astype(BF16), preferred_element_type=F32) + carry_ref[...]
    r1 = jnp.sum(jnp.where(oh1, before, 0.0), axis=-1, keepdims=True)
    r2 = jnp.sum(jnp.where(oh2, before, 0.0), axis=-1, keepdims=True)
    carry_ref[...] = carry_ref[...] + jnp.sum(sel, axis=0, keepdims=True)
    cnt_ref[...] = carry_ref[...]
    fields = (i1, i2, r1, r2, w1, w2)
    route = jnp.zeros((tm, LANES), F32)
    for f, val in enumerate(fields):
        route = jnp.where(lane == float(f), val, route)
    route_ref[...] = route


def _oproj_router(rows_buf, o, h, wo, g, wr, br, *, first_token):
    m, d = h.shape
    tm = min(TOK_TILE, m)
    weights = (wo, g, wr, br)
    row = lambda w: pl.BlockSpec((tm, w), lambda i: (i, 0))
    first_block = first_token // tm
    return pl.pallas_call(
        functools.partial(_oproj_router_kernel, tm=tm),
        grid=(m // tm,),
        in_specs=[pl.BlockSpec(memory_space=pl.ANY), row(d), row(d)] + [_const(w.shape) for w in weights],
        out_specs=[row(d), pl.BlockSpec((tm * ROW_TILES, LANES), lambda i: (i + first_block, 0)), row(LANES),
                   pl.BlockSpec((1, LANES), lambda i: (0, 0))],
        out_shape=[jax.ShapeDtypeStruct((m, d), F32), jax.ShapeDtypeStruct(rows_buf.shape, F32),
                   jax.ShapeDtypeStruct((m, LANES), F32), jax.ShapeDtypeStruct((1, LANES), F32)],
        scratch_shapes=[pltpu.VMEM((1, LANES), F32)],
        input_output_aliases={0: 1},
        compiler_params=_params("arbitrary"),
        name="oproj_router",
    )(rows_buf, o, h, *weights)


def _row(ref, r):
    return ref.at[pl.ds(pl.multiple_of(r * ROW_TILES, ROW_TILES), ROW_TILES), :]


def _for_rows(lo, hi, fn):
    groups = (hi - lo) // COPY_UNROLL

    def body(j, carry):
        for u in range(COPY_UNROLL):
            fn(lo + j * COPY_UNROLL + u)
        return carry
    lax.fori_loop(0, groups, body, 0)

    def single(r, carry):
        fn(r)
        return carry
    lax.fori_loop(lo + groups * COPY_UNROLL, hi, single, 0)


def _moe_kernel(te_ref, nv_ref, nr_ref, src_ref, dst_ref, rows_ref, wg_ref, wu_ref, wd_ref, y_ref,
                xg_ref, yb_ref, pad_ref, xb_ref, acc_ref, gsem, ssem, *, tm, nf, nt):
    i = pl.program_id(0)
    f = pl.program_id(1)
    nv = nv_ref[0]
    valid = i < nv
    slot = i % 2
    part = tm // nf

    def gather(tile, lo, n, to_slot):
        def one(r):
            pltpu.make_async_copy(_row(rows_ref, src_ref[tile * tm + r]), _row(xg_ref.at[to_slot], r),
                                  gsem.at[to_slot]).start()
        _for_rows(lo, lo + n, one)

    def wait_gather(at_slot):
        pltpu.make_async_copy(rows_ref.at[pl.ds(0, tm * ROW_TILES), :], xg_ref.at[at_slot], gsem.at[at_slot]).wait()

    def wait_scatter(at_slot):
        pltpu.make_async_copy(yb_ref.at[at_slot], y_ref.at[pl.ds(0, tm * ROW_TILES), :], ssem.at[at_slot]).wait()

    @pl.when(jnp.logical_and(i == 0, f == 0))
    def _():
        gather(0, 0, tm, 0)

    @pl.when(jnp.logical_and(valid, f == 0))
    def _():
        wait_gather(slot)
        xb_ref[...] = _tiles_to_rows(xg_ref.at[slot], tm).astype(BF16)
        acc_ref[...] = jnp.zeros((tm, D_MODEL), F32)

    @pl.when(i + 1 < nv)
    def _():
        gather(i + 1, f * part, part, 1 - slot)

    @pl.when(valid)
    def _():
        x = xb_ref[...]
        a = jnp.dot(x, wg_ref[0], preferred_element_type=F32)
        b = jnp.dot(x, wu_ref[0], preferred_element_type=F32)
        acc_ref[...] += _dot(_silu(a) * b, wd_ref[0])

    @pl.when(jnp.logical_and(valid, f == nf - 1))
    def _():
        @pl.when(i >= 2)
        def _():
            wait_scatter(slot)
        _rows_to_tiles(yb_ref.at[slot], acc_ref[...])
        real = nr_ref[i]

        def to_pair(r):
            pltpu.make_async_copy(_row(yb_ref.at[slot], r), _row(y_ref, dst_ref[i * tm + r]), ssem.at[slot]).start()

        def to_pad(r):
            pltpu.make_async_copy(_row(yb_ref.at[slot], r), _row(pad_ref.at[slot], r), ssem.at[slot]).start()
        _for_rows(0, real, to_pair)
        _for_rows(real, tm, to_pad)

    @pl.when(jnp.logical_and(i == nt - 1, f == nf - 1))
    def _():
        @pl.when(nv >= 2)
        def _():
            wait_scatter(nv % 2)
        wait_scatter((nv - 1) % 2)


def _moe(tile_expert, n_valid, tile_rows, src, dst, rows, wg, wu, wd, n_pairs):
    tm, tf = MOE_TILE, MOE_F_TILE
    nt, nf = tile_expert.shape[0], D_EXPERT // tf

    def fidx(i, f, nv):
        return jnp.where(i < nv[0], f, nf - 1)
    tile_buf = pltpu.VMEM((2, tm * ROW_TILES, LANES), F32)
    return pl.pallas_call(
        functools.partial(_moe_kernel, tm=tm, nf=nf, nt=nt),
        grid_spec=pltpu.PrefetchScalarGridSpec(
            num_scalar_prefetch=5,
            grid=(nt, nf),
            in_specs=[pl.BlockSpec(memory_space=pl.ANY),
                      pl.BlockSpec((1, D_MODEL, tf), lambda i, f, te, nv, *_: (te[i], 0, fidx(i, f, nv))),
                      pl.BlockSpec((1, D_MODEL, tf), lambda i, f, te, nv, *_: (te[i], 0, fidx(i, f, nv))),
                      pl.BlockSpec((1, tf, D_MODEL), lambda i, f, te, nv, *_: (te[i], fidx(i, f, nv), 0))],
            out_specs=pl.BlockSpec(memory_space=pl.ANY),
            scratch_shapes=[tile_buf, tile_buf, tile_buf,
                            pltpu.VMEM((tm, D_MODEL), BF16), pltpu.VMEM((tm, D_MODEL), F32),
                            pltpu.SemaphoreType.DMA((2,)), pltpu.SemaphoreType.DMA((2,))],
        ),
        out_shape=jax.ShapeDtypeStruct((n_pairs * ROW_TILES, LANES), F32),
        compiler_params=_params("arbitrary", "arbitrary"),
        name="moe_grouped",
    )(tile_expert, n_valid, tile_rows, src, dst, rows, wg, wu, wd)


def _combine_kernel(y_ref, h_ref, route_ref, p_ref, g_ref, wpg_ref, wpp_ref, o_ref, *, tb):
    route = route_ref[...]
    w1 = route[:, 4:5]
    w2 = route[:, 5:6]
    pair = 2 * ROW_TILES
    y1 = jnp.concatenate([y_ref[pl.ds(s, tb, stride=pair), :] for s in range(ROW_TILES)], axis=1)
    y2 = jnp.concatenate([y_ref[pl.ds(ROW_TILES + s, tb, stride=pair), :] for s in range(ROW_TILES)], axis=1)
    h5 = h_ref[...] + (w1 * y1 + w2 * y2)
    o_ref[...] = _ple(h5, p_ref[...], g_ref[...], wpg_ref[...], wpp_ref[...])


def _combine(y_pairs, h, route, p, g, wpg, wpp, *, first_token):
    m, d = h.shape
    tb = min(COMBINE_TILE, m)
    weights = (g, wpg, wpp)
    row = lambda w: pl.BlockSpec((tb, w), lambda i: (i, 0))
    first_block = first_token // tb
    return pl.pallas_call(
        functools.partial(_combine_kernel, tb=tb),
        grid=(m // tb,),
        in_specs=[pl.BlockSpec((tb * 2 * ROW_TILES, LANES), lambda i: (i + first_block, 0)),
                  row(d), row(LANES), row(PLE_DIM)] + [_const(w.shape) for w in weights],
        out_specs=row(d),
        out_shape=jax.ShapeDtypeStruct((m, d), F32),
        compiler_params=_params("arbitrary"),
        name="moe_combine",
    )(y_pairs, h, route, p, *weights)


def _row2(v):
    return v.reshape(1, -1).astype(F32)


def kernel(x_prompt, x_sample, state_conv, cache_k, cache_v, p_prompt, p_sample, norm_mix, norm_ffn, norm_ple,
           conv_w_pw1, conv_b_pw1, conv_w_dw, conv_b_dw, conv_ln_g, conv_ln_b, conv_w_pw2, conv_b_pw2,
           norm_kv, w_k, w_v, k_norm, w_q, q_norm, sinks, w_o, ffn_w_gate, ffn_w_up, ffn_w_down,
           moe_w_router, moe_b_router, moe_w_gate, moe_w_up, moe_w_down, ple_w_gate, ple_w_proj):
    nb, t, d = x_prompt.shape
    ns = x_sample.shape[0]
    mp = nb * t
    bf = lambda w: w.astype(BF16)

    conv_w = (_row2(norm_mix[0]), bf(conv_w_pw1[0]), _row2(conv_b_pw1[0]), conv_w_dw[0], _row2(conv_b_dw[0]),
              _row2(conv_ln_g[0]), _row2(conv_ln_b[0]), bf(conv_w_pw2[0]), _row2(conv_b_pw2[0]))
    ffn_w = (_row2(norm_ffn[0]), bf(ffn_w_gate[0]), bf(ffn_w_up[0]), bf(ffn_w_down[0]))
    wq = w_q[0].reshape(d, N_KV_HEADS, GROUP, HEAD_DIM).transpose(0, 2, 1, 3).reshape(d, d)
    wo = w_o[0].reshape(N_KV_HEADS, GROUP, HEAD_DIM, d).transpose(1, 0, 2, 3).reshape(d, d)
    sinks_gk = sinks[0].astype(F32).reshape(N_KV_HEADS, GROUP).T.reshape(N_HEADS)
    kvq_w = (_row2(norm_ple[0]), bf(ple_w_gate[0]), bf(ple_w_proj[0]), _row2(norm_kv), bf(w_k), bf(w_v),
             _row2(jnp.tile(k_norm, N_KV_HEADS)), _row2(norm_mix[1]), bf(wq), _row2(jnp.tile(q_norm[0], N_HEADS)))
    wr = jnp.zeros((d, LANES), F32).at[:, :N_EXPERTS].set(moe_w_router[0])
    wr_hi = bf(wr)
    wr_lo = bf(wr - wr_hi.astype(F32))
    br = jnp.full((1, LANES), NEG_BIG, F32).at[0, :N_EXPERTS].set(moe_b_router[0])
    router_w = (bf(wo), _row2(norm_ffn[1]), jnp.concatenate([wr_hi, wr_lo], axis=1), br)
    ple1_w = (_row2(norm_ple[1]), bf(ple_w_gate[1]), bf(ple_w_proj[1]))

    h1p, conv_p = _conv_prompt(x_prompt, *conv_w)
    h3p, kp, vp, qp = _ple_kvq(_ffn(h1p.reshape(mp, d), *ffn_w), p_prompt[0].reshape(mp, PLE_DIM), *kvq_w)
    h1s, conv_s = _conv_sample(x_sample.reshape(ns, d), state_conv[0], *conv_w)
    h3s, ks, vs, qs = _ple_kvq(_ffn(h1s, *ffn_w), p_sample[0].reshape(ns, PLE_DIM), *kvq_w)

    kp3 = kp.reshape(nb, t, KV_DIM)
    vp3 = vp.reshape(nb, t, KV_DIM)
    op = _attn_prompt(qp.reshape(nb, t, d), kp3, vp3, sinks_gk)
    os_, k_state_s, v_state_s = _attn_sample(qs, cache_k.reshape(ns, WINDOW, KV_DIM),
                                             cache_v.reshape(ns, WINDOW, KV_DIM), ks, vs,
                                             sinks_gk.reshape(N_HEADS, 1))

    mt = mp + ns
    rows = jnp.zeros((mt * ROW_TILES, LANES), F32)
    h4p, rows, route_p, cnt_p = _oproj_router(rows, op.reshape(mp, d), h3p, *router_w, first_token=0)
    h4s, rows, route_s, cnt_s = _oproj_router(rows, os_, h3s, *router_w, first_token=mp)

    cnt_p = cnt_p[0, :N_EXPERTS].astype(jnp.int32)
    cnt_s = cnt_s[0, :N_EXPERTS].astype(jnp.int32)
    group = (cnt_p + cnt_s + MOE_TILE - 1) // MOE_TILE * MOE_TILE
    ends = jnp.cumsum(group)
    starts = ends - group
    n_tiles = (2 * (mp + ns) + N_EXPERTS * (MOE_TILE - 1)) // MOE_TILE
    n_valid = (ends[-1] // MOE_TILE).astype(jnp.int32)
    tile_start = jnp.minimum(jnp.arange(n_tiles, dtype=jnp.int32), n_valid - 1) * MOE_TILE
    tile_expert = jnp.sum(tile_start[:, None] >= ends[None, :], axis=1).astype(jnp.int32)

    def positions(route, base):
        idx = route[:, 0:2].astype(jnp.int32)
        rank = route[:, 2:4].astype(jnp.int32)
        return (starts[idx] + base[idx] + rank).reshape(-1)
    pos = jnp.concatenate([positions(route_p, jnp.zeros_like(cnt_p)), positions(route_s, cnt_p)])
    pair_of_row = jnp.zeros((n_tiles * MOE_TILE,), jnp.int32).at[pos].set(
        jnp.arange(2 * mt, dtype=jnp.int32), unique_indices=True)
    tile_ids = jnp.arange(n_tiles, dtype=jnp.int32)
    rows_left = (starts + cnt_p + cnt_s)[tile_expert] - tile_ids * MOE_TILE
    tile_rows = jnp.where(tile_ids < n_valid, jnp.clip(rows_left, 0, MOE_TILE), 0).astype(jnp.int32)
    y_pairs = _moe(tile_expert, n_valid.reshape(1), tile_rows, pair_of_row // 2, pair_of_row, rows,
                   bf(moe_w_gate[0]), bf(moe_w_up[0]), bf(moe_w_down[0]), 2 * mt)

    y_p = _combine(y_pairs, h4p, route_p, p_prompt[1].reshape(mp, PLE_DIM), *ple1_w, first_token=0)
    y_s = _combine(y_pairs, h4s, route_s, p_sample[1].reshape(ns, PLE_DIM), *ple1_w, first_token=mp)

    kv4 = lambda a, n: a.reshape(n, WINDOW, N_KV_HEADS, HEAD_DIM)
    return (y_p.reshape(nb, t, d), y_s.reshape(ns, 1, d), conv_p[None],
            kv4(kp3[:, t - WINDOW:], nb), kv4(vp3[:, t - WINDOW:], nb),
            conv_s[None], kv4(k_state_s, ns), kv4(v_state_s, ns))
```

```python
import functools

import jax
import jax.numpy as jnp
from jax import lax
from jax.experimental import pallas as pl
from jax.experimental.pallas import tpu as pltpu

D_MODEL = 1024
CONV_WIDTH = 31
CONV_CTX = CONV_WIDTH - 1
HEAD_DIM = 64
N_HEADS = 16
N_KV_HEADS = 4
GROUP = N_HEADS // N_KV_HEADS
KV_DIM = N_KV_HEADS * HEAD_DIM
WINDOW = 128
D_FF = 2816
N_EXPERTS = 8
D_EXPERT = 3584
PLE_DIM = 256
EPS = 1e-6

BF16 = jnp.bfloat16
F32 = jnp.float32

LANES = 128
SUBLANES = 8
ROW_TILES = D_MODEL // LANES
VMEM_LIMIT = 56 * 1024 * 1024

CONV_PAD = 32
CONV_ROWS = 64
TOK_TILE = 512
MOE_TILE = 512
MOE_F_TILE = 1792
COMBINE_TILE = 512
COPY_UNROLL = 8
NEG_BIG = -1e30


def _params(*sem):
    return pltpu.CompilerParams(dimension_semantics=sem, vmem_limit_bytes=VMEM_LIMIT)


def _const(shape):
    nd = len(shape)
    return pl.BlockSpec(shape, lambda *_: (0,) * nd, pipeline_mode=pl.Buffered(1))


def _rms(x, g):
    return x * lax.rsqrt(jnp.mean(x * x, axis=-1, keepdims=True) + EPS) * g


def _dot(a, w):
    return jnp.dot(a.astype(BF16), w, preferred_element_type=F32)


def _sigmoid(x):
    return 1.0 / (1.0 + jnp.exp(-x))


def _silu(x):
    return x * _sigmoid(x)


def _layer_norm(x, g, b):
    mu = jnp.mean(x, axis=-1, keepdims=True)
    xc = x - mu
    var = jnp.mean(xc * xc, axis=-1, keepdims=True)
    return xc * lax.rsqrt(var + EPS) * g + b


def _head_rms(x, g):
    t, w = x.shape
    lane = lax.broadcasted_iota(jnp.int32, (t, LANES), 1)
    lo = lane < HEAD_DIM
    outs = []
    for c in range(w // LANES):
        blk = x[:, c * LANES:(c + 1) * LANES]
        sq = blk * blk
        s_lo = jnp.sum(jnp.where(lo, sq, 0.0), axis=-1, keepdims=True)
        s_hi = jnp.sum(jnp.where(lo, 0.0, sq), axis=-1, keepdims=True)
        inv = jnp.where(lo, lax.rsqrt(s_lo / HEAD_DIM + EPS), lax.rsqrt(s_hi / HEAD_DIM + EPS))
        outs.append(blk * inv)
    return jnp.concatenate(outs, axis=1) * g


def _rows_to_tiles(ref, x):
    t = x.shape[0]
    for s in range(ROW_TILES):
        ref[pl.ds(s, t, stride=ROW_TILES), :] = x[:, s * LANES:(s + 1) * LANES]


def _tiles_to_rows(ref, t):
    return jnp.concatenate([ref[pl.ds(s, t, stride=ROW_TILES), :] for s in range(ROW_TILES)], axis=1)


def _conv_taps(uext_ref, wdw_ref, bdw_ref, c_ref, tt):
    win_rows = CONV_ROWS + CONV_PAD

    def chunk(ci, carry):
        r0 = pl.multiple_of(ci * CONV_ROWS, CONV_ROWS)
        for lt in range(ROW_TILES):
            cols = slice(lt * LANES, (lt + 1) * LANES)
            win = uext_ref[pl.ds(r0, win_rows), cols]
            acc = jnp.broadcast_to(bdw_ref[:, cols], (CONV_ROWS, LANES))
            for b in range(SUBLANES):
                wb = win if b == 0 else pltpu.roll(win, win_rows - b, axis=0)
                for a in range(CONV_PAD // SUBLANES + 1):
                    j = a * SUBLANES + b - (CONV_PAD - CONV_CTX)
                    if 0 <= j < CONV_WIDTH:
                        acc = acc + wdw_ref[j:j + 1, cols] * wb[a * SUBLANES:a * SUBLANES + CONV_ROWS]
            c_ref[pl.ds(r0, CONV_ROWS), cols] = acc
        return carry
    lax.fori_loop(0, tt // CONV_ROWS, chunk, 0)


def _conv_prompt_kernel(x_ref, nm_ref, w1_ref, b1_ref, wdw_ref, bdw_ref, lng_ref, lnb_ref, w2_ref, b2_ref,
                        h_ref, st_ref, uext_ref, c_ref, *, tt):
    t = pl.program_id(1)

    @pl.when(t == 0)
    def _():
        uext_ref[0:CONV_PAD, :] = jnp.zeros((CONV_PAD, D_MODEL), F32)

    @pl.when(t > 0)
    def _():
        uext_ref[0:CONV_PAD, :] = uext_ref[tt:tt + CONV_PAD, :]

    x = x_ref[0]
    z = _dot(_rms(x, nm_ref[...]), w1_ref[...]) + b1_ref[...]
    u = z[:, :D_MODEL] * _sigmoid(z[:, D_MODEL:])
    uext_ref[CONV_PAD:CONV_PAD + tt, :] = u
    _conv_taps(uext_ref, wdw_ref, bdw_ref, c_ref, tt)
    c = _silu(_layer_norm(c_ref[...], lng_ref[...], lnb_ref[...]))
    h_ref[0] = x + _dot(c, w2_ref[...]) + b2_ref[...]
    st_ref[0] = uext_ref[CONV_PAD + tt - CONV_CTX:CONV_PAD + tt, :]


def _conv_prompt(x, nm, w1, b1, wdw, bdw, lng, lnb, w2, b2):
    n, t, d = x.shape
    tt = TOK_TILE
    weights = (nm, w1, b1, wdw, bdw, lng, lnb, w2, b2)
    return pl.pallas_call(
        functools.partial(_conv_prompt_kernel, tt=tt),
        grid=(n, t // tt),
        in_specs=[pl.BlockSpec((1, tt, d), lambda b, i: (b, i, 0))] + [_const(w.shape) for w in weights],
        out_specs=[pl.BlockSpec((1, tt, d), lambda b, i: (b, i, 0)),
                   pl.BlockSpec((1, CONV_CTX, d), lambda b, i: (b, 0, 0))],
        out_shape=[jax.ShapeDtypeStruct((n, t, d), F32), jax.ShapeDtypeStruct((n, CONV_CTX, d), F32)],
        scratch_shapes=[pltpu.VMEM((CONV_PAD + tt, d), F32), pltpu.VMEM((tt, d), F32)],
        compiler_params=_params("arbitrary", "arbitrary"),
        name="conv_prompt",
    )(x, *weights)


def _conv_sample_kernel(x_ref, past_ref, nm_ref, w1_ref, b1_ref, wdw_ref, bdw_ref, lng_ref, lnb_ref,
                        w2_ref, b2_ref, h_ref, st_ref, c_ref, *, nb):
    x = x_ref[...]
    z = _dot(_rms(x, nm_ref[...]), w1_ref[...]) + b1_ref[...]
    u = z[:, :D_MODEL] * _sigmoid(z[:, D_MODEL:])
    w_past = wdw_ref[0:CONV_CTX, :]
    for i in range(nb):
        past = past_ref[i]
        c_ref[i:i + 1, :] = jnp.sum(past * w_past, axis=0, keepdims=True)
        st_ref[i, 0:CONV_CTX - 1, :] = past[1:CONV_CTX, :]
        st_ref[i, CONV_CTX - 1:CONV_CTX, :] = u[i:i + 1, :]
    c = c_ref[...] + wdw_ref[CONV_CTX:CONV_WIDTH, :] * u + bdw_ref[...]
    c = _silu(_layer_norm(c, lng_ref[...], lnb_ref[...]))
    h_ref[...] = x + _dot(c, w2_ref[...]) + b2_ref[...]


def _conv_sample(x, past, nm, w1, b1, wdw, bdw, lng, lnb, w2, b2):
    n, d = x.shape
    nb = 16
    weights = (nm, w1, b1, wdw, bdw, lng, lnb, w2, b2)
    return pl.pallas_call(
        functools.partial(_conv_sample_kernel, nb=nb),
        grid=(n // nb,),
        in_specs=[pl.BlockSpec((nb, d), lambda i: (i, 0)),
                  pl.BlockSpec((nb, CONV_CTX, d), lambda i: (i, 0, 0))] + [_const(w.shape) for w in weights],
        out_specs=[pl.BlockSpec((nb, d), lambda i: (i, 0)),
                   pl.BlockSpec((nb, CONV_CTX, d), lambda i: (i, 0, 0))],
        out_shape=[jax.ShapeDtypeStruct((n, d), F32), jax.ShapeDtypeStruct((n, CONV_CTX, d), F32)],
        scratch_shapes=[pltpu.VMEM((nb, d), F32)],
        compiler_params=_params("arbitrary"),
        name="conv_sample",
    )(x, past, *weights)


def _ffn_kernel(h_ref, g_ref, wg_ref, wu_ref, wd_ref, o_ref):
    h = h_ref[...]
    hn = _rms(h, g_ref[...]).astype(BF16)
    a = jnp.dot(hn, wg_ref[...], preferred_element_type=F32)
    b = jnp.dot(hn, wu_ref[...], preferred_element_type=F32)
    o_ref[...] = h + _dot(_silu(a) * b, wd_ref[...])


def _ffn(h, g, wg, wu, wd):
    m, d = h.shape
    tm = min(TOK_TILE, m)
    weights = (g, wg, wu, wd)
    return pl.pallas_call(
        _ffn_kernel,
        grid=(m // tm,),
        in_specs=[pl.BlockSpec((tm, d), lambda i: (i, 0))] + [_const(w.shape) for w in weights],
        out_specs=pl.BlockSpec((tm, d), lambda i: (i, 0)),
        out_shape=jax.ShapeDtypeStruct((m, d), F32),
        compiler_params=_params("arbitrary"),
        name="dense_ffn",
    )(h, *weights)


def _ple(h, p, g, w_gate, w_proj):
    return h + _sigmoid(_dot(_rms(h, g), w_gate)) * _dot(p, w_proj)


def _ple_kvq_kernel(h_ref, p_ref, gple_ref, wpg_ref, wpp_ref, gkv_ref, wk_ref, wv_ref, kn_ref,
                    gmix_ref, wq_ref, qn_ref, h3_ref, k_ref, v_ref, q_ref):
    h3 = _ple(h_ref[...], p_ref[...], gple_ref[...], wpg_ref[...], wpp_ref[...])
    h3_ref[...] = h3
    s = _rms(h3, gkv_ref[...]).astype(BF16)
    k_ref[...] = _head_rms(jnp.dot(s, wk_ref[...], preferred_element_type=F32), kn_ref[...])
    v_ref[...] = jnp.dot(s, wv_ref[...], preferred_element_type=F32)
    q = _dot(_rms(h3, gmix_ref[...]), wq_ref[...])
    q_ref[...] = (_head_rms(q, qn_ref[...]) * (HEAD_DIM ** -0.5)).astype(BF16)


def _ple_kvq(h, p, gple, wpg, wpp, gkv, wk, wv, kn, gmix, wq, qn):
    m, d = h.shape
    tm = min(TOK_TILE, m)
    weights = (gple, wpg, wpp, gkv, wk, wv, kn, gmix, wq, qn)
    row = lambda w: pl.BlockSpec((tm, w), lambda i: (i, 0))
    return pl.pallas_call(
        _ple_kvq_kernel,
        grid=(m // tm,),
        in_specs=[row(d), row(PLE_DIM)] + [_const(w.shape) for w in weights],
        out_specs=[row(d), row(KV_DIM), row(KV_DIM), row(d)],
        out_shape=[jax.ShapeDtypeStruct((m, d), F32), jax.ShapeDtypeStruct((m, KV_DIM), F32),
                   jax.ShapeDtypeStruct((m, KV_DIM), F32), jax.ShapeDtypeStruct((m, d), BF16)],
        compiler_params=_params("arbitrary"),
        name="ple_kv_q",
    )(h, p, *weights)


def _softmax_sink(s, sink):
    m = jnp.maximum(jnp.max(s, axis=-1, keepdims=True), sink)
    e = jnp.exp(s - m)
    den = jnp.sum(e, axis=-1, keepdims=True) + jnp.exp(sink - m)
    return e / den


def _attn_prompt_kernel(q_ref, kp_ref, kc_ref, vp_ref, vc_ref, sink_ref, o_ref):
    n = pl.program_id(1)
    q = q_ref[0]
    k = jnp.concatenate([kp_ref[0], kc_ref[0]], axis=0).astype(BF16)
    v = jnp.concatenate([vp_ref[0], vc_ref[0]], axis=0).astype(BF16)
    qi = lax.broadcasted_iota(jnp.int32, (WINDOW, 2 * WINDOW), 0)
    kj = lax.broadcasted_iota(jnp.int32, (WINDOW, 2 * WINDOW), 1)
    dist = qi + WINDOW - kj
    first_key = jnp.where(n > 0, 0, WINDOW)
    mask = (dist >= 0) & (dist <= WINDOW) & (kj >= first_key)
    outs = []
    for h in range(N_HEADS):
        kh = h % N_KV_HEADS
        qh = q[:, h * HEAD_DIM:(h + 1) * HEAD_DIM]
        kk = k[:, kh * HEAD_DIM:(kh + 1) * HEAD_DIM]
        vv = v[:, kh * HEAD_DIM:(kh + 1) * HEAD_DIM]
        s = lax.dot_general(qh, kk, (((1,), (1,)), ((), ())), preferred_element_type=F32)
        s = jnp.where(mask, s, -jnp.inf)
        p = _softmax_sink(s, sink_ref[h])
        outs.append(jnp.dot(p.astype(BF16), vv, preferred_element_type=F32))
    o_ref[0] = jnp.concatenate(outs, axis=1).astype(BF16)


def _attn_prompt(q, k, v, sinks):
    n, t, d = q.shape
    nb = t // WINDOW
    cur = lambda w: pl.BlockSpec((1, WINDOW, w), lambda b, i: (b, i, 0))
    prev = lambda w: pl.BlockSpec((1, WINDOW, w), lambda b, i: (b, jnp.maximum(i - 1, 0), 0))
    return pl.pallas_call(
        _attn_prompt_kernel,
        grid=(n, nb),
        in_specs=[cur(d), prev(KV_DIM), cur(KV_DIM), prev(KV_DIM), cur(KV_DIM),
                  pl.BlockSpec(memory_space=pltpu.SMEM)],
        out_specs=cur(d),
        out_shape=jax.ShapeDtypeStruct((n, t, d), BF16),
        compiler_params=_params("arbitrary", "arbitrary"),
        name="attn_prompt",
    )(q, k, k, v, v, sinks)


def _attn_sample_kernel(q_ref, kc_ref, vc_ref, kn_ref, vn_ref, sink_ref, o_ref, ks_ref, vs_ref, *, nb):
    lane_head = lax.broadcasted_iota(jnp.int32, (N_KV_HEADS, KV_DIM), 1) // HEAD_DIM
    own = lane_head == lax.broadcasted_iota(jnp.int32, (N_KV_HEADS, KV_DIM), 0)
    sink = sink_ref[...]
    rows = []
    for i in range(nb):
        kc = kc_ref[i]
        vc = vc_ref[i]
        kn = kn_ref[i:i + 1, :]
        vn = vn_ref[i:i + 1, :]
        ks_ref[i, 0:WINDOW - 1, :] = kc[1:WINDOW, :]
        ks_ref[i, WINDOW - 1:WINDOW, :] = kn
        vs_ref[i, 0:WINDOW - 1, :] = vc[1:WINDOW, :]
        vs_ref[i, WINDOW - 1:WINDOW, :] = vn
        qrow = q_ref[i:i + 1, :].astype(F32)
        qx = jnp.concatenate(
            [jnp.where(own, jnp.broadcast_to(qrow[:, g * KV_DIM:(g + 1) * KV_DIM], (N_KV_HEADS, KV_DIM)), 0.0)
             for g in range(GROUP)], axis=0)
        s_c = lax.dot_general(qx.astype(BF16), kc.astype(BF16), (((1,), (1,)), ((), ())),
                              preferred_element_type=F32)
        s_n = jnp.sum(qx * kn.astype(BF16).astype(F32), axis=-1, keepdims=True)
        m = jnp.maximum(jnp.maximum(jnp.max(s_c, axis=-1, keepdims=True), s_n), sink)
        e_c = jnp.exp(s_c - m)
        e_n = jnp.exp(s_n - m)
        den = jnp.sum(e_c, axis=-1, keepdims=True) + e_n + jnp.exp(sink - m)
        ox = jnp.dot((e_c / den).astype(BF16), vc.astype(BF16), preferred_element_type=F32)
        ox = ox + (e_n / den).astype(BF16).astype(F32) * vn.astype(BF16).astype(F32)
        rows.append(jnp.concatenate(
            [jnp.sum(jnp.where(own, ox[g * N_KV_HEADS:(g + 1) * N_KV_HEADS], 0.0), axis=0, keepdims=True)
             for g in range(GROUP)], axis=1))
    o_ref[...] = jnp.concatenate(rows, axis=0).astype(BF16)


def _attn_sample(q, kc, vc, kn, vn, sinks_col):
    n, d = q.shape
    nb = 16
    row = lambda w: pl.BlockSpec((nb, w), lambda i: (i, 0))
    cache = pl.BlockSpec((nb, WINDOW, KV_DIM), lambda i: (i, 0, 0))
    return pl.pallas_call(
        functools.partial(_attn_sample_kernel, nb=nb),
        grid=(n // nb,),
        in_specs=[row(d), cache, cache, row(KV_DIM), row(KV_DIM), _const(sinks_col.shape)],
        out_specs=[row(d), cache, cache],
        out_shape=[jax.ShapeDtypeStruct((n, d), BF16),
                   jax.ShapeDtypeStruct((n, WINDOW, KV_DIM), F32),
                   jax.ShapeDtypeStruct((n, WINDOW, KV_DIM), F32)],
        compiler_params=_params("arbitrary"),
        name="attn_sample",
    )(q, kc, vc, kn, vn, sinks_col)


def _oproj_router_kernel(rows_in_ref, o_ref, h_ref, wo_ref, g_ref, wr_ref, br_ref,
                         h4_ref, hn_ref, route_ref, cnt_ref, carry_ref, *, tm):
    del rows_in_ref
    i = pl.program_id(0)

    @pl.when(i == 0)
    def _():
        carry_ref[...] = jnp.zeros((1, LANES), F32)

    h4 = h_ref[...] + jnp.dot(o_ref[...], wo_ref[...], preferred_element_type=F32)
    h4_ref[...] = h4
    hn = _rms(h4, g_ref[...])
    _rows_to_tiles(hn_ref, hn)
    hn_hi = hn.astype(BF16)
    hn_lo = (hn - hn_hi.astype(F32)).astype(BF16)
    part_hi = jnp.dot(hn_hi, wr_ref[...], preferred_element_type=F32)
    part_lo = jnp.dot(hn_lo, wr_ref[...], preferred_element_type=F32)
    logits = (part_hi[:, :LANES] + (part_hi[:, LANES:] + part_lo[:, :LANES]) + part_lo[:, LANES:]) + br_ref[...]
    lane = lax.broadcasted_iota(jnp.int32, (tm, LANES), 1).astype(F32)
    m1 = jnp.max(logits, axis=-1, keepdims=True)
    i1 = jnp.min(jnp.where(logits == m1, lane, float(LANES)), axis=-1, keepdims=True)
    rest = jnp.where(lane == i1, -jnp.inf, logits)
    m2 = jnp.max(rest, axis=-1, keepdims=True)
    i2 = jnp.min(jnp.where(rest == m2, lane, float(LANES)), axis=-1, keepdims=True)
    e2 = jnp.exp(m2 - m1)
    w1 = 1.0 / (1.0 + e2)
    w2 = e2 / (1.0 + e2)
    oh1 = lane == i1
    oh2 = lane == i2
    sel = jnp.where(oh1, 1.0, 0.0) + jnp.where(oh2, 1.0, 0.0)
    row = lax.broadcasted_iota(jnp.int32, (tm, tm), 0)
    col = lax.broadcasted_iota(jnp.int32, (tm, tm), 1)
    tri = jnp.where(row > col, 1.0, 0.0).astype(BF16)
    before = jnp.dot(tri, sel.astype(BF16), preferred_element_type=F32) + carry_ref[...]
    r1 = jnp.sum(jnp.where(oh1, before, 0.0), axis=-1, keepdims=True)
    r2 = jnp.sum(jnp.where(oh2, before, 0.0), axis=-1, keepdims=True)
    carry_ref[...] = carry_ref[...] + jnp.sum(sel, axis=0, keepdims=True)
    cnt_ref[...] = carry_ref[...]
    fields = (i1, i2, r1, r2, w1, w2)
    route = jnp.zeros((tm, LANES), F32)
    for f, val in enumerate(fields):
        route = jnp.where(lane == float(f), val, route)
    route_ref[...] = route


def _oproj_router(rows_buf, o, h, wo, g, wr, br, *, first_token):
    m, d = h.shape
    tm = min(TOK_TILE, m)
    weights = (wo, g, wr, br)
    row = lambda w: pl.BlockSpec((tm, w), lambda i: (i, 0))
    first_block = first_token // tm
    return pl.pallas_call(
        functools.partial(_oproj_router_kernel, tm=tm),
        grid=(m // tm,),
        in_specs=[pl.BlockSpec(memory_space=pl.ANY), row(d), row(d)] + [_const(w.shape) for w in weights],
        out_specs=[row(d), pl.BlockSpec((tm * ROW_TILES, LANES), lambda i: (i + first_block, 0)), row(LANES),
                   pl.BlockSpec((1, LANES), lambda i: (0, 0))],
        out_shape=[jax.ShapeDtypeStruct((m, d), F32), jax.ShapeDtypeStruct(rows_buf.shape, F32),
                   jax.ShapeDtypeStruct((m, LANES), F32), jax.ShapeDtypeStruct((1, LANES), F32)],
        scratch_shapes=[pltpu.VMEM((1, LANES), F32)],
        input_output_aliases={0: 1},
        compiler_params=_params("arbitrary"),
        name="oproj_router",
    )(rows_buf, o, h, *weights)


def _row(ref, r):
    return ref.at[pl.ds(pl.multiple_of(r * ROW_TILES, ROW_TILES), ROW_TILES), :]


def _for_rows(lo, hi, fn):
    assert (hi - lo) % COPY_UNROLL == 0

    def body(j, carry):
        for u in range(COPY_UNROLL):
            fn(lo + j * COPY_UNROLL + u)
        return carry
    lax.fori_loop(0, (hi - lo) // COPY_UNROLL, body, 0)


def _moe_kernel(te_ref, nv_ref, nr_ref, pos_ref, rows_ref, wg_ref, wu_ref, wd_ref, y_ref,
                inv_ref, xg_ref, yb_ref, xb_ref, acc_ref, gsem, ssem, isem, *, tm, nf, nt, n_pairs):
    i = pl.program_id(0)
    f = pl.program_id(1)
    nv = nv_ref[0]
    valid = i < nv
    slot = i % 2
    part = tm // nf

    def gather_row(tile, r, to_slot):
        pltpu.make_async_copy(_row(rows_ref, inv_ref[tile * tm + r] // 2), _row(xg_ref.at[to_slot], r),
                              gsem.at[to_slot]).start()

    def scatter_row(tile, real, r, from_slot):
        pair = jnp.where(r < real, inv_ref[tile * tm + r], n_pairs + from_slot * tm + r)
        pltpu.make_async_copy(_row(yb_ref.at[from_slot], r), _row(y_ref, pair), ssem.at[from_slot]).start()

    def wait_gather(at_slot):
        pltpu.make_async_copy(rows_ref.at[pl.ds(0, tm * ROW_TILES), :], xg_ref.at[at_slot], gsem.at[at_slot]).wait()

    def wait_scatter(at_slot):
        pltpu.make_async_copy(yb_ref.at[at_slot], y_ref.at[pl.ds(0, tm * ROW_TILES), :], ssem.at[at_slot]).wait()

    @pl.when(jnp.logical_and(i == 0, f == 0))
    def _():
        def place(p):
            inv_ref[pos_ref[p]] = p
        _for_rows(0, n_pairs, place)

        def clear_padding(t, carry):
            def clear(r, c):
                inv_ref[t * tm + r] = 0
                return c
            return lax.fori_loop(nr_ref[t], tm, clear, carry)
        lax.fori_loop(0, nv, clear_padding, 0)
        yb_ref[...] = jnp.zeros(yb_ref.shape, F32)
        spare = y_ref.at[pl.ds(n_pairs * ROW_TILES, 2 * tm * ROW_TILES), :]
        for s in range(2):
            half = spare.at[pl.ds(s * tm * ROW_TILES, tm * ROW_TILES), :]
            pltpu.make_async_copy(yb_ref.at[s], half, isem).start()
            pltpu.make_async_copy(yb_ref.at[s], half, isem).wait()
        _for_rows(0, tm, lambda r: gather_row(0, r, 0))

    @pl.when(jnp.logical_and(valid, f == 0))
    def _():
        wait_gather(slot)
        xb_ref[...] = _tiles_to_rows(xg_ref.at[slot], tm).astype(BF16)
        acc_ref[...] = jnp.zeros((tm, D_MODEL), F32)

    def matmuls(first):
        nxt = jnp.minimum(i + 1, nv - 1)
        for r in range(part):
            gather_row(nxt, f * part + r, 1 - slot)
        if first:
            prev = jnp.maximum(i - 1, 0)
            real = jnp.where(i >= 1, nr_ref[prev], 0)
            for r in range(tm):
                scatter_row(prev, real, r, 1 - slot)
        x = xb_ref[...]
        a = jnp.dot(x, wg_ref[0], preferred_element_type=F32)
        b = jnp.dot(x, wu_ref[0], preferred_element_type=F32)
        acc_ref[...] += _dot(_silu(a) * b, wd_ref[0])

    @pl.when(jnp.logical_and(valid, f == 0))
    def _():
        matmuls(True)

    @pl.when(jnp.logical_and(valid, f > 0))
    def _():
        matmuls(False)

    @pl.when(jnp.logical_and(valid, f == nf - 1))
    def _():
        @pl.when(i >= 1)
        def _():
            wait_scatter(slot)
        _rows_to_tiles(yb_ref.at[slot], acc_ref[...])

        @pl.when(i == nv - 1)
        def _():
            _for_rows(0, tm, lambda r: scatter_row(i, nr_ref[i], r, slot))

    @pl.when(jnp.logical_and(i == nt - 1, f == nf - 1))
    def _():
        wait_gather(nv % 2)
        wait_scatter(nv % 2)
        wait_scatter((nv - 1) % 2)


def _moe(tile_expert, n_valid, tile_rows, pos, rows, wg, wu, wd):
    tm, tf = MOE_TILE, MOE_F_TILE
    nt, nf = tile_expert.shape[0], D_EXPERT // tf
    n_pairs = pos.shape[0]

    def fidx(i, f, nv):
        return jnp.where(i < nv[0], f, nf - 1)
    tile_buf = pltpu.VMEM((2, tm * ROW_TILES, LANES), F32)
    return pl.pallas_call(
        functools.partial(_moe_kernel, tm=tm, nf=nf, nt=nt, n_pairs=n_pairs),
        grid_spec=pltpu.PrefetchScalarGridSpec(
            num_scalar_prefetch=4,
            grid=(nt, nf),
            in_specs=[pl.BlockSpec(memory_space=pl.ANY),
                      pl.BlockSpec((1, D_MODEL, tf), lambda i, f, te, nv, *_: (te[i], 0, fidx(i, f, nv))),
                      pl.BlockSpec((1, D_MODEL, tf), lambda i, f, te, nv, *_: (te[i], 0, fidx(i, f, nv))),
                      pl.BlockSpec((1, tf, D_MODEL), lambda i, f, te, nv, *_: (te[i], fidx(i, f, nv), 0))],
            out_specs=pl.BlockSpec(memory_space=pl.ANY),
            scratch_shapes=[pltpu.SMEM((nt * tm,), jnp.int32), tile_buf, tile_buf,
                            pltpu.VMEM((tm, D_MODEL), BF16), pltpu.VMEM((tm, D_MODEL), F32),
                            pltpu.SemaphoreType.DMA((2,)), pltpu.SemaphoreType.DMA((2,)),
                            pltpu.SemaphoreType.DMA(())],
        ),
        out_shape=jax.ShapeDtypeStruct(((n_pairs + 2 * tm) * ROW_TILES, LANES), F32),
        compiler_params=_params("arbitrary", "arbitrary"),
        name="moe_grouped",
    )(tile_expert, n_valid, tile_rows, pos, rows, wg, wu, wd)


def _combine_kernel(y_ref, h_ref, route_ref, p_ref, g_ref, wpg_ref, wpp_ref, o_ref, *, tb):
    route = route_ref[...]
    w1 = route[:, 4:5]
    w2 = route[:, 5:6]
    pair = 2 * ROW_TILES
    y1 = jnp.concatenate([y_ref[pl.ds(s, tb, stride=pair), :] for s in range(ROW_TILES)], axis=1)
    y2 = jnp.concatenate([y_ref[pl.ds(ROW_TILES + s, tb, stride=pair), :] for s in range(ROW_TILES)], axis=1)
    h5 = h_ref[...] + (w1 * y1 + w2 * y2)
    o_ref[...] = _ple(h5, p_ref[...], g_ref[...], wpg_ref[...], wpp_ref[...])


def _combine(y_pairs, h, route, p, g, wpg, wpp, *, first_token):
    m, d = h.shape
    tb = min(COMBINE_TILE, m)
    weights = (g, wpg, wpp)
    row = lambda w: pl.BlockSpec((tb, w), lambda i: (i, 0))
    first_block = first_token // tb
    return pl.pallas_call(
        functools.partial(_combine_kernel, tb=tb),
        grid=(m // tb,),
        in_specs=[pl.BlockSpec((tb * 2 * ROW_TILES, LANES), lambda i: (i + first_block, 0)),
                  row(d), row(LANES), row(PLE_DIM)] + [_const(w.shape) for w in weights],
        out_specs=row(d),
        out_shape=jax.ShapeDtypeStruct((m, d), F32),
        compiler_params=_params("arbitrary"),
        name="moe_combine",
    )(y_pairs, h, route, p, *weights)


def _row2(v):
    return v.reshape(1, -1).astype(F32)


def kernel(x_prompt, x_sample, state_conv, cache_k, cache_v, p_prompt, p_sample, norm_mix, norm_ffn, norm_ple,
           conv_w_pw1, conv_b_pw1, conv_w_dw, conv_b_dw, conv_ln_g, conv_ln_b, conv_w_pw2, conv_b_pw2,
           norm_kv, w_k, w_v, k_norm, w_q, q_norm, sinks, w_o, ffn_w_gate, ffn_w_up, ffn_w_down,
           moe_w_router, moe_b_router, moe_w_gate, moe_w_up, moe_w_down, ple_w_gate, ple_w_proj):
    nb, t, d = x_prompt.shape
    ns = x_sample.shape[0]
    mp = nb * t
    bf = lambda w: w.astype(BF16)

    conv_w = (_row2(norm_mix[0]), bf(conv_w_pw1[0]), _row2(conv_b_pw1[0]), conv_w_dw[0], _row2(conv_b_dw[0]),
              _row2(conv_ln_g[0]), _row2(conv_ln_b[0]), bf(conv_w_pw2[0]), _row2(conv_b_pw2[0]))
    ffn_w = (_row2(norm_ffn[0]), bf(ffn_w_gate[0]), bf(ffn_w_up[0]), bf(ffn_w_down[0]))
    wq = w_q[0].reshape(d, N_KV_HEADS, GROUP, HEAD_DIM).transpose(0, 2, 1, 3).reshape(d, d)
    wo = w_o[0].reshape(N_KV_HEADS, GROUP, HEAD_DIM, d).transpose(1, 0, 2, 3).reshape(d, d)
    sinks_gk = sinks[0].astype(F32).reshape(N_KV_HEADS, GROUP).T.reshape(N_HEADS)
    kvq_w = (_row2(norm_ple[0]), bf(ple_w_gate[0]), bf(ple_w_proj[0]), _row2(norm_kv), bf(w_k), bf(w_v),
             _row2(jnp.tile(k_norm, N_KV_HEADS)), _row2(norm_mix[1]), bf(wq), _row2(jnp.tile(q_norm[0], N_HEADS)))
    wr = jnp.zeros((d, LANES), F32).at[:, :N_EXPERTS].set(moe_w_router[0])
    wr_hi = bf(wr)
    wr_lo = bf(wr - wr_hi.astype(F32))
    br = jnp.full((1, LANES), NEG_BIG, F32).at[0, :N_EXPERTS].set(moe_b_router[0])
    router_w = (bf(wo), _row2(norm_ffn[1]), jnp.concatenate([wr_hi, wr_lo], axis=1), br)
    ple1_w = (_row2(norm_ple[1]), bf(ple_w_gate[1]), bf(ple_w_proj[1]))

    h1p, conv_p = _conv_prompt(x_prompt, *conv_w)
    h3p, kp, vp, qp = _ple_kvq(_ffn(h1p.reshape(mp, d), *ffn_w), p_prompt[0].reshape(mp, PLE_DIM), *kvq_w)
    h1s, conv_s = _conv_sample(x_sample.reshape(ns, d), state_conv[0], *conv_w)
    h3s, ks, vs, qs = _ple_kvq(_ffn(h1s, *ffn_w), p_sample[0].reshape(ns, PLE_DIM), *kvq_w)

    kp3 = kp.reshape(nb, t, KV_DIM)
    vp3 = vp.reshape(nb, t, KV_DIM)
    op = _attn_prompt(qp.reshape(nb, t, d), kp3, vp3, sinks_gk)
    os_, k_state_s, v_state_s = _attn_sample(qs, cache_k.reshape(ns, WINDOW, KV_DIM),
                                             cache_v.reshape(ns, WINDOW, KV_DIM), ks, vs,
                                             sinks_gk.reshape(N_HEADS, 1))

    mt = mp + ns
    rows = jnp.zeros((mt * ROW_TILES, LANES), F32)
    h4p, rows, route_p, cnt_p = _oproj_router(rows, op.reshape(mp, d), h3p, *router_w, first_token=0)
    h4s, rows, route_s, cnt_s = _oproj_router(rows, os_, h3s, *router_w, first_token=mp)

    cnt_p = cnt_p[0, :N_EXPERTS].astype(jnp.int32)
    cnt_s = cnt_s[0, :N_EXPERTS].astype(jnp.int32)
    group = (cnt_p + cnt_s + MOE_TILE - 1) // MOE_TILE * MOE_TILE
    ends = jnp.cumsum(group)
    starts = ends - group
    n_tiles = (2 * (mp + ns) + N_EXPERTS * (MOE_TILE - 1)) // MOE_TILE
    n_valid = (ends[-1] // MOE_TILE).astype(jnp.int32)
    tile_start = jnp.minimum(jnp.arange(n_tiles, dtype=jnp.int32), n_valid - 1) * MOE_TILE
    tile_expert = jnp.sum(tile_start[:, None] >= ends[None, :], axis=1).astype(jnp.int32)

    def positions(route, base):
        idx = route[:, 0:2].astype(jnp.int32)
        rank = route[:, 2:4].astype(jnp.int32)
        return (starts[idx] + base[idx] + rank).reshape(-1)
    pos = jnp.concatenate([positions(route_p, jnp.zeros_like(cnt_p)), positions(route_s, cnt_p)])
    tile_ids = jnp.arange(n_tiles, dtype=jnp.int32)
    rows_left = (starts + cnt_p + cnt_s)[tile_expert] - tile_ids * MOE_TILE
    tile_rows = jnp.where(tile_ids < n_valid, jnp.clip(rows_left, 0, MOE_TILE), 0).astype(jnp.int32)
    y_pairs = _moe(tile_expert, n_valid.reshape(1), tile_rows, pos.astype(jnp.int32), rows,
                   bf(moe_w_gate[0]), bf(moe_w_up[0]), bf(moe_w_down[0]))

    y_p = _combine(y_pairs, h4p, route_p, p_prompt[1].reshape(mp, PLE_DIM), *ple1_w, first_token=0)
    y_s = _combine(y_pairs, h4s, route_s, p_sample[1].reshape(ns, PLE_DIM), *ple1_w, first_token=mp)

    kv4 = lambda a, n: a.reshape(n, WINDOW, N_KV_HEADS, HEAD_DIM)
    return (y_p.reshape(nb, t, d), y_s.reshape(ns, 1, d), conv_p[None],
            kv4(kp3[:, t - WINDOW:], nb), kv4(vp3[:, t - WINDOW:], nb),
            conv_s[None], kv4(k_state_s, ns), kv4(v_state_s, ns))
```

```python
import functools

import jax
import jax.numpy as jnp
from jax import lax
from jax.experimental import pallas as pl
from jax.experimental.pallas import tpu as pltpu

D_MODEL = 1024
CONV_WIDTH = 31
CONV_CTX = CONV_WIDTH - 1
HEAD_DIM = 64
N_HEADS = 16
N_KV_HEADS = 4
GROUP = N_HEADS // N_KV_HEADS
KV_DIM = N_KV_HEADS * HEAD_DIM
WINDOW = 128
D_FF = 2816
N_EXPERTS = 8
D_EXPERT = 3584
PLE_DIM = 256
EPS = 1e-6

BF16 = jnp.bfloat16
F32 = jnp.float32

LANES = 128
SUBLANES = 8
ROW_TILES = D_MODEL // LANES
VMEM_LIMIT = 56 * 1024 * 1024

CONV_PAD = 32
CONV_ROWS = 64
TOK_TILE = 512
MOE_TILE = 512
MOE_F_TILE = 1792
COMBINE_TILE = 512
COPY_UNROLL = 8
NEG_BIG = -1e30


def _params(*sem):
    return pltpu.CompilerParams(dimension_semantics=sem, vmem_limit_bytes=VMEM_LIMIT)


def _const(shape):
    nd = len(shape)
    return pl.BlockSpec(shape, lambda *_: (0,) * nd, pipeline_mode=pl.Buffered(1))


def _rms(x, g):
    return x * lax.rsqrt(jnp.mean(x * x, axis=-1, keepdims=True) + EPS) * g


def _dot(a, w):
    return jnp.dot(a.astype(BF16), w, preferred_element_type=F32)


def _sigmoid(x):
    return 1.0 / (1.0 + jnp.exp(-x))


def _silu(x):
    return x * _sigmoid(x)


def _layer_norm(x, g, b):
    mu = jnp.mean(x, axis=-1, keepdims=True)
    xc = x - mu
    var = jnp.mean(xc * xc, axis=-1, keepdims=True)
    return xc * lax.rsqrt(var + EPS) * g + b


def _head_rms(x, g):
    t, w = x.shape
    lane = lax.broadcasted_iota(jnp.int32, (t, LANES), 1)
    lo = lane < HEAD_DIM
    outs = []
    for c in range(w // LANES):
        blk = x[:, c * LANES:(c + 1) * LANES]
        sq = blk * blk
        s_lo = jnp.sum(jnp.where(lo, sq, 0.0), axis=-1, keepdims=True)
        s_hi = jnp.sum(jnp.where(lo, 0.0, sq), axis=-1, keepdims=True)
        inv = jnp.where(lo, lax.rsqrt(s_lo / HEAD_DIM + EPS), lax.rsqrt(s_hi / HEAD_DIM + EPS))
        outs.append(blk * inv)
    return jnp.concatenate(outs, axis=1) * g


def _rows_to_tiles(ref, x):
    t = x.shape[0]
    for s in range(ROW_TILES):
        ref[pl.ds(s, t, stride=ROW_TILES), :] = x[:, s * LANES:(s + 1) * LANES]


def _tiles_to_rows(ref, t):
    return jnp.concatenate([ref[pl.ds(s, t, stride=ROW_TILES), :] for s in range(ROW_TILES)], axis=1)


def _conv_taps(uext_ref, wdw_ref, bdw_ref, c_ref, tt):
    win_rows = CONV_ROWS + CONV_PAD

    def chunk(ci, carry):
        r0 = pl.multiple_of(ci * CONV_ROWS, CONV_ROWS)
        for lt in range(ROW_TILES):
            cols = slice(lt * LANES, (lt + 1) * LANES)
            win = uext_ref[pl.ds(r0, win_rows), cols]
            acc = jnp.broadcast_to(bdw_ref[:, cols], (CONV_ROWS, LANES))
            for b in range(SUBLANES):
                wb = win if b == 0 else pltpu.roll(win, win_rows - b, axis=0)
                for a in range(CONV_PAD // SUBLANES + 1):
                    j = a * SUBLANES + b - (CONV_PAD - CONV_CTX)
                    if 0 <= j < CONV_WIDTH:
                        acc = acc + wdw_ref[j:j + 1, cols] * wb[a * SUBLANES:a * SUBLANES + CONV_ROWS]
            c_ref[pl.ds(r0, CONV_ROWS), cols] = acc
        return carry
    lax.fori_loop(0, tt // CONV_ROWS, chunk, 0)


def _conv_prompt_kernel(x_ref, nm_ref, w1_ref, b1_ref, wdw_ref, bdw_ref, lng_ref, lnb_ref, w2_ref, b2_ref,
                        h_ref, st_ref, uext_ref, c_ref, *, tt):
    t = pl.program_id(1)

    @pl.when(t == 0)
    def _():
        uext_ref[0:CONV_PAD, :] = jnp.zeros((CONV_PAD, D_MODEL), F32)

    @pl.when(t > 0)
    def _():
        uext_ref[0:CONV_PAD, :] = uext_ref[tt:tt + CONV_PAD, :]

    x = x_ref[0]
    z = _dot(_rms(x, nm_ref[...]), w1_ref[...]) + b1_ref[...]
    u = z[:, :D_MODEL] * _sigmoid(z[:, D_MODEL:])
    uext_ref[CONV_PAD:CONV_PAD + tt, :] = u
    _conv_taps(uext_ref, wdw_ref, bdw_ref, c_ref, tt)
    c = _silu(_layer_norm(c_ref[...], lng_ref[...], lnb_ref[...]))
    h_ref[0] = x + _dot(c, w2_ref[...]) + b2_ref[...]
    st_ref[0] = uext_ref[CONV_PAD + tt - CONV_CTX:CONV_PAD + tt, :]


def _conv_prompt(x, nm, w1, b1, wdw, bdw, lng, lnb, w2, b2):
    n, t, d = x.shape
    tt = TOK_TILE
    weights = (nm, w1, b1, wdw, bdw, lng, lnb, w2, b2)
    return pl.pallas_call(
        functools.partial(_conv_prompt_kernel, tt=tt),
        grid=(n, t // tt),
        in_specs=[pl.BlockSpec((1, tt, d), lambda b, i: (b, i, 0))] + [_const(w.shape) for w in weights],
        out_specs=[pl.BlockSpec((1, tt, d), lambda b, i: (b, i, 0)),
                   pl.BlockSpec((1, CONV_CTX, d), lambda b, i: (b, 0, 0))],
        out_shape=[jax.ShapeDtypeStruct((n, t, d), F32), jax.ShapeDtypeStruct((n, CONV_CTX, d), F32)],
        scratch_shapes=[pltpu.VMEM((CONV_PAD + tt, d), F32), pltpu.VMEM((tt, d), F32)],
        compiler_params=_params("arbitrary", "arbitrary"),
        name="conv_prompt",
    )(x, *weights)


def _conv_sample_kernel(x_ref, past_ref, nm_ref, w1_ref, b1_ref, wdw_ref, bdw_ref, lng_ref, lnb_ref,
                        w2_ref, b2_ref, h_ref, st_ref, c_ref, *, nb):
    x = x_ref[...]
    z = _dot(_rms(x, nm_ref[...]), w1_ref[...]) + b1_ref[...]
    u = z[:, :D_MODEL] * _sigmoid(z[:, D_MODEL:])
    w_past = wdw_ref[0:CONV_CTX, :]
    for i in range(nb):
        past = past_ref[i]
        c_ref[i:i + 1, :] = jnp.sum(past * w_past, axis=0, keepdims=True)
        st_ref[i, 0:CONV_CTX - 1, :] = past[1:CONV_CTX, :]
        st_ref[i, CONV_CTX - 1:CONV_CTX, :] = u[i:i + 1, :]
    c = c_ref[...] + wdw_ref[CONV_CTX:CONV_WIDTH, :] * u + bdw_ref[...]
    c = _silu(_layer_norm(c, lng_ref[...], lnb_ref[...]))
    h_ref[...] = x + _dot(c, w2_ref[...]) + b2_ref[...]


def _conv_sample(x, past, nm, w1, b1, wdw, bdw, lng, lnb, w2, b2):
    n, d = x.shape
    nb = 16
    weights = (nm, w1, b1, wdw, bdw, lng, lnb, w2, b2)
    return pl.pallas_call(
        functools.partial(_conv_sample_kernel, nb=nb),
        grid=(n // nb,),
        in_specs=[pl.BlockSpec((nb, d), lambda i: (i, 0)),
                  pl.BlockSpec((nb, CONV_CTX, d), lambda i: (i, 0, 0))] + [_const(w.shape) for w in weights],
        out_specs=[pl.BlockSpec((nb, d), lambda i: (i, 0)),
                   pl.BlockSpec((nb, CONV_CTX, d), lambda i: (i, 0, 0))],
        out_shape=[jax.ShapeDtypeStruct((n, d), F32), jax.ShapeDtypeStruct((n, CONV_CTX, d), F32)],
        scratch_shapes=[pltpu.VMEM((nb, d), F32)],
        compiler_params=_params("arbitrary"),
        name="conv_sample",
    )(x, past, *weights)


def _ffn_kernel(h_ref, g_ref, wg_ref, wu_ref, wd_ref, o_ref):
    h = h_ref[...]
    hn = _rms(h, g_ref[...]).astype(BF16)
    a = jnp.dot(hn, wg_ref[...], preferred_element_type=F32)
    b = jnp.dot(hn, wu_ref[...], preferred_element_type=F32)
    o_ref[...] = h + _dot(_silu(a) * b, wd_ref[...])


def _ffn(h, g, wg, wu, wd):
    m, d = h.shape
    tm = min(TOK_TILE, m)
    weights = (g, wg, wu, wd)
    return pl.pallas_call(
        _ffn_kernel,
        grid=(m // tm,),
        in_specs=[pl.BlockSpec((tm, d), lambda i: (i, 0))] + [_const(w.shape) for w in weights],
        out_specs=pl.BlockSpec((tm, d), lambda i: (i, 0)),
        out_shape=jax.ShapeDtypeStruct((m, d), F32),
        compiler_params=_params("arbitrary"),
        name="dense_ffn",
    )(h, *weights)


def _ple(h, p, g, w_gate, w_proj):
    return h + _sigmoid(_dot(_rms(h, g), w_gate)) * _dot(p, w_proj)


def _ple_kvq_kernel(h_ref, p_ref, gple_ref, wpg_ref, wpp_ref, gkv_ref, wk_ref, wv_ref, kn_ref,
                    gmix_ref, wq_ref, qn_ref, h3_ref, k_ref, v_ref, q_ref):
    h3 = _ple(h_ref[...], p_ref[...], gple_ref[...], wpg_ref[...], wpp_ref[...])
    h3_ref[...] = h3
    s = _rms(h3, gkv_ref[...]).astype(BF16)
    k_ref[...] = _head_rms(jnp.dot(s, wk_ref[...], preferred_element_type=F32), kn_ref[...])
    v_ref[...] = jnp.dot(s, wv_ref[...], preferred_element_type=F32)
    q = _dot(_rms(h3, gmix_ref[...]), wq_ref[...])
    q_ref[...] = (_head_rms(q, qn_ref[...]) * (HEAD_DIM ** -0.5)).astype(BF16)


def _ple_kvq(h, p, gple, wpg, wpp, gkv, wk, wv, kn, gmix, wq, qn):
    m, d = h.shape
    tm = min(TOK_TILE, m)
    weights = (gple, wpg, wpp, gkv, wk, wv, kn, gmix, wq, qn)
    row = lambda w: pl.BlockSpec((tm, w), lambda i: (i, 0))
    return pl.pallas_call(
        _ple_kvq_kernel,
        grid=(m // tm,),
        in_specs=[row(d), row(PLE_DIM)] + [_const(w.shape) for w in weights],
        out_specs=[row(d), row(KV_DIM), row(KV_DIM), row(d)],
        out_shape=[jax.ShapeDtypeStruct((m, d), F32), jax.ShapeDtypeStruct((m, KV_DIM), F32),
                   jax.ShapeDtypeStruct((m, KV_DIM), F32), jax.ShapeDtypeStruct((m, d), BF16)],
        compiler_params=_params("arbitrary"),
        name="ple_kv_q",
    )(h, p, *weights)


def _softmax_sink(s, sink):
    m = jnp.maximum(jnp.max(s, axis=-1, keepdims=True), sink)
    e = jnp.exp(s - m)
    den = jnp.sum(e, axis=-1, keepdims=True) + jnp.exp(sink - m)
    return e / den


def _attn_prompt_kernel(q_ref, kp_ref, kc_ref, vp_ref, vc_ref, sink_ref, o_ref):
    n = pl.program_id(1)
    q = q_ref[0]
    k = jnp.concatenate([kp_ref[0], kc_ref[0]], axis=0).astype(BF16)
    v = jnp.concatenate([vp_ref[0], vc_ref[0]], axis=0).astype(BF16)
    qi = lax.broadcasted_iota(jnp.int32, (WINDOW, 2 * WINDOW), 0)
    kj = lax.broadcasted_iota(jnp.int32, (WINDOW, 2 * WINDOW), 1)
    dist = qi + WINDOW - kj
    first_key = jnp.where(n > 0, 0, WINDOW)
    mask = (dist >= 0) & (dist <= WINDOW) & (kj >= first_key)
    outs = []
    for h in range(N_HEADS):
        kh = h % N_KV_HEADS
        qh = q[:, h * HEAD_DIM:(h + 1) * HEAD_DIM]
        kk = k[:, kh * HEAD_DIM:(kh + 1) * HEAD_DIM]
        vv = v[:, kh * HEAD_DIM:(kh + 1) * HEAD_DIM]
        s = lax.dot_general(qh, kk, (((1,), (1,)), ((), ())), preferred_element_type=F32)
        s = jnp.where(mask, s, -jnp.inf)
        p = _softmax_sink(s, sink_ref[h])
        outs.append(jnp.dot(p.astype(BF16), vv, preferred_element_type=F32))
    o_ref[0] = jnp.concatenate(outs, axis=1).astype(BF16)


def _attn_prompt(q, k, v, sinks):
    n, t, d = q.shape
    nb = t // WINDOW
    cur = lambda w: pl.BlockSpec((1, WINDOW, w), lambda b, i: (b, i, 0))
    prev = lambda w: pl.BlockSpec((1, WINDOW, w), lambda b, i: (b, jnp.maximum(i - 1, 0), 0))
    return pl.pallas_call(
        _attn_prompt_kernel,
        grid=(n, nb),
        in_specs=[cur(d), prev(KV_DIM), cur(KV_DIM), prev(KV_DIM), cur(KV_DIM),
                  pl.BlockSpec(memory_space=pltpu.SMEM)],
        out_specs=cur(d),
        out_shape=jax.ShapeDtypeStruct((n, t, d), BF16),
        compiler_params=_params("arbitrary", "arbitrary"),
        name="attn_prompt",
    )(q, k, k, v, v, sinks)


def _attn_sample_kernel(q_ref, kc_ref, vc_ref, kn_ref, vn_ref, sink_ref, o_ref, ks_ref, vs_ref, *, nb):
    lane_head = lax.broadcasted_iota(jnp.int32, (N_KV_HEADS, KV_DIM), 1) // HEAD_DIM
    own = lane_head == lax.broadcasted_iota(jnp.int32, (N_KV_HEADS, KV_DIM), 0)
    sink = sink_ref[...]
    rows = []
    for i in range(nb):
        kc = kc_ref[i]
        vc = vc_ref[i]
        kn = kn_ref[i:i + 1, :]
        vn = vn_ref[i:i + 1, :]
        ks_ref[i, 0:WINDOW - 1, :] = kc[1:WINDOW, :]
        ks_ref[i, WINDOW - 1:WINDOW, :] = kn
        vs_ref[i, 0:WINDOW - 1, :] = vc[1:WINDOW, :]
        vs_ref[i, WINDOW - 1:WINDOW, :] = vn
        qrow = q_ref[i:i + 1, :].astype(F32)
        qx = jnp.concatenate(
            [jnp.where(own, jnp.broadcast_to(qrow[:, g * KV_DIM:(g + 1) * KV_DIM], (N_KV_HEADS, KV_DIM)), 0.0)
             for g in range(GROUP)], axis=0)
        s_c = lax.dot_general(qx.astype(BF16), kc.astype(BF16), (((1,), (1,)), ((), ())),
                              preferred_element_type=F32)
        s_n = jnp.sum(qx * kn.astype(BF16).astype(F32), axis=-1, keepdims=True)
        m = jnp.maximum(jnp.maximum(jnp.max(s_c, axis=-1, keepdims=True), s_n), sink)
        e_c = jnp.exp(s_c - m)
        e_n = jnp.exp(s_n - m)
        den = jnp.sum(e_c, axis=-1, keepdims=True) + e_n + jnp.exp(sink - m)
        ox = jnp.dot((e_c / den).astype(BF16), vc.astype(BF16), preferred_element_type=F32)
        ox = ox + (e_n / den).astype(BF16).astype(F32) * vn.astype(BF16).astype(F32)
        rows.append(jnp.concatenate(
            [jnp.sum(jnp.where(own, ox[g * N_KV_HEADS:(g + 1) * N_KV_HEADS], 0.0), axis=0, keepdims=True)
             for g in range(GROUP)], axis=1))
    o_ref[...] = jnp.concatenate(rows, axis=0).astype(BF16)


def _attn_sample(q, kc, vc, kn, vn, sinks_col):
    n, d = q.shape
    nb = 16
    row = lambda w: pl.BlockSpec((nb, w), lambda i: (i, 0))
    cache = pl.BlockSpec((nb, WINDOW, KV_DIM), lambda i: (i, 0, 0))
    return pl.pallas_call(
        functools.partial(_attn_sample_kernel, nb=nb),
        grid=(n // nb,),
        in_specs=[row(d), cache, cache, row(KV_DIM), row(KV_DIM), _const(sinks_col.shape)],
        out_specs=[row(d), cache, cache],
        out_shape=[jax.ShapeDtypeStruct((n, d), BF16),
                   jax.ShapeDtypeStruct((n, WINDOW, KV_DIM), F32),
                   jax.ShapeDtypeStruct((n, WINDOW, KV_DIM), F32)],
        compiler_params=_params("arbitrary"),
        name="attn_sample",
    )(q, kc, vc, kn, vn, sinks_col)


def _oproj_router_kernel(rows_in_ref, o_ref, h_ref, wo_ref, g_ref, wr_ref, br_ref,
                         h4_ref, hn_ref, route_ref, cnt_ref, carry_ref, *, tm):
    del rows_in_ref
    i = pl.program_id(0)

    @pl.when(i == 0)
    def _():
        carry_ref[...] = jnp.zeros((1, LANES), F32)

    h4 = h_ref[...] + jnp.dot(o_ref[...], wo_ref[...], preferred_element_type=F32)
    h4_ref[...] = h4
    hn = _rms(h4, g_ref[...])
    _rows_to_tiles(hn_ref, hn)
    hn_hi = hn.astype(BF16)
    hn_lo = (hn - hn_hi.astype(F32)).astype(BF16)
    part_hi = jnp.dot(hn_hi, wr_ref[...], preferred_element_type=F32)
    part_lo = jnp.dot(hn_lo, wr_ref[...], preferred_element_type=F32)
    logits = (part_hi[:, :LANES] + (part_hi[:, LANES:] + part_lo[:, :LANES]) + part_lo[:, LANES:]) + br_ref[...]
    lane = lax.broadcasted_iota(jnp.int32, (tm, LANES), 1).astype(F32)
    m1 = jnp.max(logits, axis=-1, keepdims=True)
    i1 = jnp.min(jnp.where(logits == m1, lane, float(LANES)), axis=-1, keepdims=True)
    rest = jnp.where(lane == i1, -jnp.inf, logits)
    m2 = jnp.max(rest, axis=-1, keepdims=True)
    i2 = jnp.min(jnp.where(rest == m2, lane, float(LANES)), axis=-1, keepdims=True)
    e2 = jnp.exp(m2 - m1)
    w1 = 1.0 / (1.0 + e2)
    w2 = e2 / (1.0 + e2)
    oh1 = lane == i1
    oh2 = lane == i2
    sel = jnp.where(oh1, 1.0, 0.0) + jnp.where(oh2, 1.0, 0.0)
    row = lax.broadcasted_iota(jnp.int32, (tm, tm), 0)
    col = lax.broadcasted_iota(jnp.int32, (tm, tm), 1)
    tri = jnp.where(row > col, 1.0, 0.0).astype(BF16)
    before = jnp.dot(tri, sel.astype(BF16), preferred_element_type=F32) + carry_ref[...]
    r1 = jnp.sum(jnp.where(oh1, before, 0.0), axis=-1, keepdims=True)
    r2 = jnp.sum(jnp.where(oh2, before, 0.0), axis=-1, keepdims=True)
    carry_ref[...] = carry_ref[...] + jnp.sum(sel, axis=0, keepdims=True)
    cnt_ref[...] = carry_ref[...]
    fields = (i1, i2, r1, r2, w1, w2)
    route = jnp.zeros((tm, LANES), F32)
    for f, val in enumerate(fields):
        route = jnp.where(lane == float(f), val, route)
    route_ref[...] = route


def _oproj_router(rows_buf, o, h, wo, g, wr, br, *, first_token):
    m, d = h.shape
    tm = min(TOK_TILE, m)
    weights = (wo, g, wr, br)
    row = lambda w: pl.BlockSpec((tm, w), lambda i: (i, 0))
    first_block = first_token // tm
    return pl.pallas_call(
        functools.partial(_oproj_router_kernel, tm=tm),
        grid=(m // tm,),
        in_specs=[pl.BlockSpec(memory_space=pl.ANY), row(d), row(d)] + [_const(w.shape) for w in weights],
        out_specs=[row(d), pl.BlockSpec((tm * ROW_TILES, LANES), lambda i: (i + first_block, 0)), row(LANES),
                   pl.BlockSpec((1, LANES), lambda i: (0, 0))],
        out_shape=[jax.ShapeDtypeStruct((m, d), F32), jax.ShapeDtypeStruct(rows_buf.shape, F32),
                   jax.ShapeDtypeStruct((m, LANES), F32), jax.ShapeDtypeStruct((1, LANES), F32)],
        scratch_shapes=[pltpu.VMEM((1, LANES), F32)],
        input_output_aliases={0: 1},
        compiler_params=_params("arbitrary"),
        name="oproj_router",
    )(rows_buf, o, h, *weights)


def _row(ref, r):
    return ref.at[pl.ds(pl.multiple_of(r * ROW_TILES, ROW_TILES), ROW_TILES), :]


def _for_rows(lo, hi, fn):
    assert (hi - lo) % COPY_UNROLL == 0

    def body(j, carry):
        for u in range(COPY_UNROLL):
            fn(lo + j * COPY_UNROLL + u)
        return carry
    lax.fori_loop(0, (hi - lo) // COPY_UNROLL, body, 0)


def _moe_kernel(te_ref, nv_ref, nr_ref, pos_ref, rows_ref, wg_ref, wu_ref, wd_ref, y_ref,
                inv_ref, xg_ref, yb_ref, xb_ref, acc_ref, gsem, ssem, isem, *, tm, nf, nt, n_pairs):
    i = pl.program_id(0)
    f = pl.program_id(1)
    nv = nv_ref[0]
    valid = i < nv
    slot = i % 2
    assert nf == 2

    def gather_row(tile, r, to_slot):
        pltpu.make_async_copy(_row(rows_ref, inv_ref[tile * tm + r] // 2), _row(xg_ref.at[to_slot], r),
                              gsem.at[to_slot]).start()

    def scatter_row(tile, real, r, from_slot):
        pair = jnp.where(r < real, inv_ref[tile * tm + r], n_pairs + from_slot * tm + r)
        pltpu.make_async_copy(_row(yb_ref.at[from_slot], r), _row(y_ref, pair), ssem.at[from_slot]).start()

    def wait_gather(at_slot):
        pltpu.make_async_copy(rows_ref.at[pl.ds(0, tm * ROW_TILES), :], xg_ref.at[at_slot], gsem.at[at_slot]).wait()

    def wait_scatter(at_slot):
        pltpu.make_async_copy(yb_ref.at[at_slot], y_ref.at[pl.ds(0, tm * ROW_TILES), :], ssem.at[at_slot]).wait()

    @pl.when(jnp.logical_and(i == 0, f == 0))
    def _():
        def place(p):
            inv_ref[pos_ref[p]] = p
        _for_rows(0, n_pairs, place)

        def clear_padding(t, carry):
            def clear(r, c):
                inv_ref[t * tm + r] = 0
                return c
            return lax.fori_loop(nr_ref[t], tm, clear, carry)
        lax.fori_loop(0, nv, clear_padding, 0)
        yb_ref[...] = jnp.zeros(yb_ref.shape, F32)
        spare = y_ref.at[pl.ds(n_pairs * ROW_TILES, 2 * tm * ROW_TILES), :]
        for s in range(2):
            half = spare.at[pl.ds(s * tm * ROW_TILES, tm * ROW_TILES), :]
            pltpu.make_async_copy(yb_ref.at[s], half, isem).start()
            pltpu.make_async_copy(yb_ref.at[s], half, isem).wait()
        _for_rows(0, tm, lambda r: gather_row(0, r, 0))

    @pl.when(jnp.logical_and(valid, f == 0))
    def _():
        wait_gather(slot)
        xb_ref[...] = _tiles_to_rows(xg_ref.at[slot], tm).astype(BF16)
        acc_ref[...] = jnp.zeros((tm, D_MODEL), F32)

    def matmuls(first):
        if first:
            nxt = jnp.minimum(i + 1, nv - 1)
            for r in range(tm):
                gather_row(nxt, r, 1 - slot)
        else:
            prev = jnp.maximum(i - 1, 0)
            real = jnp.where(i >= 1, nr_ref[prev], 0)
            for r in range(tm):
                scatter_row(prev, real, r, 1 - slot)
        x = xb_ref[...]
        a = jnp.dot(x, wg_ref[0], preferred_element_type=F32)
        b = jnp.dot(x, wu_ref[0], preferred_element_type=F32)
        acc_ref[...] += _dot(_silu(a) * b, wd_ref[0])

    @pl.when(jnp.logical_and(valid, f == 0))
    def _():
        matmuls(True)

    @pl.when(jnp.logical_and(valid, f > 0))
    def _():
        matmuls(False)

    @pl.when(jnp.logical_and(valid, f == nf - 1))
    def _():
        @pl.when(i >= 1)
        def _():
            wait_scatter(slot)
        _rows_to_tiles(yb_ref.at[slot], acc_ref[...])

        @pl.when(i == nv - 1)
        def _():
            _for_rows(0, tm, lambda r: scatter_row(i, nr_ref[i], r, slot))

    @pl.when(jnp.logical_and(i == nt - 1, f == nf - 1))
    def _():
        wait_gather(nv % 2)
        wait_scatter(nv % 2)
        wait_scatter((nv - 1) % 2)


def _moe(tile_expert, n_valid, tile_rows, pos, rows, wg, wu, wd):
    tm, tf = MOE_TILE, MOE_F_TILE
    nt, nf = tile_expert.shape[0], D_EXPERT // tf
    n_pairs = pos.shape[0]

    def fidx(i, f, nv):
        return jnp.where(i < nv[0], f, nf - 1)
    tile_buf = pltpu.VMEM((2, tm * ROW_TILES, LANES), F32)
    return pl.pallas_call(
        functools.partial(_moe_kernel, tm=tm, nf=nf, nt=nt, n_pairs=n_pairs),
        grid_spec=pltpu.PrefetchScalarGridSpec(
            num_scalar_prefetch=4,
            grid=(nt, nf),
            in_specs=[pl.BlockSpec(memory_space=pl.ANY),
                      pl.BlockSpec((1, D_MODEL, tf), lambda i, f, te, nv, *_: (te[i], 0, fidx(i, f, nv))),
                      pl.BlockSpec((1, D_MODEL, tf), lambda i, f, te, nv, *_: (te[i], 0, fidx(i, f, nv))),
                      pl.BlockSpec((1, tf, D_MODEL), lambda i, f, te, nv, *_: (te[i], fidx(i, f, nv), 0))],
            out_specs=pl.BlockSpec(memory_space=pl.ANY),
            scratch_shapes=[pltpu.SMEM((nt * tm,), jnp.int32), tile_buf, tile_buf,
                            pltpu.VMEM((tm, D_MODEL), BF16), pltpu.VMEM((tm, D_MODEL), F32),
                            pltpu.SemaphoreType.DMA((2,)), pltpu.SemaphoreType.DMA((2,)),
                            pltpu.SemaphoreType.DMA(())],
        ),
        out_shape=jax.ShapeDtypeStruct(((n_pairs + 2 * tm) * ROW_TILES, LANES), F32),
        compiler_params=_params("arbitrary", "arbitrary"),
        name="moe_grouped",
    )(tile_expert, n_valid, tile_rows, pos, rows, wg, wu, wd)


def _combine_kernel(y_ref, h_ref, route_ref, p_ref, g_ref, wpg_ref, wpp_ref, o_ref, *, tb):
    route = route_ref[...]
    w1 = route[:, 4:5]
    w2 = route[:, 5:6]
    pair = 2 * ROW_TILES
    y1 = jnp.concatenate([y_ref[pl.ds(s, tb, stride=pair), :] for s in range(ROW_TILES)], axis=1)
    y2 = jnp.concatenate([y_ref[pl.ds(ROW_TILES + s, tb, stride=pair), :] for s in range(ROW_TILES)], axis=1)
    h5 = h_ref[...] + (w1 * y1 + w2 * y2)
    o_ref[...] = _ple(h5, p_ref[...], g_ref[...], wpg_ref[...], wpp_ref[...])


def _combine(y_pairs, h, route, p, g, wpg, wpp, *, first_token):
    m, d = h.shape
    tb = min(COMBINE_TILE, m)
    weights = (g, wpg, wpp)
    row = lambda w: pl.BlockSpec((tb, w), lambda i: (i, 0))
    first_block = first_token // tb
    p_block = m // tb
    return pl.pallas_call(
        functools.partial(_combine_kernel, tb=tb),
        grid=(m // tb,),
        in_specs=[pl.BlockSpec((tb * 2 * ROW_TILES, LANES), lambda i: (i + first_block, 0)),
                  row(d), row(LANES), pl.BlockSpec((tb, PLE_DIM), lambda i: (i + p_block, 0))]
                 + [_const(w.shape) for w in weights],
        out_specs=row(d),
        out_shape=jax.ShapeDtypeStruct((m, d), F32),
        compiler_params=_params("arbitrary"),
        name="moe_combine",
    )(y_pairs, h, route, p, *weights)


def _row2(v):
    return v.reshape(1, -1).astype(F32)


def kernel(x_prompt, x_sample, state_conv, cache_k, cache_v, p_prompt, p_sample, norm_mix, norm_ffn, norm_ple,
           conv_w_pw1, conv_b_pw1, conv_w_dw, conv_b_dw, conv_ln_g, conv_ln_b, conv_w_pw2, conv_b_pw2,
           norm_kv, w_k, w_v, k_norm, w_q, q_norm, sinks, w_o, ffn_w_gate, ffn_w_up, ffn_w_down,
           moe_w_router, moe_b_router, moe_w_gate, moe_w_up, moe_w_down, ple_w_gate, ple_w_proj):
    nb, t, d = x_prompt.shape
    ns = x_sample.shape[0]
    mp = nb * t
    bf = lambda w: w.astype(BF16)

    conv_w = (_row2(norm_mix[0]), bf(conv_w_pw1[0]), _row2(conv_b_pw1[0]), conv_w_dw[0], _row2(conv_b_dw[0]),
              _row2(conv_ln_g[0]), _row2(conv_ln_b[0]), bf(conv_w_pw2[0]), _row2(conv_b_pw2[0]))
    ffn_w = (_row2(norm_ffn[0]), bf(ffn_w_gate[0]), bf(ffn_w_up[0]), bf(ffn_w_down[0]))
    wq = w_q[0].reshape(d, N_KV_HEADS, GROUP, HEAD_DIM).transpose(0, 2, 1, 3).reshape(d, d)
    wo = w_o[0].reshape(N_KV_HEADS, GROUP, HEAD_DIM, d).transpose(1, 0, 2, 3).reshape(d, d)
    sinks_gk = sinks[0].astype(F32).reshape(N_KV_HEADS, GROUP).T.reshape(N_HEADS)
    kvq_w = (_row2(norm_ple[0]), bf(ple_w_gate[0]), bf(ple_w_proj[0]), _row2(norm_kv), bf(w_k), bf(w_v),
             _row2(jnp.tile(k_norm, N_KV_HEADS)), _row2(norm_mix[1]), bf(wq), _row2(jnp.tile(q_norm[0], N_HEADS)))
    wr = jnp.zeros((d, LANES), F32).at[:, :N_EXPERTS].set(moe_w_router[0])
    wr_hi = bf(wr)
    wr_lo = bf(wr - wr_hi.astype(F32))
    br = jnp.full((1, LANES), NEG_BIG, F32).at[0, :N_EXPERTS].set(moe_b_router[0])
    router_w = (bf(wo), _row2(norm_ffn[1]), jnp.concatenate([wr_hi, wr_lo], axis=1), br)
    ple1_w = (_row2(norm_ple[1]), bf(ple_w_gate[1]), bf(ple_w_proj[1]))

    h1p, conv_p = _conv_prompt(x_prompt, *conv_w)
    pp = p_prompt.reshape(-1, PLE_DIM)
    ps = p_sample.reshape(-1, PLE_DIM)
    h3p, kp, vp, qp = _ple_kvq(_ffn(h1p.reshape(mp, d), *ffn_w), pp, *kvq_w)
    h1s, conv_s = _conv_sample(x_sample.reshape(ns, d), state_conv[0], *conv_w)
    h3s, ks, vs, qs = _ple_kvq(_ffn(h1s, *ffn_w), ps, *kvq_w)

    kp3 = kp.reshape(nb, t, KV_DIM)
    vp3 = vp.reshape(nb, t, KV_DIM)
    op = _attn_prompt(qp.reshape(nb, t, d), kp3, vp3, sinks_gk)
    os_, k_state_s, v_state_s = _attn_sample(qs, cache_k.reshape(ns, WINDOW, KV_DIM),
                                             cache_v.reshape(ns, WINDOW, KV_DIM), ks, vs,
                                             sinks_gk.reshape(N_HEADS, 1))

    mt = mp + ns
    rows = jnp.zeros((mt * ROW_TILES, LANES), F32)
    h4p, rows, route_p, cnt_p = _oproj_router(rows, op.reshape(mp, d), h3p, *router_w, first_token=0)
    h4s, rows, route_s, cnt_s = _oproj_router(rows, os_, h3s, *router_w, first_token=mp)

    cnt_p = cnt_p[0, :N_EXPERTS].astype(jnp.int32)
    cnt_s = cnt_s[0, :N_EXPERTS].astype(jnp.int32)
    group = (cnt_p + cnt_s + MOE_TILE - 1) // MOE_TILE * MOE_TILE
    ends = jnp.cumsum(group)
    starts = ends - group
    n_tiles = (2 * (mp + ns) + N_EXPERTS * (MOE_TILE - 1)) // MOE_TILE
    n_valid = (ends[-1] // MOE_TILE).astype(jnp.int32)
    tile_start = jnp.minimum(jnp.arange(n_tiles, dtype=jnp.int32), n_valid - 1) * MOE_TILE
    tile_expert = jnp.sum(tile_start[:, None] >= ends[None, :], axis=1).astype(jnp.int32)

    def positions(route, base):
        idx = route[:, 0:2].astype(jnp.int32)
        rank = route[:, 2:4].astype(jnp.int32)
        return (starts[idx] + base[idx] + rank).reshape(-1)
    pos = jnp.concatenate([positions(route_p, jnp.zeros_like(cnt_p)), positions(route_s, cnt_p)])
    tile_ids = jnp.arange(n_tiles, dtype=jnp.int32)
    rows_left = (starts + cnt_p + cnt_s)[tile_expert] - tile_ids * MOE_TILE
    tile_rows = jnp.where(tile_ids < n_valid, jnp.clip(rows_left, 0, MOE_TILE), 0).astype(jnp.int32)
    y_pairs = _moe(tile_expert, n_valid.reshape(1), tile_rows, pos.astype(jnp.int32), rows,
                   bf(moe_w_gate[0]), bf(moe_w_up[0]), bf(moe_w_down[0]))

    y_p = _combine(y_pairs, h4p, route_p, pp, *ple1_w, first_token=0)
    y_s = _combine(y_pairs, h4s, route_s, ps, *ple1_w, first_token=mp)

    kv4 = lambda a, n: a.reshape(n, WINDOW, N_KV_HEADS, HEAD_DIM)
    return (y_p.reshape(nb, t, d), y_s.reshape(ns, 1, d), conv_p[None],
            kv4(kp3[:, t - WINDOW:], nb), kv4(vp3[:, t - WINDOW:], nb),
            conv_s[None], kv4(k_state_s, ns), kv4(v_state_s, ns))
```

```python
import functools

import jax
import jax.numpy as jnp
from jax import lax
from jax.experimental import pallas as pl
from jax.experimental.pallas import tpu as pltpu

D_MODEL = 1024
CONV_WIDTH = 31
CONV_CTX = CONV_WIDTH - 1
HEAD_DIM = 64
N_HEADS = 16
N_KV_HEADS = 4
GROUP = N_HEADS // N_KV_HEADS
KV_DIM = N_KV_HEADS * HEAD_DIM
WINDOW = 128
D_FF = 2816
N_EXPERTS = 8
D_EXPERT = 3584
PLE_DIM = 256
EPS = 1e-6

BF16 = jnp.bfloat16
F32 = jnp.float32

LANES = 128
SUBLANES = 8
ROW_TILES = D_MODEL // LANES
VMEM_LIMIT = 56 * 1024 * 1024

CONV_PAD = 32
CONV_ROWS = 64
TOK_TILE = 512
MOE_TILE = 512
MOE_F_TILE = 1792
COMBINE_TILE = 512
COPY_UNROLL = 8
NEG_BIG = -1e30


def _params(*sem):
    return pltpu.CompilerParams(dimension_semantics=sem, vmem_limit_bytes=VMEM_LIMIT)


def _const(shape):
    nd = len(shape)
    return pl.BlockSpec(shape, lambda *_: (0,) * nd, pipeline_mode=pl.Buffered(1))


def _rms(x, g):
    return x * lax.rsqrt(jnp.mean(x * x, axis=-1, keepdims=True) + EPS) * g


def _dot(a, w):
    return jnp.dot(a.astype(BF16), w, preferred_element_type=F32)


def _sigmoid(x):
    return 1.0 / (1.0 + jnp.exp(-x))


def _silu(x):
    return x * _sigmoid(x)


def _layer_norm(x, g, b):
    mu = jnp.mean(x, axis=-1, keepdims=True)
    xc = x - mu
    var = jnp.mean(xc * xc, axis=-1, keepdims=True)
    return xc * lax.rsqrt(var + EPS) * g + b


def _head_rms(x, g):
    t, w = x.shape
    lane = lax.broadcasted_iota(jnp.int32, (t, LANES), 1)
    lo = lane < HEAD_DIM
    outs = []
    for c in range(w // LANES):
        blk = x[:, c * LANES:(c + 1) * LANES]
        sq = blk * blk
        s_lo = jnp.sum(jnp.where(lo, sq, 0.0), axis=-1, keepdims=True)
        s_hi = jnp.sum(jnp.where(lo, 0.0, sq), axis=-1, keepdims=True)
        inv = jnp.where(lo, lax.rsqrt(s_lo / HEAD_DIM + EPS), lax.rsqrt(s_hi / HEAD_DIM + EPS))
        outs.append(blk * inv)
    return jnp.concatenate(outs, axis=1) * g


def _rows_to_tiles(ref, x):
    t = x.shape[0]
    for s in range(ROW_TILES):
        ref[pl.ds(s, t, stride=ROW_TILES), :] = x[:, s * LANES:(s + 1) * LANES]


def _tiles_to_rows(ref, t):
    return jnp.concatenate([ref[pl.ds(s, t, stride=ROW_TILES), :] for s in range(ROW_TILES)], axis=1)


def _conv_taps(uext_ref, wdw_ref, bdw_ref, c_ref, tt):
    win_rows = CONV_ROWS + CONV_PAD

    def chunk(ci, carry):
        r0 = pl.multiple_of(ci * CONV_ROWS, CONV_ROWS)
        for lt in range(ROW_TILES):
            cols = slice(lt * LANES, (lt + 1) * LANES)
            win = uext_ref[pl.ds(r0, win_rows), cols]
            acc = jnp.broadcast_to(bdw_ref[:, cols], (CONV_ROWS, LANES))
            for b in range(SUBLANES):
                wb = win if b == 0 else pltpu.roll(win, win_rows - b, axis=0)
                for a in range(CONV_PAD // SUBLANES + 1):
                    j = a * SUBLANES + b - (CONV_PAD - CONV_CTX)
                    if 0 <= j < CONV_WIDTH:
                        acc = acc + wdw_ref[j:j + 1, cols] * wb[a * SUBLANES:a * SUBLANES + CONV_ROWS]
            c_ref[pl.ds(r0, CONV_ROWS), cols] = acc
        return carry
    lax.fori_loop(0, tt // CONV_ROWS, chunk, 0)


def _conv_prompt_kernel(x_ref, nm_ref, w1_ref, b1_ref, wdw_ref, bdw_ref, lng_ref, lnb_ref, w2_ref, b2_ref,
                        h_ref, st_ref, uext_ref, c_ref, *, tt):
    t = pl.program_id(1)

    @pl.when(t == 0)
    def _():
        uext_ref[0:CONV_PAD, :] = jnp.zeros((CONV_PAD, D_MODEL), F32)

    @pl.when(t > 0)
    def _():
        uext_ref[0:CONV_PAD, :] = uext_ref[tt:tt + CONV_PAD, :]

    x = x_ref[0]
    z = _dot(_rms(x, nm_ref[...]), w1_ref[...]) + b1_ref[...]
    u = z[:, :D_MODEL] * _sigmoid(z[:, D_MODEL:])
    uext_ref[CONV_PAD:CONV_PAD + tt, :] = u
    _conv_taps(uext_ref, wdw_ref, bdw_ref, c_ref, tt)
    c = _silu(_layer_norm(c_ref[...], lng_ref[...], lnb_ref[...]))
    h_ref[0] = x + _dot(c, w2_ref[...]) + b2_ref[...]
    st_ref[0, 0] = uext_ref[CONV_PAD + tt - CONV_CTX:CONV_PAD + tt, :]


def _conv_prompt(x, nm, w1, b1, wdw, bdw, lng, lnb, w2, b2):
    n, t, d = x.shape
    tt = TOK_TILE
    weights = (nm, w1, b1, wdw, bdw, lng, lnb, w2, b2)
    return pl.pallas_call(
        functools.partial(_conv_prompt_kernel, tt=tt),
        grid=(n, t // tt),
        in_specs=[pl.BlockSpec((1, tt, d), lambda b, i: (b, i, 0))] + [_const(w.shape) for w in weights],
        out_specs=[pl.BlockSpec((1, tt, d), lambda b, i: (b, i, 0)),
                   pl.BlockSpec((1, 1, CONV_CTX, d), lambda b, i: (0, b, 0, 0))],
        out_shape=[jax.ShapeDtypeStruct((n, t, d), F32), jax.ShapeDtypeStruct((1, n, CONV_CTX, d), F32)],
        scratch_shapes=[pltpu.VMEM((CONV_PAD + tt, d), F32), pltpu.VMEM((tt, d), F32)],
        compiler_params=_params("arbitrary", "arbitrary"),
        name="conv_prompt",
    )(x, *weights)


def _conv_sample_kernel(x_ref, past_ref, nm_ref, w1_ref, b1_ref, wdw_ref, bdw_ref, lng_ref, lnb_ref,
                        w2_ref, b2_ref, h_ref, st_ref, c_ref, *, nb):
    x = x_ref[...]
    z = _dot(_rms(x, nm_ref[...]), w1_ref[...]) + b1_ref[...]
    u = z[:, :D_MODEL] * _sigmoid(z[:, D_MODEL:])
    w_past = wdw_ref[0:CONV_CTX, :]
    for i in range(nb):
        past = past_ref[0, i]
        c_ref[i:i + 1, :] = jnp.sum(past * w_past, axis=0, keepdims=True)
        st_ref[0, i, 0:CONV_CTX - 1, :] = past[1:CONV_CTX, :]
        st_ref[0, i, CONV_CTX - 1:CONV_CTX, :] = u[i:i + 1, :]
    c = c_ref[...] + wdw_ref[CONV_CTX:CONV_WIDTH, :] * u + bdw_ref[...]
    c = _silu(_layer_norm(c, lng_ref[...], lnb_ref[...]))
    h_ref[...] = x + _dot(c, w2_ref[...]) + b2_ref[...]


def _conv_sample(x, past, nm, w1, b1, wdw, bdw, lng, lnb, w2, b2):
    n, d = x.shape
    nb = 16
    weights = (nm, w1, b1, wdw, bdw, lng, lnb, w2, b2)
    state = pl.BlockSpec((1, nb, CONV_CTX, d), lambda i: (0, i, 0, 0))
    return pl.pallas_call(
        functools.partial(_conv_sample_kernel, nb=nb),
        grid=(n // nb,),
        in_specs=[pl.BlockSpec((nb, d), lambda i: (i, 0)), state] + [_const(w.shape) for w in weights],
        out_specs=[pl.BlockSpec((nb, d), lambda i: (i, 0)), state],
        out_shape=[jax.ShapeDtypeStruct((n, d), F32), jax.ShapeDtypeStruct((1, n, CONV_CTX, d), F32)],
        scratch_shapes=[pltpu.VMEM((nb, d), F32)],
        compiler_params=_params("arbitrary"),
        name="conv_sample",
    )(x, past, *weights)


def _ffn_kernel(h_ref, g_ref, wg_ref, wu_ref, wd_ref, o_ref):
    h = h_ref[...]
    hn = _rms(h, g_ref[...]).astype(BF16)
    a = jnp.dot(hn, wg_ref[...], preferred_element_type=F32)
    b = jnp.dot(hn, wu_ref[...], preferred_element_type=F32)
    o_ref[...] = h + _dot(_silu(a) * b, wd_ref[...])


def _ffn(h, g, wg, wu, wd):
    m, d = h.shape
    tm = min(TOK_TILE, m)
    weights = (g, wg, wu, wd)
    return pl.pallas_call(
        _ffn_kernel,
        grid=(m // tm,),
        in_specs=[pl.BlockSpec((tm, d), lambda i: (i, 0))] + [_const(w.shape) for w in weights],
        out_specs=pl.BlockSpec((tm, d), lambda i: (i, 0)),
        out_shape=jax.ShapeDtypeStruct((m, d), F32),
        compiler_params=_params("arbitrary"),
        name="dense_ffn",
    )(h, *weights)


def _ple(h, p, g, w_gate, w_proj):
    return h + _sigmoid(_dot(_rms(h, g), w_gate)) * _dot(p, w_proj)


def _ple_kvq_kernel(h_ref, p_ref, gple_ref, wpg_ref, wpp_ref, gkv_ref, wk_ref, wv_ref, kn_ref,
                    gmix_ref, wq_ref, qn_ref, h3_ref, k_ref, v_ref, q_ref):
    h3 = _ple(h_ref[...], p_ref[...], gple_ref[...], wpg_ref[...], wpp_ref[...])
    h3_ref[...] = h3
    s = _rms(h3, gkv_ref[...]).astype(BF16)
    k_ref[...] = _head_rms(jnp.dot(s, wk_ref[...], preferred_element_type=F32), kn_ref[...])
    v_ref[...] = jnp.dot(s, wv_ref[...], preferred_element_type=F32)
    q = _dot(_rms(h3, gmix_ref[...]), wq_ref[...])
    q_ref[...] = (_head_rms(q, qn_ref[...]) * (HEAD_DIM ** -0.5)).astype(BF16)


def _ple_kvq(h, p, gple, wpg, wpp, gkv, wk, wv, kn, gmix, wq, qn):
    m, d = h.shape
    tm = min(TOK_TILE, m)
    weights = (gple, wpg, wpp, gkv, wk, wv, kn, gmix, wq, qn)
    row = lambda w: pl.BlockSpec((tm, w), lambda i: (i, 0))
    return pl.pallas_call(
        _ple_kvq_kernel,
        grid=(m // tm,),
        in_specs=[row(d), row(PLE_DIM)] + [_const(w.shape) for w in weights],
        out_specs=[row(d), row(KV_DIM), row(KV_DIM), row(d)],
        out_shape=[jax.ShapeDtypeStruct((m, d), F32), jax.ShapeDtypeStruct((m, KV_DIM), F32),
                   jax.ShapeDtypeStruct((m, KV_DIM), F32), jax.ShapeDtypeStruct((m, d), BF16)],
        compiler_params=_params("arbitrary"),
        name="ple_kv_q",
    )(h, p, *weights)


def _softmax_sink(s, sink):
    m = jnp.maximum(jnp.max(s, axis=-1, keepdims=True), sink)
    e = jnp.exp(s - m)
    den = jnp.sum(e, axis=-1, keepdims=True) + jnp.exp(sink - m)
    return e / den


def _attn_prompt_kernel(q_ref, kp_ref, kc_ref, vp_ref, vc_ref, sink_ref, o_ref):
    n = pl.program_id(1)
    q = q_ref[0]
    k = jnp.concatenate([kp_ref[0], kc_ref[0]], axis=0).astype(BF16)
    v = jnp.concatenate([vp_ref[0], vc_ref[0]], axis=0).astype(BF16)
    qi = lax.broadcasted_iota(jnp.int32, (WINDOW, 2 * WINDOW), 0)
    kj = lax.broadcasted_iota(jnp.int32, (WINDOW, 2 * WINDOW), 1)
    dist = qi + WINDOW - kj
    first_key = jnp.where(n > 0, 0, WINDOW)
    mask = (dist >= 0) & (dist <= WINDOW) & (kj >= first_key)
    outs = []
    for h in range(N_HEADS):
        kh = h % N_KV_HEADS
        qh = q[:, h * HEAD_DIM:(h + 1) * HEAD_DIM]
        kk = k[:, kh * HEAD_DIM:(kh + 1) * HEAD_DIM]
        vv = v[:, kh * HEAD_DIM:(kh + 1) * HEAD_DIM]
        s = lax.dot_general(qh, kk, (((1,), (1,)), ((), ())), preferred_element_type=F32)
        s = jnp.where(mask, s, -jnp.inf)
        p = _softmax_sink(s, sink_ref[h])
        outs.append(jnp.dot(p.astype(BF16), vv, preferred_element_type=F32))
    o_ref[0] = jnp.concatenate(outs, axis=1).astype(BF16)


def _attn_prompt(q, k, v, sinks):
    n, t, d = q.shape
    nb = t // WINDOW
    cur = lambda w: pl.BlockSpec((1, WINDOW, w), lambda b, i: (b, i, 0))
    prev = lambda w: pl.BlockSpec((1, WINDOW, w), lambda b, i: (b, jnp.maximum(i - 1, 0), 0))
    return pl.pallas_call(
        _attn_prompt_kernel,
        grid=(n, nb),
        in_specs=[cur(d), prev(KV_DIM), cur(KV_DIM), prev(KV_DIM), cur(KV_DIM),
                  pl.BlockSpec(memory_space=pltpu.SMEM)],
        out_specs=cur(d),
        out_shape=jax.ShapeDtypeStruct((n, t, d), BF16),
        compiler_params=_params("arbitrary", "arbitrary"),
        name="attn_prompt",
    )(q, k, k, v, v, sinks)


def _attn_sample_kernel(q_ref, kc_ref, vc_ref, kn_ref, vn_ref, sink_ref, o_ref, ks_ref, vs_ref, *, nb):
    lane_head = lax.broadcasted_iota(jnp.int32, (N_KV_HEADS, KV_DIM), 1) // HEAD_DIM
    own = lane_head == lax.broadcasted_iota(jnp.int32, (N_KV_HEADS, KV_DIM), 0)
    sink = sink_ref[...]
    rows = []
    for i in range(nb):
        kc = kc_ref[i]
        vc = vc_ref[i]
        kn = kn_ref[i:i + 1, :]
        vn = vn_ref[i:i + 1, :]
        ks_ref[i, 0:WINDOW - 1, :] = kc[1:WINDOW, :]
        ks_ref[i, WINDOW - 1:WINDOW, :] = kn
        vs_ref[i, 0:WINDOW - 1, :] = vc[1:WINDOW, :]
        vs_ref[i, WINDOW - 1:WINDOW, :] = vn
        qrow = q_ref[i:i + 1, :].astype(F32)
        qx = jnp.concatenate(
            [jnp.where(own, jnp.broadcast_to(qrow[:, g * KV_DIM:(g + 1) * KV_DIM], (N_KV_HEADS, KV_DIM)), 0.0)
             for g in range(GROUP)], axis=0)
        s_c = lax.dot_general(qx.astype(BF16), kc.astype(BF16), (((1,), (1,)), ((), ())),
                              preferred_element_type=F32)
        s_n = jnp.sum(qx * kn.astype(BF16).astype(F32), axis=-1, keepdims=True)
        m = jnp.maximum(jnp.maximum(jnp.max(s_c, axis=-1, keepdims=True), s_n), sink)
        e_c = jnp.exp(s_c - m)
        e_n = jnp.exp(s_n - m)
        den = jnp.sum(e_c, axis=-1, keepdims=True) + e_n + jnp.exp(sink - m)
        ox = jnp.dot((e_c / den).astype(BF16), vc.astype(BF16), preferred_element_type=F32)
        ox = ox + (e_n / den).astype(BF16).astype(F32) * vn.astype(BF16).astype(F32)
        rows.append(jnp.concatenate(
            [jnp.sum(jnp.where(own, ox[g * N_KV_HEADS:(g + 1) * N_KV_HEADS], 0.0), axis=0, keepdims=True)
             for g in range(GROUP)], axis=1))
    o_ref[...] = jnp.concatenate(rows, axis=0).astype(BF16)


def _attn_sample(q, kc, vc, kn, vn, sinks_col):
    n, d = q.shape
    nb = 16
    row = lambda w: pl.BlockSpec((nb, w), lambda i: (i, 0))
    cache = pl.BlockSpec((nb, WINDOW, KV_DIM), lambda i: (i, 0, 0))
    return pl.pallas_call(
        functools.partial(_attn_sample_kernel, nb=nb),
        grid=(n // nb,),
        in_specs=[row(d), cache, cache, row(KV_DIM), row(KV_DIM), _const(sinks_col.shape)],
        out_specs=[row(d), cache, cache],
        out_shape=[jax.ShapeDtypeStruct((n, d), BF16),
                   jax.ShapeDtypeStruct((n, WINDOW, KV_DIM), F32),
                   jax.ShapeDtypeStruct((n, WINDOW, KV_DIM), F32)],
        compiler_params=_params("arbitrary"),
        name="attn_sample",
    )(q, kc, vc, kn, vn, sinks_col)


def _oproj_router_kernel(rows_in_ref, o_ref, h_ref, wo_ref, g_ref, wr_ref, br_ref,
                         h4_ref, hn_ref, route_ref, sort_ref, cnt_ref, carry_ref, *, tm):
    del rows_in_ref
    i = pl.program_id(0)

    @pl.when(i == 0)
    def _():
        carry_ref[...] = jnp.zeros((1, LANES), F32)

    h4 = h_ref[...] + jnp.dot(o_ref[...], wo_ref[...], preferred_element_type=F32)
    h4_ref[...] = h4
    hn = _rms(h4, g_ref[...])
    _rows_to_tiles(hn_ref, hn)
    hn_hi = hn.astype(BF16)
    hn_lo = (hn - hn_hi.astype(F32)).astype(BF16)
    part_hi = jnp.dot(hn_hi, wr_ref[...], preferred_element_type=F32)
    part_lo = jnp.dot(hn_lo, wr_ref[...], preferred_element_type=F32)
    logits = (part_hi[:, :LANES] + (part_hi[:, LANES:] + part_lo[:, :LANES]) + part_lo[:, LANES:]) + br_ref[...]
    lane = lax.broadcasted_iota(jnp.int32, (tm, LANES), 1).astype(F32)
    m1 = jnp.max(logits, axis=-1, keepdims=True)
    i1 = jnp.min(jnp.where(logits == m1, lane, float(LANES)), axis=-1, keepdims=True)
    rest = jnp.where(lane == i1, -jnp.inf, logits)
    m2 = jnp.max(rest, axis=-1, keepdims=True)
    i2 = jnp.min(jnp.where(rest == m2, lane, float(LANES)), axis=-1, keepdims=True)
    e2 = jnp.exp(m2 - m1)
    w1 = 1.0 / (1.0 + e2)
    w2 = e2 / (1.0 + e2)
    oh1 = lane == i1
    oh2 = lane == i2
    sel = jnp.where(oh1, 1.0, 0.0) + jnp.where(oh2, 1.0, 0.0)
    row = lax.broadcasted_iota(jnp.int32, (tm, tm), 0)
    col = lax.broadcasted_iota(jnp.int32, (tm, tm), 1)
    tri = jnp.where(row > col, 1.0, 0.0).astype(BF16)
    before = jnp.dot(tri, sel.astype(BF16), preferred_element_type=F32) + carry_ref[...]
    r1 = jnp.sum(jnp.where(oh1, before, 0.0), axis=-1, keepdims=True)
    r2 = jnp.sum(jnp.where(oh2, before, 0.0), axis=-1, keepdims=True)
    carry_ref[...] = carry_ref[...] + jnp.sum(sel, axis=0, keepdims=True)
    cnt_ref[...] = carry_ref[...]
    fields = (i1, i2, r1, r2, w1, w2)
    route = jnp.zeros((tm, LANES), F32)
    for f, val in enumerate(fields):
        route = jnp.where(lane == float(f), val, route)
    route_ref[...] = route
    sort_ref[...] = route.T[0:SUBLANES, :]


def _oproj_router(rows_buf, o, h, wo, g, wr, br, *, first_token):
    m, d = h.shape
    tm = min(TOK_TILE, m)
    weights = (wo, g, wr, br)
    row = lambda w: pl.BlockSpec((tm, w), lambda i: (i, 0))
    first_block = first_token // tm
    return pl.pallas_call(
        functools.partial(_oproj_router_kernel, tm=tm),
        grid=(m // tm,),
        in_specs=[pl.BlockSpec(memory_space=pl.ANY), row(d), row(d)] + [_const(w.shape) for w in weights],
        out_specs=[row(d), pl.BlockSpec((tm * ROW_TILES, LANES), lambda i: (i + first_block, 0)), row(LANES),
                   pl.BlockSpec((SUBLANES, tm), lambda i: (0, i)), pl.BlockSpec((1, LANES), lambda i: (0, 0))],
        out_shape=[jax.ShapeDtypeStruct((m, d), F32), jax.ShapeDtypeStruct(rows_buf.shape, F32),
                   jax.ShapeDtypeStruct((m, LANES), F32), jax.ShapeDtypeStruct((SUBLANES, m), F32),
                   jax.ShapeDtypeStruct((1, LANES), F32)],
        scratch_shapes=[pltpu.VMEM((1, LANES), F32)],
        input_output_aliases={0: 1},
        compiler_params=_params("arbitrary"),
        name="oproj_router",
    )(rows_buf, o, h, *weights)


def _row(ref, r):
    return ref.at[pl.ds(pl.multiple_of(r * ROW_TILES, ROW_TILES), ROW_TILES), :]


def _for_rows(lo, hi, fn):
    assert (hi - lo) % COPY_UNROLL == 0

    def body(j, carry):
        for u in range(COPY_UNROLL):
            fn(lo + j * COPY_UNROLL + u)
        return carry
    lax.fori_loop(0, (hi - lo) // COPY_UNROLL, body, 0)


def _moe_kernel(te_ref, nv_ref, nr_ref, pos_ref, rows_ref, wg_ref, wu_ref, wd_ref, y_ref,
                inv_ref, xg_ref, yb_ref, xb_ref, acc_ref, gsem, ssem, isem, *, tm, nf, nt, n_pairs, pair_regions):
    i = pl.program_id(0)
    f = pl.program_id(1)
    nv = nv_ref[0]
    valid = i < nv
    slot = i % 2
    assert nf == 2

    def gather_row(tile, r, to_slot):
        pair = inv_ref[tile * tm + r]
        token = pair
        for first_pair, _, first_token in pair_regions[1:]:
            token = jnp.where(pair >= first_pair, pair - (first_pair - first_token), token)
        pltpu.make_async_copy(_row(rows_ref, token), _row(xg_ref.at[to_slot], r), gsem.at[to_slot]).start()

    def scatter_row(tile, real, r, from_slot):
        pair = jnp.where(r < real, inv_ref[tile * tm + r], n_pairs + from_slot * tm + r)
        pltpu.make_async_copy(_row(yb_ref.at[from_slot], r), _row(y_ref, pair), ssem.at[from_slot]).start()

    def wait_gather(at_slot):
        pltpu.make_async_copy(rows_ref.at[pl.ds(0, tm * ROW_TILES), :], xg_ref.at[at_slot], gsem.at[at_slot]).wait()

    def wait_scatter(at_slot):
        pltpu.make_async_copy(yb_ref.at[at_slot], y_ref.at[pl.ds(0, tm * ROW_TILES), :], ssem.at[at_slot]).wait()

    @pl.when(jnp.logical_and(i == 0, f == 0))
    def _():
        def place(p):
            inv_ref[pos_ref[p]] = p
        _for_rows(0, n_pairs, place)

        def clear_padding(t, carry):
            def clear(r, c):
                inv_ref[t * tm + r] = 0
                return c
            return lax.fori_loop(nr_ref[t], tm, clear, carry)
        lax.fori_loop(0, nv, clear_padding, 0)
        yb_ref[...] = jnp.zeros(yb_ref.shape, F32)
        spare = y_ref.at[pl.ds(n_pairs * ROW_TILES, 2 * tm * ROW_TILES), :]
        for s in range(2):
            half = spare.at[pl.ds(s * tm * ROW_TILES, tm * ROW_TILES), :]
            pltpu.make_async_copy(yb_ref.at[s], half, isem).start()
            pltpu.make_async_copy(yb_ref.at[s], half, isem).wait()
        _for_rows(0, tm, lambda r: gather_row(0, r, 0))

    @pl.when(jnp.logical_and(valid, f == 0))
    def _():
        wait_gather(slot)
        xb_ref[...] = _tiles_to_rows(xg_ref.at[slot], tm).astype(BF16)
        acc_ref[...] = jnp.zeros((tm, D_MODEL), F32)

    def matmuls(first):
        if first:
            nxt = jnp.minimum(i + 1, nv - 1)
            for r in range(tm):
                gather_row(nxt, r, 1 - slot)
        else:
            prev = jnp.maximum(i - 1, 0)
            real = jnp.where(i >= 1, nr_ref[prev], 0)
            for r in range(tm):
                scatter_row(prev, real, r, 1 - slot)
        x = xb_ref[...]
        a = jnp.dot(x, wg_ref[0], preferred_element_type=F32)
        b = jnp.dot(x, wu_ref[0], preferred_element_type=F32)
        acc_ref[...] += _dot(_silu(a) * b, wd_ref[0])

    @pl.when(jnp.logical_and(valid, f == 0))
    def _():
        matmuls(True)

    @pl.when(jnp.logical_and(valid, f > 0))
    def _():
        matmuls(False)

    @pl.when(jnp.logical_and(valid, f == nf - 1))
    def _():
        @pl.when(i >= 1)
        def _():
            wait_scatter(slot)
        _rows_to_tiles(yb_ref.at[slot], acc_ref[...])

        @pl.when(i == nv - 1)
        def _():
            _for_rows(0, tm, lambda r: scatter_row(i, nr_ref[i], r, slot))

    @pl.when(jnp.logical_and(i == nt - 1, f == nf - 1))
    def _():
        wait_gather(nv % 2)
        wait_scatter(nv % 2)
        wait_scatter((nv - 1) % 2)


def _moe(tile_expert, n_valid, tile_rows, pos, pair_regions, rows, wg, wu, wd):
    tm, tf = MOE_TILE, MOE_F_TILE
    nt, nf = tile_expert.shape[0], D_EXPERT // tf
    n_pairs = pos.shape[0]

    def fidx(i, f, nv):
        return jnp.where(i < nv[0], f, nf - 1)
    tile_buf = pltpu.VMEM((2, tm * ROW_TILES, LANES), F32)
    return pl.pallas_call(
        functools.partial(_moe_kernel, tm=tm, nf=nf, nt=nt, n_pairs=n_pairs, pair_regions=pair_regions),
        grid_spec=pltpu.PrefetchScalarGridSpec(
            num_scalar_prefetch=4,
            grid=(nt, nf),
            in_specs=[pl.BlockSpec(memory_space=pl.ANY),
                      pl.BlockSpec((1, D_MODEL, tf), lambda i, f, te, nv, *_: (te[i], 0, fidx(i, f, nv))),
                      pl.BlockSpec((1, D_MODEL, tf), lambda i, f, te, nv, *_: (te[i], 0, fidx(i, f, nv))),
                      pl.BlockSpec((1, tf, D_MODEL), lambda i, f, te, nv, *_: (te[i], fidx(i, f, nv), 0))],
            out_specs=pl.BlockSpec(memory_space=pl.ANY),
            scratch_shapes=[pltpu.SMEM((nt * tm,), jnp.int32), tile_buf, tile_buf,
                            pltpu.VMEM((tm, D_MODEL), BF16), pltpu.VMEM((tm, D_MODEL), F32),
                            pltpu.SemaphoreType.DMA((2,)), pltpu.SemaphoreType.DMA((2,)),
                            pltpu.SemaphoreType.DMA(())],
        ),
        out_shape=jax.ShapeDtypeStruct(((n_pairs + 2 * tm) * ROW_TILES, LANES), F32),
        compiler_params=_params("arbitrary", "arbitrary"),
        name="moe_grouped",
    )(tile_expert, n_valid, tile_rows, pos, rows, wg, wu, wd)


def _combine_kernel(y1_ref, y2_ref, h_ref, route_ref, p_ref, g_ref, wpg_ref, wpp_ref, o_ref, *, tb):
    route = route_ref[...]
    w1 = route[:, 4:5]
    w2 = route[:, 5:6]
    h5 = h_ref[...] + (w1 * _tiles_to_rows(y1_ref, tb) + w2 * _tiles_to_rows(y2_ref, tb))
    o_ref[...] = _ple(h5, p_ref[...], g_ref[...], wpg_ref[...], wpp_ref[...])


def _combine(y_pairs, h, route, p, g, wpg, wpp, *, first_pair):
    m, d = h.shape
    tb = min(COMBINE_TILE, m)
    weights = (g, wpg, wpp)
    row = lambda w: pl.BlockSpec((tb, w), lambda i: (i, 0))
    steps = m // tb
    first_block = first_pair // tb
    pair_rows = lambda slot: pl.BlockSpec((tb * ROW_TILES, LANES), lambda i: (i + first_block + slot * steps, 0))
    return pl.pallas_call(
        functools.partial(_combine_kernel, tb=tb),
        grid=(steps,),
        in_specs=[pair_rows(0), pair_rows(1), row(d), row(LANES),
                  pl.BlockSpec((tb, PLE_DIM), lambda i: (i + steps, 0))] + [_const(w.shape) for w in weights],
        out_specs=row(d),
        out_shape=jax.ShapeDtypeStruct((m, d), F32),
        compiler_params=_params("arbitrary"),
        name="moe_combine",
    )(y_pairs, y_pairs, h, route, p, *weights)


def _row2(v):
    return v.reshape(1, -1).astype(F32)


def kernel(x_prompt, x_sample, state_conv, cache_k, cache_v, p_prompt, p_sample, norm_mix, norm_ffn, norm_ple,
           conv_w_pw1, conv_b_pw1, conv_w_dw, conv_b_dw, conv_ln_g, conv_ln_b, conv_w_pw2, conv_b_pw2,
           norm_kv, w_k, w_v, k_norm, w_q, q_norm, sinks, w_o, ffn_w_gate, ffn_w_up, ffn_w_down,
           moe_w_router, moe_b_router, moe_w_gate, moe_w_up, moe_w_down, ple_w_gate, ple_w_proj):
    nb, t, d = x_prompt.shape
    ns = x_sample.shape[0]
    mp = nb * t
    bf = lambda w: w.astype(BF16)

    conv_w = (_row2(norm_mix[0]), bf(conv_w_pw1[0]), _row2(conv_b_pw1[0]), conv_w_dw[0], _row2(conv_b_dw[0]),
              _row2(conv_ln_g[0]), _row2(conv_ln_b[0]), bf(conv_w_pw2[0]), _row2(conv_b_pw2[0]))
    ffn_w = (_row2(norm_ffn[0]), bf(ffn_w_gate[0]), bf(ffn_w_up[0]), bf(ffn_w_down[0]))
    wq = w_q[0].reshape(d, N_KV_HEADS, GROUP, HEAD_DIM).transpose(0, 2, 1, 3).reshape(d, d)
    wo = w_o[0].reshape(N_KV_HEADS, GROUP, HEAD_DIM, d).transpose(1, 0, 2, 3).reshape(d, d)
    sinks_gk = sinks[0].astype(F32).reshape(N_KV_HEADS, GROUP).T.reshape(N_HEADS)
    kvq_w = (_row2(norm_ple[0]), bf(ple_w_gate[0]), bf(ple_w_proj[0]), _row2(norm_kv), bf(w_k), bf(w_v),
             _row2(jnp.tile(k_norm, N_KV_HEADS)), _row2(norm_mix[1]), bf(wq), _row2(jnp.tile(q_norm[0], N_HEADS)))
    wr = jnp.zeros((d, LANES), F32).at[:, :N_EXPERTS].set(moe_w_router[0])
    wr_hi = bf(wr)
    wr_lo = bf(wr - wr_hi.astype(F32))
    br = jnp.full((1, LANES), NEG_BIG, F32).at[0, :N_EXPERTS].set(moe_b_router[0])
    router_w = (bf(wo), _row2(norm_ffn[1]), jnp.concatenate([wr_hi, wr_lo], axis=1), br)
    ple1_w = (_row2(norm_ple[1]), bf(ple_w_gate[1]), bf(ple_w_proj[1]))

    h1p, conv_p = _conv_prompt(x_prompt, *conv_w)
    pp = p_prompt.reshape(-1, PLE_DIM)
    ps = p_sample.reshape(-1, PLE_DIM)
    h3p, kp, vp, qp = _ple_kvq(_ffn(h1p.reshape(mp, d), *ffn_w), pp, *kvq_w)
    h1s, conv_s = _conv_sample(x_sample.reshape(ns, d), state_conv, *conv_w)
    h3s, ks, vs, qs = _ple_kvq(_ffn(h1s, *ffn_w), ps, *kvq_w)

    kp3 = kp.reshape(nb, t, KV_DIM)
    vp3 = vp.reshape(nb, t, KV_DIM)
    op = _attn_prompt(qp.reshape(nb, t, d), kp3, vp3, sinks_gk)
    os_, k_state_s, v_state_s = _attn_sample(qs, cache_k.reshape(ns, WINDOW, KV_DIM),
                                             cache_v.reshape(ns, WINDOW, KV_DIM), ks, vs,
                                             sinks_gk.reshape(N_HEADS, 1))

    mt = mp + ns
    rows = jnp.zeros((mt * ROW_TILES, LANES), F32)
    h4p, rows, route_p, sort_p, cnt_p = _oproj_router(rows, op.reshape(mp, d), h3p, *router_w, first_token=0)
    h4s, rows, route_s, sort_s, cnt_s = _oproj_router(rows, os_, h3s, *router_w, first_token=mp)

    cnt_p = cnt_p[0, :N_EXPERTS].astype(jnp.int32)
    cnt_s = cnt_s[0, :N_EXPERTS].astype(jnp.int32)
    group = (cnt_p + cnt_s + MOE_TILE - 1) // MOE_TILE * MOE_TILE
    ends = jnp.cumsum(group)
    starts = ends - group
    n_tiles = (2 * (mp + ns) + N_EXPERTS * (MOE_TILE - 1)) // MOE_TILE
    n_valid = (ends[-1] // MOE_TILE).astype(jnp.int32)
    tile_start = jnp.minimum(jnp.arange(n_tiles, dtype=jnp.int32), n_valid - 1) * MOE_TILE
    tile_expert = jnp.sum(tile_start[:, None] >= ends[None, :], axis=1).astype(jnp.int32)

    def positions(sort_info, slot, base):
        idx = sort_info[slot].astype(jnp.int32)
        rank = sort_info[2 + slot].astype(jnp.int32)
        return (starts + base)[idx] + rank
    no_base = jnp.zeros_like(cnt_p)
    pos = jnp.concatenate([positions(sort_p, 0, no_base), positions(sort_p, 1, no_base),
                           positions(sort_s, 0, cnt_p), positions(sort_s, 1, cnt_p)]).astype(jnp.int32)
    pair_regions = ((0, mp, 0), (mp, mp, 0), (2 * mp, ns, mp), (2 * mp + ns, ns, mp))
    tile_ids = jnp.arange(n_tiles, dtype=jnp.int32)
    rows_left = (starts + cnt_p + cnt_s)[tile_expert] - tile_ids * MOE_TILE
    tile_rows = jnp.where(tile_ids < n_valid, jnp.clip(rows_left, 0, MOE_TILE), 0).astype(jnp.int32)
    y_pairs = _moe(tile_expert, n_valid.reshape(1), tile_rows, pos, pair_regions, rows,
                   bf(moe_w_gate[0]), bf(moe_w_up[0]), bf(moe_w_down[0]))

    y_p = _combine(y_pairs, h4p, route_p, pp, *ple1_w, first_pair=0)
    y_s = _combine(y_pairs, h4s, route_s, ps, *ple1_w, first_pair=2 * mp)

    kv4 = lambda a, n: a.reshape(n, WINDOW, N_KV_HEADS, HEAD_DIM)
    return (y_p.reshape(nb, t, d), y_s.reshape(ns, 1, d), conv_p,
            kv4(kp3[:, t - WINDOW:], nb), kv4(vp3[:, t - WINDOW:], nb),
            conv_s, kv4(k_state_s, ns), kv4(v_state_s, ns))
```

```python
import functools

import jax
import jax.numpy as jnp
from jax import lax
from jax.experimental import pallas as pl
from jax.experimental.pallas import tpu as pltpu

D_MODEL = 1024
CONV_WIDTH = 31
CONV_CTX = CONV_WIDTH - 1
HEAD_DIM = 64
N_HEADS = 16
N_KV_HEADS = 4
GROUP = N_HEADS // N_KV_HEADS
KV_DIM = N_KV_HEADS * HEAD_DIM
WINDOW = 128
D_FF = 2816
N_EXPERTS = 8
D_EXPERT = 3584
PLE_DIM = 256
EPS = 1e-6

BF16 = jnp.bfloat16
F32 = jnp.float32

LANES = 128
SUBLANES = 8
ROW_TILES = D_MODEL // LANES
VMEM_LIMIT = 56 * 1024 * 1024

CONV_PAD = 32
CONV_ROWS = 128
TOK_TILE = 512
MOE_TILE = 512
MOE_F_TILE = 1792
COMBINE_TILE = 512
COPY_UNROLL = 8
NEG_BIG = -1e30


def _params(*sem):
    return pltpu.CompilerParams(dimension_semantics=sem, vmem_limit_bytes=VMEM_LIMIT)


def _const(shape):
    nd = len(shape)
    return pl.BlockSpec(shape, lambda *_: (0,) * nd, pipeline_mode=pl.Buffered(1))


def _rms(x, g):
    return x * lax.rsqrt(jnp.mean(x * x, axis=-1, keepdims=True) + EPS) * g


def _dot(a, w):
    return jnp.dot(a.astype(BF16), w, preferred_element_type=F32)


def _sigmoid(x):
    return 1.0 / (1.0 + jnp.exp(-x))


def _silu(x):
    return x * _sigmoid(x)


def _layer_norm(x, g, b):
    mu = jnp.mean(x, axis=-1, keepdims=True)
    xc = x - mu
    var = jnp.mean(xc * xc, axis=-1, keepdims=True)
    return xc * lax.rsqrt(var + EPS) * g + b


def _head_rms(x, g):
    t, w = x.shape
    lane = lax.broadcasted_iota(jnp.int32, (t, LANES), 1)
    lo = lane < HEAD_DIM
    outs = []
    for c in range(w // LANES):
        blk = x[:, c * LANES:(c + 1) * LANES]
        sq = blk * blk
        s_lo = jnp.sum(jnp.where(lo, sq, 0.0), axis=-1, keepdims=True)
        s_hi = jnp.sum(jnp.where(lo, 0.0, sq), axis=-1, keepdims=True)
        inv = jnp.where(lo, lax.rsqrt(s_lo / HEAD_DIM + EPS), lax.rsqrt(s_hi / HEAD_DIM + EPS))
        outs.append(blk * inv)
    return jnp.concatenate(outs, axis=1) * g


def _rows_to_tiles(ref, x):
    t = x.shape[0]
    for s in range(ROW_TILES):
        ref[pl.ds(s, t, stride=ROW_TILES), :] = x[:, s * LANES:(s + 1) * LANES]


def _tiles_to_rows(ref, t):
    return jnp.concatenate([ref[pl.ds(s, t, stride=ROW_TILES), :] for s in range(ROW_TILES)], axis=1)


def _conv_taps(uext_ref, wdw_ref, bdw_ref, c_ref, tt):
    win_rows = CONV_ROWS + CONV_PAD

    def chunk(ci, carry):
        r0 = pl.multiple_of(ci * CONV_ROWS, CONV_ROWS)
        for lt in range(ROW_TILES):
            cols = slice(lt * LANES, (lt + 1) * LANES)
            win = uext_ref[pl.ds(r0, win_rows), cols]
            acc = jnp.broadcast_to(bdw_ref[:, cols], (CONV_ROWS, LANES))
            for b in range(SUBLANES):
                wb = win if b == 0 else pltpu.roll(win, win_rows - b, axis=0)
                for a in range(CONV_PAD // SUBLANES + 1):
                    j = a * SUBLANES + b - (CONV_PAD - CONV_CTX)
                    if 0 <= j < CONV_WIDTH:
                        acc = acc + wdw_ref[j:j + 1, cols] * wb[a * SUBLANES:a * SUBLANES + CONV_ROWS]
            c_ref[pl.ds(r0, CONV_ROWS), cols] = acc
        return carry
    lax.fori_loop(0, tt // CONV_ROWS, chunk, 0)


def _conv_prompt_kernel(x_ref, nm_ref, w1_ref, b1_ref, wdw_ref, bdw_ref, lng_ref, lnb_ref, w2_ref, b2_ref,
                        h_ref, st_ref, uext_ref, c_ref, *, tt):
    t = pl.program_id(1)

    @pl.when(t == 0)
    def _():
        uext_ref[0:CONV_PAD, :] = jnp.zeros((CONV_PAD, D_MODEL), F32)

    @pl.when(t > 0)
    def _():
        uext_ref[0:CONV_PAD, :] = uext_ref[tt:tt + CONV_PAD, :]

    x = x_ref[0]
    z = _dot(_rms(x, nm_ref[...]), w1_ref[...]) + b1_ref[...]
    u = z[:, :D_MODEL] * _sigmoid(z[:, D_MODEL:])
    uext_ref[CONV_PAD:CONV_PAD + tt, :] = u
    _conv_taps(uext_ref, wdw_ref, bdw_ref, c_ref, tt)
    c = _silu(_layer_norm(c_ref[...], lng_ref[...], lnb_ref[...]))
    h_ref[0] = x + _dot(c, w2_ref[...]) + b2_ref[...]
    st_ref[0, 0] = uext_ref[CONV_PAD + tt - CONV_CTX:CONV_PAD + tt, :]


def _conv_prompt(x, nm, w1, b1, wdw, bdw, lng, lnb, w2, b2):
    n, t, d = x.shape
    tt = TOK_TILE
    weights = (nm, w1, b1, wdw, bdw, lng, lnb, w2, b2)
    return pl.pallas_call(
        functools.partial(_conv_prompt_kernel, tt=tt),
        grid=(n, t // tt),
        in_specs=[pl.BlockSpec((1, tt, d), lambda b, i: (b, i, 0))] + [_const(w.shape) for w in weights],
        out_specs=[pl.BlockSpec((1, tt, d), lambda b, i: (b, i, 0)),
                   pl.BlockSpec((1, 1, CONV_CTX, d), lambda b, i: (0, b, 0, 0))],
        out_shape=[jax.ShapeDtypeStruct((n, t, d), F32), jax.ShapeDtypeStruct((1, n, CONV_CTX, d), F32)],
        scratch_shapes=[pltpu.VMEM((CONV_PAD + tt, d), F32), pltpu.VMEM((tt, d), F32)],
        compiler_params=_params("arbitrary", "arbitrary"),
        name="conv_prompt",
    )(x, *weights)


def _conv_sample_kernel(x_ref, past_ref, nm_ref, w1_ref, b1_ref, wdw_ref, bdw_ref, lng_ref, lnb_ref,
                        w2_ref, b2_ref, h_ref, st_ref):
    x = x_ref[...]
    z = _dot(_rms(x, nm_ref[...]), w1_ref[...]) + b1_ref[...]
    u = z[:, :D_MODEL] * _sigmoid(z[:, D_MODEL:])
    c = wdw_ref[CONV_CTX:CONV_WIDTH, :] * u + bdw_ref[...]
    for j in range(CONV_CTX):
        c = c + wdw_ref[j:j + 1, :] * past_ref[0, j]
        if j > 0:
            st_ref[0, j - 1] = past_ref[0, j]
    st_ref[0, CONV_CTX - 1] = u
    c = _silu(_layer_norm(c, lng_ref[...], lnb_ref[...]))
    h_ref[...] = x + _dot(c, w2_ref[...]) + b2_ref[...]


def _conv_sample(x, past, nm, w1, b1, wdw, bdw, lng, lnb, w2, b2):
    n, d = x.shape
    nb = 16
    weights = (nm, w1, b1, wdw, bdw, lng, lnb, w2, b2)
    state = pl.BlockSpec((1, CONV_CTX, nb, d), lambda i: (0, 0, i, 0))
    return pl.pallas_call(
        _conv_sample_kernel,
        grid=(n // nb,),
        in_specs=[pl.BlockSpec((nb, d), lambda i: (i, 0)), state] + [_const(w.shape) for w in weights],
        out_specs=[pl.BlockSpec((nb, d), lambda i: (i, 0)), state],
        out_shape=[jax.ShapeDtypeStruct((n, d), F32), jax.ShapeDtypeStruct((1, CONV_CTX, n, d), F32)],
        compiler_params=_params("arbitrary"),
        name="conv_sample",
    )(x, past, *weights)


def _ffn_kernel(h_ref, g_ref, wg_ref, wu_ref, wd_ref, o_ref):
    h = h_ref[...]
    hn = _rms(h, g_ref[...]).astype(BF16)
    a = jnp.dot(hn, wg_ref[...], preferred_element_type=F32)
    b = jnp.dot(hn, wu_ref[...], preferred_element_type=F32)
    o_ref[...] = h + _dot(_silu(a) * b, wd_ref[...])


def _ffn(h, g, wg, wu, wd):
    m, d = h.shape
    tm = min(TOK_TILE, m)
    weights = (g, wg, wu, wd)
    return pl.pallas_call(
        _ffn_kernel,
        grid=(m // tm,),
        in_specs=[pl.BlockSpec((tm, d), lambda i: (i, 0))] + [_const(w.shape) for w in weights],
        out_specs=pl.BlockSpec((tm, d), lambda i: (i, 0)),
        out_shape=jax.ShapeDtypeStruct((m, d), F32),
        compiler_params=_params("arbitrary"),
        name="dense_ffn",
    )(h, *weights)


def _ple(h, p, g, w_gate, w_proj):
    return h + _sigmoid(_dot(_rms(h, g), w_gate)) * _dot(p, w_proj)


def _ple_kvq_kernel(h_ref, p_ref, gple_ref, wpg_ref, wpp_ref, gkv_ref, wk_ref, wv_ref, kn_ref,
                    gmix_ref, wq_ref, qn_ref, h3_ref, k_ref, v_ref, q_ref):
    h3 = _ple(h_ref[...], p_ref[...], gple_ref[...], wpg_ref[...], wpp_ref[...])
    h3_ref[...] = h3
    s = _rms(h3, gkv_ref[...]).astype(BF16)
    k_ref[...] = _head_rms(jnp.dot(s, wk_ref[...], preferred_element_type=F32), kn_ref[...])
    v_ref[...] = jnp.dot(s, wv_ref[...], preferred_element_type=F32)
    q = _dot(_rms(h3, gmix_ref[...]), wq_ref[...])
    q_ref[...] = (_head_rms(q, qn_ref[...]) * (HEAD_DIM ** -0.5)).astype(BF16)


def _ple_kvq(h, p, gple, wpg, wpp, gkv, wk, wv, kn, gmix, wq, qn):
    m, d = h.shape
    tm = min(TOK_TILE, m)
    weights = (gple, wpg, wpp, gkv, wk, wv, kn, gmix, wq, qn)
    row = lambda w: pl.BlockSpec((tm, w), lambda i: (i, 0))
    return pl.pallas_call(
        _ple_kvq_kernel,
        grid=(m // tm,),
        in_specs=[row(d), row(PLE_DIM)] + [_const(w.shape) for w in weights],
        out_specs=[row(d), row(KV_DIM), row(KV_DIM), row(d)],
        out_shape=[jax.ShapeDtypeStruct((m, d), F32), jax.ShapeDtypeStruct((m, KV_DIM), F32),
                   jax.ShapeDtypeStruct((m, KV_DIM), F32), jax.ShapeDtypeStruct((m, d), BF16)],
        compiler_params=_params("arbitrary"),
        name="ple_kv_q",
    )(h, p, *weights)


def _softmax_sink(s, sink):
    m = jnp.maximum(jnp.max(s, axis=-1, keepdims=True), sink)
    e = jnp.exp(s - m)
    den = jnp.sum(e, axis=-1, keepdims=True) + jnp.exp(sink - m)
    return e / den


def _attn_prompt_kernel(q_ref, kp_ref, kc_ref, vp_ref, vc_ref, sink_ref, o_ref):
    n = pl.program_id(1)
    q = q_ref[0]
    k = jnp.concatenate([kp_ref[0], kc_ref[0]], axis=0).astype(BF16)
    v = jnp.concatenate([vp_ref[0], vc_ref[0]], axis=0).astype(BF16)
    qi = lax.broadcasted_iota(jnp.int32, (WINDOW, 2 * WINDOW), 0)
    kj = lax.broadcasted_iota(jnp.int32, (WINDOW, 2 * WINDOW), 1)
    dist = qi + WINDOW - kj
    first_key = jnp.where(n > 0, 0, WINDOW)
    mask = (dist >= 0) & (dist <= WINDOW) & (kj >= first_key)
    outs = []
    for h in range(N_HEADS):
        kh = h % N_KV_HEADS
        qh = q[:, h * HEAD_DIM:(h + 1) * HEAD_DIM]
        kk = k[:, kh * HEAD_DIM:(kh + 1) * HEAD_DIM]
        vv = v[:, kh * HEAD_DIM:(kh + 1) * HEAD_DIM]
        s = lax.dot_general(qh, kk, (((1,), (1,)), ((), ())), preferred_element_type=F32)
        s = jnp.where(mask, s, -jnp.inf)
        p = _softmax_sink(s, sink_ref[h])
        outs.append(jnp.dot(p.astype(BF16), vv, preferred_element_type=F32))
    o_ref[0] = jnp.concatenate(outs, axis=1).astype(BF16)


def _attn_prompt(q, k, v, sinks):
    n, t, d = q.shape
    nb = t // WINDOW
    cur = lambda w: pl.BlockSpec((1, WINDOW, w), lambda b, i: (b, i, 0))
    prev = lambda w: pl.BlockSpec((1, WINDOW, w), lambda b, i: (b, jnp.maximum(i - 1, 0), 0))
    return pl.pallas_call(
        _attn_prompt_kernel,
        grid=(n, nb),
        in_specs=[cur(d), prev(KV_DIM), cur(KV_DIM), prev(KV_DIM), cur(KV_DIM),
                  pl.BlockSpec(memory_space=pltpu.SMEM)],
        out_specs=cur(d),
        out_shape=jax.ShapeDtypeStruct((n, t, d), BF16),
        compiler_params=_params("arbitrary", "arbitrary"),
        name="attn_prompt",
    )(q, k, k, v, v, sinks)


def _attn_sample_kernel(q_ref, kc_ref, vc_ref, kn_ref, vn_ref, sink_ref, o_ref, ks_ref, vs_ref, *, nb):
    lane_head = lax.broadcasted_iota(jnp.int32, (N_KV_HEADS, KV_DIM), 1) // HEAD_DIM
    own = lane_head == lax.broadcasted_iota(jnp.int32, (N_KV_HEADS, KV_DIM), 0)
    sink = sink_ref[...]
    rows = []
    for i in range(nb):
        kc = kc_ref[i].T
        vc = vc_ref[i].T
        kn = kn_ref[i:i + 1, :]
        vn = vn_ref[i:i + 1, :]
        ks_ref[i, 0:WINDOW - 1, :] = kc[1:WINDOW, :]
        ks_ref[i, WINDOW - 1:WINDOW, :] = kn
        vs_ref[i, 0:WINDOW - 1, :] = vc[1:WINDOW, :]
        vs_ref[i, WINDOW - 1:WINDOW, :] = vn
        qrow = q_ref[i:i + 1, :].astype(F32)
        qx = jnp.concatenate(
            [jnp.where(own, jnp.broadcast_to(qrow[:, g * KV_DIM:(g + 1) * KV_DIM], (N_KV_HEADS, KV_DIM)), 0.0)
             for g in range(GROUP)], axis=0)
        s_c = lax.dot_general(qx.astype(BF16), kc.astype(BF16), (((1,), (1,)), ((), ())),
                              preferred_element_type=F32)
        s_n = jnp.sum(qx * kn.astype(BF16).astype(F32), axis=-1, keepdims=True)
        m = jnp.maximum(jnp.maximum(jnp.max(s_c, axis=-1, keepdims=True), s_n), sink)
        e_c = jnp.exp(s_c - m)
        e_n = jnp.exp(s_n - m)
        den = jnp.sum(e_c, axis=-1, keepdims=True) + e_n + jnp.exp(sink - m)
        ox = jnp.dot((e_c / den).astype(BF16), vc.astype(BF16), preferred_element_type=F32)
        ox = ox + (e_n / den).astype(BF16).astype(F32) * vn.astype(BF16).astype(F32)
        rows.append(jnp.concatenate(
            [jnp.sum(jnp.where(own, ox[g * N_KV_HEADS:(g + 1) * N_KV_HEADS], 0.0), axis=0, keepdims=True)
             for g in range(GROUP)], axis=1))
    o_ref[...] = jnp.concatenate(rows, axis=0).astype(BF16)


def _attn_sample(q, kc, vc, kn, vn, sinks_col):
    n, d = q.shape
    nb = 16
    row = lambda w: pl.BlockSpec((nb, w), lambda i: (i, 0))
    cache_in = pl.BlockSpec((nb, KV_DIM, WINDOW), lambda i: (i, 0, 0))
    cache = pl.BlockSpec((nb, WINDOW, KV_DIM), lambda i: (i, 0, 0))
    return pl.pallas_call(
        functools.partial(_attn_sample_kernel, nb=nb),
        grid=(n // nb,),
        in_specs=[row(d), cache_in, cache_in, row(KV_DIM), row(KV_DIM), _const(sinks_col.shape)],
        out_specs=[row(d), cache, cache],
        out_shape=[jax.ShapeDtypeStruct((n, d), BF16),
                   jax.ShapeDtypeStruct((n, WINDOW, KV_DIM), F32),
                   jax.ShapeDtypeStruct((n, WINDOW, KV_DIM), F32)],
        compiler_params=_params("arbitrary"),
        name="attn_sample",
    )(q, kc, vc, kn, vn, sinks_col)


def _oproj_router_kernel(rows_in_ref, o_ref, h_ref, wo_ref, g_ref, wr_ref, br_ref,
                         h4_ref, hn_ref, route_ref, sort_ref, cnt_ref, carry_ref, *, tm):
    del rows_in_ref
    i = pl.program_id(0)

    @pl.when(i == 0)
    def _():
        carry_ref[...] = jnp.zeros((1, LANES), F32)

    h4 = h_ref[...] + jnp.dot(o_ref[...], wo_ref[...], preferred_element_type=F32)
    h4_ref[...] = h4
    hn = _rms(h4, g_ref[...])
    _rows_to_tiles(hn_ref, hn)
    hn_hi = hn.astype(BF16)
    hn_lo = (hn - hn_hi.astype(F32)).astype(BF16)
    part_hi = jnp.dot(hn_hi, wr_ref[...], preferred_element_type=F32)
    part_lo = jnp.dot(hn_lo, wr_ref[...], preferred_element_type=F32)
    logits = (part_hi[:, :LANES] + (part_hi[:, LANES:] + part_lo[:, :LANES]) + part_lo[:, LANES:]) + br_ref[...]
    lane = lax.broadcasted_iota(jnp.int32, (tm, LANES), 1).astype(F32)
    m1 = jnp.max(logits, axis=-1, keepdims=True)
    i1 = jnp.min(jnp.where(logits == m1, lane, float(LANES)), axis=-1, keepdims=True)
    rest = jnp.where(lane == i1, -jnp.inf, logits)
    m2 = jnp.max(rest, axis=-1, keepdims=True)
    i2 = jnp.min(jnp.where(rest == m2, lane, float(LANES)), axis=-1, keepdims=True)
    e2 = jnp.exp(m2 - m1)
    w1 = 1.0 / (1.0 + e2)
    w2 = e2 / (1.0 + e2)
    oh1 = lane == i1
    oh2 = lane == i2
    sel = jnp.where(oh1, 1.0, 0.0) + jnp.where(oh2, 1.0, 0.0)
    row = lax.broadcasted_iota(jnp.int32, (tm, tm), 0)
    col = lax.broadcasted_iota(jnp.int32, (tm, tm), 1)
    tri = jnp.where(row > col, 1.0, 0.0).astype(BF16)
    before = jnp.dot(tri, sel.astype(BF16), preferred_element_type=F32) + carry_ref[...]
    r1 = jnp.sum(jnp.where(oh1, before, 0.0), axis=-1, keepdims=True)
    r2 = jnp.sum(jnp.where(oh2, before, 0.0), axis=-1, keepdims=True)
    carry_ref[...] = carry_ref[...] + jnp.sum(sel, axis=0, keepdims=True)
    cnt_ref[...] = carry_ref[...]
    fields = (i1, i2, r1, r2, w1, w2)
    route = jnp.zeros((tm, LANES), F32)
    for f, val in enumerate(fields):
        route = jnp.where(lane == float(f), val, route)
    route_ref[...] = route
    sort_ref[...] = route.T[0:SUBLANES, :]


def _oproj_router(rows_buf, o, h, wo, g, wr, br, *, first_token):
    m, d = h.shape
    tm = min(TOK_TILE, m)
    weights = (wo, g, wr, br)
    row = lambda w: pl.BlockSpec((tm, w), lambda i: (i, 0))
    first_block = first_token // tm
    return pl.pallas_call(
        functools.partial(_oproj_router_kernel, tm=tm),
        grid=(m // tm,),
        in_specs=[pl.BlockSpec(memory_space=pl.ANY), row(d), row(d)] + [_const(w.shape) for w in weights],
        out_specs=[row(d), pl.BlockSpec((tm * ROW_TILES, LANES), lambda i: (i + first_block, 0)), row(LANES),
                   pl.BlockSpec((SUBLANES, tm), lambda i: (0, i)), pl.BlockSpec((1, LANES), lambda i: (0, 0))],
        out_shape=[jax.ShapeDtypeStruct((m, d), F32), jax.ShapeDtypeStruct(rows_buf.shape, F32),
                   jax.ShapeDtypeStruct((m, LANES), F32), jax.ShapeDtypeStruct((SUBLANES, m), F32),
                   jax.ShapeDtypeStruct((1, LANES), F32)],
        scratch_shapes=[pltpu.VMEM((1, LANES), F32)],
        input_output_aliases={0: 1},
        compiler_params=_params("arbitrary"),
        name="oproj_router",
    )(rows_buf, o, h, *weights)


def _row(ref, r):
    return ref.at[pl.ds(pl.multiple_of(r * ROW_TILES, ROW_TILES), ROW_TILES), :]


def _for_rows(lo, hi, fn):
    assert (hi - lo) % COPY_UNROLL == 0

    def body(j, carry):
        for u in range(COPY_UNROLL):
            fn(lo + j * COPY_UNROLL + u)
        return carry
    lax.fori_loop(0, (hi - lo) // COPY_UNROLL, body, 0)


def _moe_kernel(te_ref, nv_ref, nr_ref, pos_ref, rows_ref, wg_ref, wu_ref, wd_ref, y_ref,
                inv_ref, xg_ref, yb_ref, xb_ref, acc_ref, gsem, ssem, isem, *, tm, nf, nt, n_pairs, pair_regions):
    i = pl.program_id(0)
    f = pl.program_id(1)
    nv = nv_ref[0]
    valid = i < nv
    slot = i % 2
    assert nf == 2

    def gather_row(tile, r, to_slot):
        pair = inv_ref[tile * tm + r]
        token = pair
        for first_pair, _, first_token in pair_regions[1:]:
            token = jnp.where(pair >= first_pair, pair - (first_pair - first_token), token)
        pltpu.make_async_copy(_row(rows_ref, token), _row(xg_ref.at[to_slot], r), gsem.at[to_slot]).start()

    def scatter_row(tile, real, r, from_slot):
        pair = jnp.where(r < real, inv_ref[tile * tm + r], n_pairs + from_slot * tm + r)
        pltpu.make_async_copy(_row(yb_ref.at[from_slot], r), _row(y_ref, pair), ssem.at[from_slot]).start()

    def wait_gather(at_slot):
        pltpu.make_async_copy(rows_ref.at[pl.ds(0, tm * ROW_TILES), :], xg_ref.at[at_slot], gsem.at[at_slot]).wait()

    def wait_scatter(at_slot):
        pltpu.make_async_copy(yb_ref.at[at_slot], y_ref.at[pl.ds(0, tm * ROW_TILES), :], ssem.at[at_slot]).wait()

    @pl.when(jnp.logical_and(i == 0, f == 0))
    def _():
        def place(p):
            inv_ref[pos_ref[p]] = p
        _for_rows(0, n_pairs, place)

        def clear_padding(t, carry):
            def clear(r, c):
                inv_ref[t * tm + r] = 0
                return c
            return lax.fori_loop(nr_ref[t], tm, clear, carry)
        lax.fori_loop(0, nv, clear_padding, 0)
        yb_ref[...] = jnp.zeros(yb_ref.shape, F32)
        spare = y_ref.at[pl.ds(n_pairs * ROW_TILES, 2 * tm * ROW_TILES), :]
        for s in range(2):
            half = spare.at[pl.ds(s * tm * ROW_TILES, tm * ROW_TILES), :]
            pltpu.make_async_copy(yb_ref.at[s], half, isem).start()
            pltpu.make_async_copy(yb_ref.at[s], half, isem).wait()
        _for_rows(0, tm, lambda r: gather_row(0, r, 0))

    @pl.when(jnp.logical_and(valid, f == 0))
    def _():
        wait_gather(slot)
        xb_ref[...] = _tiles_to_rows(xg_ref.at[slot], tm).astype(BF16)
        acc_ref[...] = jnp.zeros((tm, D_MODEL), F32)

    def matmuls(first):
        if first:
            nxt = jnp.minimum(i + 1, nv - 1)
            for r in range(tm):
                gather_row(nxt, r, 1 - slot)
        else:
            prev = jnp.maximum(i - 1, 0)
            real = jnp.where(i >= 1, nr_ref[prev], 0)
            for r in range(tm):
                scatter_row(prev, real, r, 1 - slot)
        x = xb_ref[...]
        a = jnp.dot(x, wg_ref[0], preferred_element_type=F32)
        b = jnp.dot(x, wu_ref[0], preferred_element_type=F32)
        acc_ref[...] += _dot(_silu(a) * b, wd_ref[0])

    @pl.when(jnp.logical_and(valid, f == 0))
    def _():
        matmuls(True)

    @pl.when(jnp.logical_and(valid, f > 0))
    def _():
        matmuls(False)

    @pl.when(jnp.logical_and(valid, f == nf - 1))
    def _():
        @pl.when(i >= 1)
        def _():
            wait_scatter(slot)
        _rows_to_tiles(yb_ref.at[slot], acc_ref[...])

        @pl.when(i == nv - 1)
        def _():
            _for_rows(0, tm, lambda r: scatter_row(i, nr_ref[i], r, slot))

    @pl.when(jnp.logical_and(i == nt - 1, f == nf - 1))
    def _():
        wait_gather(nv % 2)
        wait_scatter(nv % 2)
        wait_scatter((nv - 1) % 2)


def _moe(tile_expert, n_valid, tile_rows, pos, pair_regions, rows, wg, wu, wd):
    tm, tf = MOE_TILE, MOE_F_TILE
    nt, nf = tile_expert.shape[0], D_EXPERT // tf
    n_pairs = pos.shape[0]

    def fidx(i, f, nv):
        return jnp.where(i < nv[0], f, nf - 1)
    tile_buf = pltpu.VMEM((2, tm * ROW_TILES, LANES), F32)
    return pl.pallas_call(
        functools.partial(_moe_kernel, tm=tm, nf=nf, nt=nt, n_pairs=n_pairs, pair_regions=pair_regions),
        grid_spec=pltpu.PrefetchScalarGridSpec(
            num_scalar_prefetch=4,
            grid=(nt, nf),
            in_specs=[pl.BlockSpec(memory_space=pl.ANY),
                      pl.BlockSpec((1, D_MODEL, tf), lambda i, f, te, nv, *_: (te[i], 0, fidx(i, f, nv))),
                      pl.BlockSpec((1, D_MODEL, tf), lambda i, f, te, nv, *_: (te[i], 0, fidx(i, f, nv))),
                      pl.BlockSpec((1, tf, D_MODEL), lambda i, f, te, nv, *_: (te[i], fidx(i, f, nv), 0))],
            out_specs=pl.BlockSpec(memory_space=pl.ANY),
            scratch_shapes=[pltpu.SMEM((nt * tm,), jnp.int32), tile_buf, tile_buf,
                            pltpu.VMEM((tm, D_MODEL), BF16), pltpu.VMEM((tm, D_MODEL), F32),
                            pltpu.SemaphoreType.DMA((2,)), pltpu.SemaphoreType.DMA((2,)),
                            pltpu.SemaphoreType.DMA(())],
        ),
        out_shape=jax.ShapeDtypeStruct(((n_pairs + 2 * tm) * ROW_TILES, LANES), F32),
        compiler_params=_params("arbitrary", "arbitrary"),
        name="moe_grouped",
    )(tile_expert, n_valid, tile_rows, pos, rows, wg, wu, wd)


def _combine_kernel(y1_ref, y2_ref, h_ref, route_ref, p_ref, g_ref, wpg_ref, wpp_ref, o_ref, *, tb):
    route = route_ref[...]
    w1 = route[:, 4:5]
    w2 = route[:, 5:6]
    h5 = h_ref[...] + (w1 * _tiles_to_rows(y1_ref, tb) + w2 * _tiles_to_rows(y2_ref, tb))
    o_ref[...] = _ple(h5, p_ref[...], g_ref[...], wpg_ref[...], wpp_ref[...])


def _combine(y_pairs, h, route, p, g, wpg, wpp, *, first_pair):
    m, d = h.shape
    tb = min(COMBINE_TILE, m)
    weights = (g, wpg, wpp)
    row = lambda w: pl.BlockSpec((tb, w), lambda i: (i, 0))
    steps = m // tb
    first_block = first_pair // tb
    pair_rows = lambda slot: pl.BlockSpec((tb * ROW_TILES, LANES), lambda i: (i + first_block + slot * steps, 0))
    return pl.pallas_call(
        functools.partial(_combine_kernel, tb=tb),
        grid=(steps,),
        in_specs=[pair_rows(0), pair_rows(1), row(d), row(LANES),
                  pl.BlockSpec((tb, PLE_DIM), lambda i: (i + steps, 0))] + [_const(w.shape) for w in weights],
        out_specs=row(d),
        out_shape=jax.ShapeDtypeStruct((m, d), F32),
        compiler_params=_params("arbitrary"),
        name="moe_combine",
    )(y_pairs, y_pairs, h, route, p, *weights)


def _row2(v):
    return v.reshape(1, -1).astype(F32)


def kernel(x_prompt, x_sample, state_conv, cache_k, cache_v, p_prompt, p_sample, norm_mix, norm_ffn, norm_ple,
           conv_w_pw1, conv_b_pw1, conv_w_dw, conv_b_dw, conv_ln_g, conv_ln_b, conv_w_pw2, conv_b_pw2,
           norm_kv, w_k, w_v, k_norm, w_q, q_norm, sinks, w_o, ffn_w_gate, ffn_w_up, ffn_w_down,
           moe_w_router, moe_b_router, moe_w_gate, moe_w_up, moe_w_down, ple_w_gate, ple_w_proj):
    nb, t, d = x_prompt.shape
    ns = x_sample.shape[0]
    mp = nb * t
    bf = lambda w: w.astype(BF16)

    conv_w = (_row2(norm_mix[0]), bf(conv_w_pw1[0]), _row2(conv_b_pw1[0]), conv_w_dw[0], _row2(conv_b_dw[0]),
              _row2(conv_ln_g[0]), _row2(conv_ln_b[0]), bf(conv_w_pw2[0]), _row2(conv_b_pw2[0]))
    ffn_w = (_row2(norm_ffn[0]), bf(ffn_w_gate[0]), bf(ffn_w_up[0]), bf(ffn_w_down[0]))
    wq = w_q[0].reshape(d, N_KV_HEADS, GROUP, HEAD_DIM).transpose(0, 2, 1, 3).reshape(d, d)
    wo = w_o[0].reshape(N_KV_HEADS, GROUP, HEAD_DIM, d).transpose(1, 0, 2, 3).reshape(d, d)
    sinks_gk = sinks[0].astype(F32).reshape(N_KV_HEADS, GROUP).T.reshape(N_HEADS)
    kvq_w = (_row2(norm_ple[0]), bf(ple_w_gate[0]), bf(ple_w_proj[0]), _row2(norm_kv), bf(w_k), bf(w_v),
             _row2(jnp.tile(k_norm, N_KV_HEADS)), _row2(norm_mix[1]), bf(wq), _row2(jnp.tile(q_norm[0], N_HEADS)))
    wr = jnp.zeros((d, LANES), F32).at[:, :N_EXPERTS].set(moe_w_router[0])
    wr_hi = bf(wr)
    wr_lo = bf(wr - wr_hi.astype(F32))
    br = jnp.full((1, LANES), NEG_BIG, F32).at[0, :N_EXPERTS].set(moe_b_router[0])
    router_w = (bf(wo), _row2(norm_ffn[1]), jnp.concatenate([wr_hi, wr_lo], axis=1), br)
    ple1_w = (_row2(norm_ple[1]), bf(ple_w_gate[1]), bf(ple_w_proj[1]))

    h1p, conv_p = _conv_prompt(x_prompt, *conv_w)
    pp = p_prompt.reshape(-1, PLE_DIM)
    ps = p_sample.reshape(-1, PLE_DIM)
    h3p, kp, vp, qp = _ple_kvq(_ffn(h1p.reshape(mp, d), *ffn_w), pp, *kvq_w)
    h1s, conv_s = _conv_sample(x_sample.reshape(ns, d), state_conv.transpose(0, 2, 1, 3), *conv_w)
    conv_s = conv_s.transpose(0, 2, 1, 3)
    h3s, ks, vs, qs = _ple_kvq(_ffn(h1s, *ffn_w), ps, *kvq_w)

    kp3 = kp.reshape(nb, t, KV_DIM)
    vp3 = vp.reshape(nb, t, KV_DIM)
    op = _attn_prompt(qp.reshape(nb, t, d), kp3, vp3, sinks_gk)
    feature_major = lambda c: c.transpose(0, 2, 3, 1).reshape(ns, KV_DIM, WINDOW)
    os_, k_state_s, v_state_s = _attn_sample(qs, feature_major(cache_k), feature_major(cache_v), ks, vs,
                                             sinks_gk.reshape(N_HEADS, 1))

    mt = mp + ns
    rows = jnp.zeros((mt * ROW_TILES, LANES), F32)
    h4p, rows, route_p, sort_p, cnt_p = _oproj_router(rows, op.reshape(mp, d), h3p, *router_w, first_token=0)
    h4s, rows, route_s, sort_s, cnt_s = _oproj_router(rows, os_, h3s, *router_w, first_token=mp)

    cnt_p = cnt_p[0, :N_EXPERTS].astype(jnp.int32)
    cnt_s = cnt_s[0, :N_EXPERTS].astype(jnp.int32)
    group = (cnt_p + cnt_s + MOE_TILE - 1) // MOE_TILE * MOE_TILE
    ends = jnp.cumsum(group)
    starts = ends - group
    n_tiles = (2 * (mp + ns) + N_EXPERTS * (MOE_TILE - 1)) // MOE_TILE
    n_valid = (ends[-1] // MOE_TILE).astype(jnp.int32)
    tile_start = jnp.minimum(jnp.arange(n_tiles, dtype=jnp.int32), n_valid - 1) * MOE_TILE
    tile_expert = jnp.sum(tile_start[:, None] >= ends[None, :], axis=1).astype(jnp.int32)

    def positions(sort_info, slot, base):
        idx = sort_info[slot].astype(jnp.int32)
        rank = sort_info[2 + slot].astype(jnp.int32)
        return (starts + base)[idx] + rank
    no_base = jnp.zeros_like(cnt_p)
    pos = jnp.concatenate([positions(sort_p, 0, no_base), positions(sort_p, 1, no_base),
                           positions(sort_s, 0, cnt_p), positions(sort_s, 1, cnt_p)]).astype(jnp.int32)
    pair_regions = ((0, mp, 0), (mp, mp, 0), (2 * mp, ns, mp), (2 * mp + ns, ns, mp))
    tile_ids = jnp.arange(n_tiles, dtype=jnp.int32)
    rows_left = (starts + cnt_p + cnt_s)[tile_expert] - tile_ids * MOE_TILE
    tile_rows = jnp.where(tile_ids < n_valid, jnp.clip(rows_left, 0, MOE_TILE), 0).astype(jnp.int32)
    y_pairs = _moe(tile_expert, n_valid.reshape(1), tile_rows, pos, pair_regions, rows,
                   bf(moe_w_gate[0]), bf(moe_w_up[0]), bf(moe_w_down[0]))

    y_p = _combine(y_pairs, h4p, route_p, pp, *ple1_w, first_pair=0)
    y_s = _combine(y_pairs, h4s, route_s, ps, *ple1_w, first_pair=2 * mp)

    kv4 = lambda a, n: a.reshape(n, WINDOW, N_KV_HEADS, HEAD_DIM)
    return (y_p.reshape(nb, t, d), y_s.reshape(ns, 1, d), conv_p,
            kv4(kp3[:, t - WINDOW:], nb), kv4(vp3[:, t - WINDOW:], nb),
            conv_s, kv4(k_state_s, ns), kv4(v_state_s, ns))
```

```python
import functools

import jax
import jax.numpy as jnp
from jax import lax
from jax.experimental import pallas as pl
from jax.experimental.pallas import tpu as pltpu

D_MODEL = 1024
CONV_WIDTH = 31
CONV_CTX = CONV_WIDTH - 1
HEAD_DIM = 64
N_HEADS = 16
N_KV_HEADS = 4
GROUP = N_HEADS // N_KV_HEADS
KV_DIM = N_KV_HEADS * HEAD_DIM
WINDOW = 128
D_FF = 2816
N_EXPERTS = 8
D_EXPERT = 3584
PLE_DIM = 256
EPS = 1e-6

BF16 = jnp.bfloat16
F32 = jnp.float32

LANES = 128
SUBLANES = 8
ROW_TILES = D_MODEL // LANES
VMEM_LIMIT = 56 * 1024 * 1024

CONV_PAD = 32
CONV_ROWS = 128
TOK_TILE = 512
FFN_TILE = 256
MOE_TILE = 512
MOE_F_TILE = 1792
COMBINE_TILE = 512
COPY_UNROLL = 8
NEG_BIG = -1e30


def _params(*sem):
    return pltpu.CompilerParams(dimension_semantics=sem, vmem_limit_bytes=VMEM_LIMIT)


def _const(shape):
    nd = len(shape)
    return pl.BlockSpec(shape, lambda *_: (0,) * nd, pipeline_mode=pl.Buffered(1))


def _rms(x, g):
    return x * lax.rsqrt(jnp.mean(x * x, axis=-1, keepdims=True) + EPS) * g


def _dot(a, w):
    return jnp.dot(a.astype(BF16), w, preferred_element_type=F32)


def _sigmoid(x):
    return 1.0 / (1.0 + jnp.exp(-x))


def _silu(x):
    return x * _sigmoid(x)


def _layer_norm(x, g, b):
    mu = jnp.mean(x, axis=-1, keepdims=True)
    xc = x - mu
    var = jnp.mean(xc * xc, axis=-1, keepdims=True)
    return xc * lax.rsqrt(var + EPS) * g + b


def _head_rms(x, g):
    t, w = x.shape
    lane = lax.broadcasted_iota(jnp.int32, (t, LANES), 1)
    lo = lane < HEAD_DIM
    outs = []
    for c in range(w // LANES):
        blk = x[:, c * LANES:(c + 1) * LANES]
        sq = blk * blk
        s_lo = jnp.sum(jnp.where(lo, sq, 0.0), axis=-1, keepdims=True)
        s_hi = jnp.sum(jnp.where(lo, 0.0, sq), axis=-1, keepdims=True)
        inv = jnp.where(lo, lax.rsqrt(s_lo / HEAD_DIM + EPS), lax.rsqrt(s_hi / HEAD_DIM + EPS))
        outs.append(blk * inv)
    return jnp.concatenate(outs, axis=1) * g


def _rows_to_tiles(ref, x):
    t = x.shape[0]
    for s in range(ROW_TILES):
        ref[pl.ds(s, t, stride=ROW_TILES), :] = x[:, s * LANES:(s + 1) * LANES]


def _tiles_to_rows(ref, t):
    return jnp.concatenate([ref[pl.ds(s, t, stride=ROW_TILES), :] for s in range(ROW_TILES)], axis=1)


def _conv_taps(uext_ref, wdw_ref, bdw_ref, c_ref, tt):
    win_rows = CONV_ROWS + CONV_PAD

    def chunk(ci, carry):
        r0 = pl.multiple_of(ci * CONV_ROWS, CONV_ROWS)
        for lt in range(ROW_TILES):
            cols = slice(lt * LANES, (lt + 1) * LANES)
            win = uext_ref[pl.ds(r0, win_rows), cols]
            acc = jnp.broadcast_to(bdw_ref[:, cols], (CONV_ROWS, LANES))
            for b in range(SUBLANES):
                wb = win if b == 0 else pltpu.roll(win, win_rows - b, axis=0)
                for a in range(CONV_PAD // SUBLANES + 1):
                    j = a * SUBLANES + b - (CONV_PAD - CONV_CTX)
                    if 0 <= j < CONV_WIDTH:
                        acc = acc + wdw_ref[j:j + 1, cols] * wb[a * SUBLANES:a * SUBLANES + CONV_ROWS]
            c_ref[pl.ds(r0, CONV_ROWS), cols] = acc
        return carry
    lax.fori_loop(0, tt // CONV_ROWS, chunk, 0)


def _conv_prompt_kernel(x_ref, nm_ref, w1_ref, b1_ref, wdw_ref, bdw_ref, lng_ref, lnb_ref, w2_ref, b2_ref,
                        h_ref, st_ref, uext_ref, c_ref, *, tt):
    t = pl.program_id(1)

    @pl.when(t == 0)
    def _():
        uext_ref[0:CONV_PAD, :] = jnp.zeros((CONV_PAD, D_MODEL), F32)

    @pl.when(t > 0)
    def _():
        uext_ref[0:CONV_PAD, :] = uext_ref[tt:tt + CONV_PAD, :]

    x = x_ref[0]
    z = _dot(_rms(x, nm_ref[...]), w1_ref[...]) + b1_ref[...]
    u = z[:, :D_MODEL] * _sigmoid(z[:, D_MODEL:])
    uext_ref[CONV_PAD:CONV_PAD + tt, :] = u
    _conv_taps(uext_ref, wdw_ref, bdw_ref, c_ref, tt)
    c = _silu(_layer_norm(c_ref[...], lng_ref[...], lnb_ref[...]))
    h_ref[0] = x + _dot(c, w2_ref[...]) + b2_ref[...]
    st_ref[0, 0] = uext_ref[CONV_PAD + tt - CONV_CTX:CONV_PAD + tt, :]


def _conv_prompt(x, nm, w1, b1, wdw, bdw, lng, lnb, w2, b2):
    n, t, d = x.shape
    tt = TOK_TILE
    weights = (nm, w1, b1, wdw, bdw, lng, lnb, w2, b2)
    return pl.pallas_call(
        functools.partial(_conv_prompt_kernel, tt=tt),
        grid=(n, t // tt),
        in_specs=[pl.BlockSpec((1, tt, d), lambda b, i: (b, i, 0))] + [_const(w.shape) for w in weights],
        out_specs=[pl.BlockSpec((1, tt, d), lambda b, i: (b, i, 0)),
                   pl.BlockSpec((1, 1, CONV_CTX, d), lambda b, i: (0, b, 0, 0))],
        out_shape=[jax.ShapeDtypeStruct((n, t, d), F32), jax.ShapeDtypeStruct((1, n, CONV_CTX, d), F32)],
        scratch_shapes=[pltpu.VMEM((CONV_PAD + tt, d), F32), pltpu.VMEM((tt, d), F32)],
        compiler_params=_params("arbitrary", "arbitrary"),
        name="conv_prompt",
    )(x, *weights)


def _conv_sample_kernel(x_ref, past_ref, nm_ref, w1_ref, b1_ref, wdw_ref, bdw_ref, lng_ref, lnb_ref,
                        w2_ref, b2_ref, h_ref, st_ref):
    x = x_ref[...]
    z = _dot(_rms(x, nm_ref[...]), w1_ref[...]) + b1_ref[...]
    u = z[:, :D_MODEL] * _sigmoid(z[:, D_MODEL:])
    c = wdw_ref[CONV_CTX:CONV_WIDTH, :] * u + bdw_ref[...]
    for j in range(CONV_CTX):
        c = c + wdw_ref[j:j + 1, :] * past_ref[0, j]
        if j > 0:
            st_ref[0, j - 1] = past_ref[0, j]
    st_ref[0, CONV_CTX - 1] = u
    c = _silu(_layer_norm(c, lng_ref[...], lnb_ref[...]))
    h_ref[...] = x + _dot(c, w2_ref[...]) + b2_ref[...]


def _conv_sample(x, past, nm, w1, b1, wdw, bdw, lng, lnb, w2, b2):
    n, d = x.shape
    nb = 16
    weights = (nm, w1, b1, wdw, bdw, lng, lnb, w2, b2)
    state = pl.BlockSpec((1, CONV_CTX, nb, d), lambda i: (0, 0, i, 0))
    return pl.pallas_call(
        _conv_sample_kernel,
        grid=(n // nb,),
        in_specs=[pl.BlockSpec((nb, d), lambda i: (i, 0)), state] + [_const(w.shape) for w in weights],
        out_specs=[pl.BlockSpec((nb, d), lambda i: (i, 0)), state],
        out_shape=[jax.ShapeDtypeStruct((n, d), F32), jax.ShapeDtypeStruct((1, CONV_CTX, n, d), F32)],
        compiler_params=_params("arbitrary"),
        name="conv_sample",
    )(x, past, *weights)


def _ffn_kernel(h_ref, g_ref, wg_ref, wu_ref, wd_ref, *refs):
    n_cast = len(refs) // 2
    cast_in, o_ref, cast_out = refs[:n_cast], refs[n_cast], refs[n_cast + 1:]
    h = h_ref[...]
    hn = _rms(h, g_ref[...]).astype(BF16)
    a = jnp.dot(hn, wg_ref[...], preferred_element_type=F32)
    b = jnp.dot(hn, wu_ref[...], preferred_element_type=F32)
    o_ref[...] = h + _dot(_silu(a) * b, wd_ref[...])
    for src, dst in zip(cast_in, cast_out):
        dst[...] = src[...].astype(BF16)


def _ffn(h, g, wg, wu, wd, cast=()):
    m, d = h.shape
    tm = min(FFN_TILE if cast else TOK_TILE, m)
    steps = m // tm
    weights = (g, wg, wu, wd)
    cast_specs = [pl.BlockSpec((c.shape[0] // steps, c.shape[1]), lambda i: (i, 0)) for c in cast]
    out = pl.pallas_call(
        _ffn_kernel,
        grid=(steps,),
        in_specs=[pl.BlockSpec((tm, d), lambda i: (i, 0))] + [_const(w.shape) for w in weights] + cast_specs,
        out_specs=[pl.BlockSpec((tm, d), lambda i: (i, 0))] + cast_specs,
        out_shape=[jax.ShapeDtypeStruct((m, d), F32)] + [jax.ShapeDtypeStruct(c.shape, BF16) for c in cast],
        compiler_params=_params("arbitrary"),
        name="dense_ffn",
    )(h, *weights, *cast)
    return out if cast else out[0]


def _ple(h, p, g, w_gate, w_proj):
    return h + _sigmoid(_dot(_rms(h, g), w_gate)) * _dot(p, w_proj)


def _ple_kvq_kernel(h_ref, p_ref, gple_ref, wpg_ref, wpp_ref, gkv_ref, wk_ref, wv_ref, kn_ref,
                    gmix_ref, wq_ref, qn_ref, h3_ref, k_ref, v_ref, q_ref):
    h3 = _ple(h_ref[...], p_ref[...], gple_ref[...], wpg_ref[...], wpp_ref[...])
    h3_ref[...] = h3
    s = _rms(h3, gkv_ref[...]).astype(BF16)
    k_ref[...] = _head_rms(jnp.dot(s, wk_ref[...], preferred_element_type=F32), kn_ref[...])
    v_ref[...] = jnp.dot(s, wv_ref[...], preferred_element_type=F32)
    q = _dot(_rms(h3, gmix_ref[...]), wq_ref[...])
    q_ref[...] = (_head_rms(q, qn_ref[...]) * (HEAD_DIM ** -0.5)).astype(BF16)


def _ple_kvq(h, p, gple, wpg, wpp, gkv, wk, wv, kn, gmix, wq, qn):
    m, d = h.shape
    tm = min(TOK_TILE, m)
    weights = (gple, wpg, wpp, gkv, wk, wv, kn, gmix, wq, qn)
    row = lambda w: pl.BlockSpec((tm, w), lambda i: (i, 0))
    return pl.pallas_call(
        _ple_kvq_kernel,
        grid=(m // tm,),
        in_specs=[row(d), row(PLE_DIM)] + [_const(w.shape) for w in weights],
        out_specs=[row(d), row(KV_DIM), row(KV_DIM), row(d)],
        out_shape=[jax.ShapeDtypeStruct((m, d), F32), jax.ShapeDtypeStruct((m, KV_DIM), F32),
                   jax.ShapeDtypeStruct((m, KV_DIM), F32), jax.ShapeDtypeStruct((m, d), BF16)],
        compiler_params=_params("arbitrary"),
        name="ple_kv_q",
    )(h, p, *weights)


def _softmax_sink(s, sink):
    m = jnp.maximum(jnp.max(s, axis=-1, keepdims=True), sink)
    e = jnp.exp(s - m)
    den = jnp.sum(e, axis=-1, keepdims=True) + jnp.exp(sink - m)
    return e / den


def _attn_prompt_kernel(q_ref, kp_ref, kc_ref, vp_ref, vc_ref, sink_ref, o_ref):
    n = pl.program_id(1)
    q = q_ref[0]
    k = jnp.concatenate([kp_ref[0], kc_ref[0]], axis=0).astype(BF16)
    v = jnp.concatenate([vp_ref[0], vc_ref[0]], axis=0).astype(BF16)
    qi = lax.broadcasted_iota(jnp.int32, (WINDOW, 2 * WINDOW), 0)
    kj = lax.broadcasted_iota(jnp.int32, (WINDOW, 2 * WINDOW), 1)
    dist = qi + WINDOW - kj
    first_key = jnp.where(n > 0, 0, WINDOW)
    mask = (dist >= 0) & (dist <= WINDOW) & (kj >= first_key)
    outs = []
    for h in range(N_HEADS):
        kh = h % N_KV_HEADS
        qh = q[:, h * HEAD_DIM:(h + 1) * HEAD_DIM]
        kk = k[:, kh * HEAD_DIM:(kh + 1) * HEAD_DIM]
        vv = v[:, kh * HEAD_DIM:(kh + 1) * HEAD_DIM]
        s = lax.dot_general(qh, kk, (((1,), (1,)), ((), ())), preferred_element_type=F32)
        s = jnp.where(mask, s, -jnp.inf)
        p = _softmax_sink(s, sink_ref[h])
        outs.append(jnp.dot(p.astype(BF16), vv, preferred_element_type=F32))
    o_ref[0] = jnp.concatenate(outs, axis=1).astype(BF16)


def _attn_prompt(q, k, v, sinks):
    n, t, d = q.shape
    nb = t // WINDOW
    cur = lambda w: pl.BlockSpec((1, WINDOW, w), lambda b, i: (b, i, 0))
    prev = lambda w: pl.BlockSpec((1, WINDOW, w), lambda b, i: (b, jnp.maximum(i - 1, 0), 0))
    return pl.pallas_call(
        _attn_prompt_kernel,
        grid=(n, nb),
        in_specs=[cur(d), prev(KV_DIM), cur(KV_DIM), prev(KV_DIM), cur(KV_DIM),
                  pl.BlockSpec(memory_space=pltpu.SMEM)],
        out_specs=cur(d),
        out_shape=jax.ShapeDtypeStruct((n, t, d), BF16),
        compiler_params=_params("arbitrary", "arbitrary"),
        name="attn_prompt",
    )(q, k, k, v, v, sinks)


def _attn_sample_kernel(q_ref, kc_ref, vc_ref, kn_ref, vn_ref, sink_ref, o_ref, ks_ref, vs_ref, *, nb):
    lane_head = lax.broadcasted_iota(jnp.int32, (N_KV_HEADS, KV_DIM), 1) // HEAD_DIM
    own = lane_head == lax.broadcasted_iota(jnp.int32, (N_KV_HEADS, KV_DIM), 0)
    sink = sink_ref[...]
    rows = []
    for i in range(nb):
        kc = kc_ref[i].T
        vc = vc_ref[i].T
        kn = kn_ref[i:i + 1, :]
        vn = vn_ref[i:i + 1, :]
        ks_ref[i, 0:WINDOW - 1, :] = kc[1:WINDOW, :]
        ks_ref[i, WINDOW - 1:WINDOW, :] = kn
        vs_ref[i, 0:WINDOW - 1, :] = vc[1:WINDOW, :]
        vs_ref[i, WINDOW - 1:WINDOW, :] = vn
        qrow = q_ref[i:i + 1, :].astype(F32)
        qx = jnp.concatenate(
            [jnp.where(own, jnp.broadcast_to(qrow[:, g * KV_DIM:(g + 1) * KV_DIM], (N_KV_HEADS, KV_DIM)), 0.0)
             for g in range(GROUP)], axis=0)
        s_c = lax.dot_general(qx.astype(BF16), kc.astype(BF16), (((1,), (1,)), ((), ())),
                              preferred_element_type=F32)
        s_n = jnp.sum(qx * kn.astype(BF16).astype(F32), axis=-1, keepdims=True)
        m = jnp.maximum(jnp.maximum(jnp.max(s_c, axis=-1, keepdims=True), s_n), sink)
        e_c = jnp.exp(s_c - m)
        e_n = jnp.exp(s_n - m)
        den = jnp.sum(e_c, axis=-1, keepdims=True) + e_n + jnp.exp(sink - m)
        ox = jnp.dot((e_c / den).astype(BF16), vc.astype(BF16), preferred_element_type=F32)
        ox = ox + (e_n / den).astype(BF16).astype(F32) * vn.astype(BF16).astype(F32)
        rows.append(jnp.concatenate(
            [jnp.sum(jnp.where(own, ox[g * N_KV_HEADS:(g + 1) * N_KV_HEADS], 0.0), axis=0, keepdims=True)
             for g in range(GROUP)], axis=1))
    o_ref[...] = jnp.concatenate(rows, axis=0).astype(BF16)


def _attn_sample(q, kc, vc, kn, vn, sinks_col):
    n, d = q.shape
    nb = 16
    row = lambda w: pl.BlockSpec((nb, w), lambda i: (i, 0))
    cache_in = pl.BlockSpec((nb, KV_DIM, WINDOW), lambda i: (i, 0, 0))
    cache = pl.BlockSpec((nb, WINDOW, KV_DIM), lambda i: (i, 0, 0))
    return pl.pallas_call(
        functools.partial(_attn_sample_kernel, nb=nb),
        grid=(n // nb,),
        in_specs=[row(d), cache_in, cache_in, row(KV_DIM), row(KV_DIM), _const(sinks_col.shape)],
        out_specs=[row(d), cache, cache],
        out_shape=[jax.ShapeDtypeStruct((n, d), BF16),
                   jax.ShapeDtypeStruct((n, WINDOW, KV_DIM), F32),
                   jax.ShapeDtypeStruct((n, WINDOW, KV_DIM), F32)],
        compiler_params=_params("arbitrary"),
        name="attn_sample",
    )(q, kc, vc, kn, vn, sinks_col)


def _oproj_router_kernel(rows_in_ref, o_ref, h_ref, wo_ref, g_ref, wr_ref, br_ref,
                         h4_ref, hn_ref, route_ref, sort_ref, cnt_ref, carry_ref, *, tm):
    del rows_in_ref
    i = pl.program_id(0)

    @pl.when(i == 0)
    def _():
        carry_ref[...] = jnp.zeros((1, LANES), F32)

    h4 = h_ref[...] + jnp.dot(o_ref[...], wo_ref[...], preferred_element_type=F32)
    h4_ref[...] = h4
    hn = _rms(h4, g_ref[...])
    _rows_to_tiles(hn_ref, hn)
    hn_hi = hn.astype(BF16)
    hn_lo = (hn - hn_hi.astype(F32)).astype(BF16)
    part_hi = jnp.dot(hn_hi, wr_ref[...], preferred_element_type=F32)
    part_lo = jnp.dot(hn_lo, wr_ref[...], preferred_element_type=F32)
    logits = (part_hi[:, :LANES] + (part_hi[:, LANES:] + part_lo[:, :LANES]) + part_lo[:, LANES:]) + br_ref[...]
    lane = lax.broadcasted_iota(jnp.int32, (tm, LANES), 1).astype(F32)
    m1 = jnp.max(logits, axis=-1, keepdims=True)
    i1 = jnp.min(jnp.where(logits == m1, lane, float(LANES)), axis=-1, keepdims=True)
    rest = jnp.where(lane == i1, -jnp.inf, logits)
    m2 = jnp.max(rest, axis=-1, keepdims=True)
    i2 = jnp.min(jnp.where(rest == m2, lane, float(LANES)), axis=-1, keepdims=True)
    e2 = jnp.exp(m2 - m1)
    w1 = 1.0 / (1.0 + e2)
    w2 = e2 / (1.0 + e2)
    oh1 = lane == i1
    oh2 = lane == i2
    sel = jnp.where(oh1, 1.0, 0.0) + jnp.where(oh2, 1.0, 0.0)
    row = lax.broadcasted_iota(jnp.int32, (tm, tm), 0)
    col = lax.broadcasted_iota(jnp.int32, (tm, tm), 1)
    tri = jnp.where(row > col, 1.0, 0.0).astype(BF16)
    before = jnp.dot(tri, sel.astype(BF16), preferred_element_type=F32) + carry_ref[...]
    r1 = jnp.sum(jnp.where(oh1, before, 0.0), axis=-1, keepdims=True)
    r2 = jnp.sum(jnp.where(oh2, before, 0.0), axis=-1, keepdims=True)
    carry_ref[...] = carry_ref[...] + jnp.sum(sel, axis=0, keepdims=True)
    cnt_ref[...] = carry_ref[...]
    fields = (i1, i2, r1, r2, w1, w2)
    route = jnp.zeros((tm, LANES), F32)
    for f, val in enumerate(fields):
        route = jnp.where(lane == float(f), val, route)
    route_ref[...] = route
    sort_ref[...] = route.T[0:SUBLANES, :]


def _oproj_router(rows_buf, o, h, wo, g, wr, br, *, first_token):
    m, d = h.shape
    tm = min(TOK_TILE, m)
    weights = (wo, g, wr, br)
    row = lambda w: pl.BlockSpec((tm, w), lambda i: (i, 0))
    first_block = first_token // tm
    return pl.pallas_call(
        functools.partial(_oproj_router_kernel, tm=tm),
        grid=(m // tm,),
        in_specs=[pl.BlockSpec(memory_space=pl.ANY), row(d), row(d)] + [_const(w.shape) for w in weights],
        out_specs=[row(d), pl.BlockSpec((tm * ROW_TILES, LANES), lambda i: (i + first_block, 0)), row(LANES),
                   pl.BlockSpec((SUBLANES, tm), lambda i: (0, i)), pl.BlockSpec((1, LANES), lambda i: (0, 0))],
        out_shape=[jax.ShapeDtypeStruct((m, d), F32), jax.ShapeDtypeStruct(rows_buf.shape, F32),
                   jax.ShapeDtypeStruct((m, LANES), F32), jax.ShapeDtypeStruct((SUBLANES, m), F32),
                   jax.ShapeDtypeStruct((1, LANES), F32)],
        scratch_shapes=[pltpu.VMEM((1, LANES), F32)],
        input_output_aliases={0: 1},
        compiler_params=_params("arbitrary"),
        name="oproj_router",
    )(rows_buf, o, h, *weights)


def _row(ref, r):
    return ref.at[pl.ds(pl.multiple_of(r * ROW_TILES, ROW_TILES), ROW_TILES), :]


def _for_rows(lo, hi, fn):
    assert (hi - lo) % COPY_UNROLL == 0

    def body(j, carry):
        for u in range(COPY_UNROLL):
            fn(lo + j * COPY_UNROLL + u)
        return carry
    lax.fori_loop(0, (hi - lo) // COPY_UNROLL, body, 0)


def _moe_kernel(te_ref, nv_ref, nr_ref, pos_ref, rows_ref, wg_ref, wu_ref, wd_ref, y_ref,
                inv_ref, xg_ref, yb_ref, xb_ref, acc_ref, gsem, ssem, isem, *, tm, nf, nt, n_pairs, pair_regions):
    i = pl.program_id(0)
    f = pl.program_id(1)
    nv = nv_ref[0]
    valid = i < nv
    slot = i % 2
    assert nf == 2

    def gather_row(tile, r, to_slot):
        pair = inv_ref[tile * tm + r]
        token = pair
        for first_pair, _, first_token in pair_regions[1:]:
            token = jnp.where(pair >= first_pair, pair - (first_pair - first_token), token)
        pltpu.make_async_copy(_row(rows_ref, token), _row(xg_ref.at[to_slot], r), gsem.at[to_slot]).start()

    def scatter_row(tile, real, r, from_slot):
        pair = jnp.where(r < real, inv_ref[tile * tm + r], n_pairs + from_slot * tm + r)
        pltpu.make_async_copy(_row(yb_ref.at[from_slot], r), _row(y_ref, pair), ssem.at[from_slot]).start()

    def wait_gather(at_slot):
        pltpu.make_async_copy(rows_ref.at[pl.ds(0, tm * ROW_TILES), :], xg_ref.at[at_slot], gsem.at[at_slot]).wait()

    def wait_scatter(at_slot):
        pltpu.make_async_copy(yb_ref.at[at_slot], y_ref.at[pl.ds(0, tm * ROW_TILES), :], ssem.at[at_slot]).wait()

    @pl.when(jnp.logical_and(i == 0, f == 0))
    def _():
        def place(p):
            inv_ref[pos_ref[p]] = p
        _for_rows(0, n_pairs, place)

        def clear_padding(t, carry):
            def clear(r, c):
                inv_ref[t * tm + r] = 0
                return c
            return lax.fori_loop(nr_ref[t], tm, clear, carry)
        lax.fori_loop(0, nv, clear_padding, 0)
        yb_ref[...] = jnp.zeros(yb_ref.shape, F32)
        spare = y_ref.at[pl.ds(n_pairs * ROW_TILES, 2 * tm * ROW_TILES), :]
        for s in range(2):
            half = spare.at[pl.ds(s * tm * ROW_TILES, tm * ROW_TILES), :]
            pltpu.make_async_copy(yb_ref.at[s], half, isem).start()
            pltpu.make_async_copy(yb_ref.at[s], half, isem).wait()
        _for_rows(0, tm, lambda r: gather_row(0, r, 0))

    @pl.when(jnp.logical_and(valid, f == 0))
    def _():
        wait_gather(slot)
        xb_ref[...] = _tiles_to_rows(xg_ref.at[slot], tm).astype(BF16)
        acc_ref[...] = jnp.zeros((tm, D_MODEL), F32)

    def matmuls(first):
        if first:
            nxt = jnp.minimum(i + 1, nv - 1)
            for r in range(tm):
                gather_row(nxt, r, 1 - slot)
        else:
            prev = jnp.maximum(i - 1, 0)
            real = jnp.where(i >= 1, nr_ref[prev], 0)
            for r in range(tm):
                scatter_row(prev, real, r, 1 - slot)
        x = xb_ref[...]
        a = jnp.dot(x, wg_ref[0], preferred_element_type=F32)
        b = jnp.dot(x, wu_ref[0], preferred_element_type=F32)
        acc_ref[...] += _dot(_silu(a) * b, wd_ref[0])

    @pl.when(jnp.logical_and(valid, f == 0))
    def _():
        matmuls(True)

    @pl.when(jnp.logical_and(valid, f > 0))
    def _():
        matmuls(False)

    @pl.when(jnp.logical_and(valid, f == nf - 1))
    def _():
        @pl.when(i >= 1)
        def _():
            wait_scatter(slot)
        _rows_to_tiles(yb_ref.at[slot], acc_ref[...])

        @pl.when(i == nv - 1)
        def _():
            _for_rows(0, tm, lambda r: scatter_row(i, nr_ref[i], r, slot))

    @pl.when(jnp.logical_and(i == nt - 1, f == nf - 1))
    def _():
        wait_gather(nv % 2)
        wait_scatter(nv % 2)
        wait_scatter((nv - 1) % 2)


def _moe(tile_expert, n_valid, tile_rows, pos, pair_regions, rows, wg, wu, wd):
    tm, tf = MOE_TILE, MOE_F_TILE
    nt, nf = tile_expert.shape[0], D_EXPERT // tf
    n_pairs = pos.shape[0]

    def fidx(i, f, nv):
        return jnp.where(i < nv[0], f, nf - 1)
    tile_buf = pltpu.VMEM((2, tm * ROW_TILES, LANES), F32)
    return pl.pallas_call(
        functools.partial(_moe_kernel, tm=tm, nf=nf, nt=nt, n_pairs=n_pairs, pair_regions=pair_regions),
        grid_spec=pltpu.PrefetchScalarGridSpec(
            num_scalar_prefetch=4,
            grid=(nt, nf),
            in_specs=[pl.BlockSpec(memory_space=pl.ANY),
                      pl.BlockSpec((1, D_MODEL, tf), lambda i, f, te, nv, *_: (te[i], 0, fidx(i, f, nv))),
                      pl.BlockSpec((1, D_MODEL, tf), lambda i, f, te, nv, *_: (te[i], 0, fidx(i, f, nv))),
                      pl.BlockSpec((1, tf, D_MODEL), lambda i, f, te, nv, *_: (te[i], fidx(i, f, nv), 0))],
            out_specs=pl.BlockSpec(memory_space=pl.ANY),
            scratch_shapes=[pltpu.SMEM((nt * tm,), jnp.int32), tile_buf, tile_buf,
                            pltpu.VMEM((tm, D_MODEL), BF16), pltpu.VMEM((tm, D_MODEL), F32),
                            pltpu.SemaphoreType.DMA((2,)), pltpu.SemaphoreType.DMA((2,)),
                            pltpu.SemaphoreType.DMA(())],
        ),
        out_shape=jax.ShapeDtypeStruct(((n_pairs + 2 * tm) * ROW_TILES, LANES), F32),
        compiler_params=_params("arbitrary", "arbitrary"),
        name="moe_grouped",
    )(tile_expert, n_valid, tile_rows, pos, rows, wg, wu, wd)


def _combine_kernel(y1_ref, y2_ref, h_ref, route_ref, p_ref, g_ref, wpg_ref, wpp_ref, o_ref, *, tb):
    route = route_ref[...]
    w1 = route[:, 4:5]
    w2 = route[:, 5:6]
    h5 = h_ref[...] + (w1 * _tiles_to_rows(y1_ref, tb) + w2 * _tiles_to_rows(y2_ref, tb))
    o_ref[...] = _ple(h5, p_ref[...], g_ref[...], wpg_ref[...], wpp_ref[...])


def _combine(y_pairs, h, route, p, g, wpg, wpp, *, first_pair):
    m, d = h.shape
    tb = min(COMBINE_TILE, m)
    weights = (g, wpg, wpp)
    row = lambda w: pl.BlockSpec((tb, w), lambda i: (i, 0))
    steps = m // tb
    first_block = first_pair // tb
    pair_rows = lambda slot: pl.BlockSpec((tb * ROW_TILES, LANES), lambda i: (i + first_block + slot * steps, 0))
    return pl.pallas_call(
        functools.partial(_combine_kernel, tb=tb),
        grid=(steps,),
        in_specs=[pair_rows(0), pair_rows(1), row(d), row(LANES),
                  pl.BlockSpec((tb, PLE_DIM), lambda i: (i + steps, 0))] + [_const(w.shape) for w in weights],
        out_specs=row(d),
        out_shape=jax.ShapeDtypeStruct((m, d), F32),
        compiler_params=_params("arbitrary"),
        name="moe_combine",
    )(y_pairs, y_pairs, h, route, p, *weights)


def _row2(v):
    return v.reshape(1, -1).astype(F32)


def kernel(x_prompt, x_sample, state_conv, cache_k, cache_v, p_prompt, p_sample, norm_mix, norm_ffn, norm_ple,
           conv_w_pw1, conv_b_pw1, conv_w_dw, conv_b_dw, conv_ln_g, conv_ln_b, conv_w_pw2, conv_b_pw2,
           norm_kv, w_k, w_v, k_norm, w_q, q_norm, sinks, w_o, ffn_w_gate, ffn_w_up, ffn_w_down,
           moe_w_router, moe_b_router, moe_w_gate, moe_w_up, moe_w_down, ple_w_gate, ple_w_proj):
    nb, t, d = x_prompt.shape
    ns = x_sample.shape[0]
    mp = nb * t
    bf = lambda w: w.astype(BF16)

    conv_w = (_row2(norm_mix[0]), bf(conv_w_pw1[0]), _row2(conv_b_pw1[0]), conv_w_dw[0], _row2(conv_b_dw[0]),
              _row2(conv_ln_g[0]), _row2(conv_ln_b[0]), bf(conv_w_pw2[0]), _row2(conv_b_pw2[0]))
    ffn_w = (_row2(norm_ffn[0]), bf(ffn_w_gate[0]), bf(ffn_w_up[0]), bf(ffn_w_down[0]))
    wq = w_q[0].reshape(d, N_KV_HEADS, GROUP, HEAD_DIM).transpose(0, 2, 1, 3).reshape(d, d)
    wo = w_o[0].reshape(N_KV_HEADS, GROUP, HEAD_DIM, d).transpose(1, 0, 2, 3).reshape(d, d)
    sinks_gk = sinks[0].astype(F32).reshape(N_KV_HEADS, GROUP).T.reshape(N_HEADS)
    kvq_w = (_row2(norm_ple[0]), bf(ple_w_gate[0]), bf(ple_w_proj[0]), _row2(norm_kv), bf(w_k), bf(w_v),
             _row2(jnp.tile(k_norm, N_KV_HEADS)), _row2(norm_mix[1]), bf(wq), _row2(jnp.tile(q_norm[0], N_HEADS)))
    wr = jnp.zeros((d, LANES), F32).at[:, :N_EXPERTS].set(moe_w_router[0])
    wr_hi = bf(wr)
    wr_lo = bf(wr - wr_hi.astype(F32))
    br = jnp.full((1, LANES), NEG_BIG, F32).at[0, :N_EXPERTS].set(moe_b_router[0])
    router_w = (bf(wo), _row2(norm_ffn[1]), jnp.concatenate([wr_hi, wr_lo], axis=1), br)
    ple1_w = (_row2(norm_ple[1]), bf(ple_w_gate[1]), bf(ple_w_proj[1]))

    h1p, conv_p = _conv_prompt(x_prompt, *conv_w)
    pp = p_prompt.reshape(-1, PLE_DIM)
    ps = p_sample.reshape(-1, PLE_DIM)
    expert_w = (moe_w_gate[0], moe_w_up[0], moe_w_down[0])
    h2p, *expert_bf = _ffn(h1p.reshape(mp, d), *ffn_w, cast=tuple(w.reshape(-1, w.shape[-1]) for w in expert_w))
    expert_bf = [b.reshape(w.shape) for b, w in zip(expert_bf, expert_w)]
    h3p, kp, vp, qp = _ple_kvq(h2p, pp, *kvq_w)
    h1s, conv_s = _conv_sample(x_sample.reshape(ns, d), state_conv.transpose(0, 2, 1, 3), *conv_w)
    conv_s = conv_s.transpose(0, 2, 1, 3)
    h3s, ks, vs, qs = _ple_kvq(_ffn(h1s, *ffn_w), ps, *kvq_w)

    kp3 = kp.reshape(nb, t, KV_DIM)
    vp3 = vp.reshape(nb, t, KV_DIM)
    op = _attn_prompt(qp.reshape(nb, t, d), kp3, vp3, sinks_gk)
    feature_major = lambda c: c.transpose(0, 2, 3, 1).reshape(ns, KV_DIM, WINDOW)
    os_, k_state_s, v_state_s = _attn_sample(qs, feature_major(cache_k), feature_major(cache_v), ks, vs,
                                             sinks_gk.reshape(N_HEADS, 1))

    mt = mp + ns
    rows = jnp.zeros((mt * ROW_TILES, LANES), F32)
    h4p, rows, route_p, sort_p, cnt_p = _oproj_router(rows, op.reshape(mp, d), h3p, *router_w, first_token=0)
    h4s, rows, route_s, sort_s, cnt_s = _oproj_router(rows, os_, h3s, *router_w, first_token=mp)

    cnt_p = cnt_p[0, :N_EXPERTS].astype(jnp.int32)
    cnt_s = cnt_s[0, :N_EXPERTS].astype(jnp.int32)
    group = (cnt_p + cnt_s + MOE_TILE - 1) // MOE_TILE * MOE_TILE
    ends = jnp.cumsum(group)
    starts = ends - group
    n_tiles = (2 * (mp + ns) + N_EXPERTS * (MOE_TILE - 1)) // MOE_TILE
    n_valid = (ends[-1] // MOE_TILE).astype(jnp.int32)
    tile_start = jnp.minimum(jnp.arange(n_tiles, dtype=jnp.int32), n_valid - 1) * MOE_TILE
    tile_expert = jnp.sum(tile_start[:, None] >= ends[None, :], axis=1).astype(jnp.int32)

    def positions(sort_info, slot, base):
        idx = sort_info[slot].astype(jnp.int32)
        rank = sort_info[2 + slot].astype(jnp.int32)
        return (starts + base)[idx] + rank
    no_base = jnp.zeros_like(cnt_p)
    pos = jnp.concatenate([positions(sort_p, 0, no_base), positions(sort_p, 1, no_base),
                           positions(sort_s, 0, cnt_p), positions(sort_s, 1, cnt_p)]).astype(jnp.int32)
    pair_regions = ((0, mp, 0), (mp, mp, 0), (2 * mp, ns, mp), (2 * mp + ns, ns, mp))
    tile_ids = jnp.arange(n_tiles, dtype=jnp.int32)
    rows_left = (starts + cnt_p + cnt_s)[tile_expert] - tile_ids * MOE_TILE
    tile_rows = jnp.where(tile_ids < n_valid, jnp.clip(rows_left, 0, MOE_TILE), 0).astype(jnp.int32)
    y_pairs = _moe(tile_expert, n_valid.reshape(1), tile_rows, pos, pair_regions, rows,
                   *expert_bf)

    y_p = _combine(y_pairs, h4p, route_p, pp, *ple1_w, first_pair=0)
    y_s = _combine(y_pairs, h4s, route_s, ps, *ple1_w, first_pair=2 * mp)

    kv4 = lambda a, n: a.reshape(n, WINDOW, N_KV_HEADS, HEAD_DIM)
    return (y_p.reshape(nb, t, d), y_s.reshape(ns, 1, d), conv_p,
            kv4(kp3[:, t - WINDOW:], nb), kv4(vp3[:, t - WINDOW:], nb),
            conv_s, kv4(k_state_s, ns), kv4(v_state_s, ns))
```

```python
import functools

import jax
import jax.numpy as jnp
from jax import lax
from jax.experimental import pallas as pl
from jax.experimental.pallas import tpu as pltpu

D_MODEL = 1024
CONV_WIDTH = 31
CONV_CTX = CONV_WIDTH - 1
HEAD_DIM = 64
N_HEADS = 16
N_KV_HEADS = 4
GROUP = N_HEADS // N_KV_HEADS
KV_DIM = N_KV_HEADS * HEAD_DIM
WINDOW = 128
D_FF = 2816
N_EXPERTS = 8
D_EXPERT = 3584
PLE_DIM = 256
EPS = 1e-6

BF16 = jnp.bfloat16
F32 = jnp.float32

LANES = 128
SUBLANES = 8
BF16_ROWS = 16
ROW_TILES = D_MODEL // LANES
VMEM_LIMIT = 58 * 1024 * 1024

CONV_PAD = 32
CONV_ROWS = 128
TOK_TILE = 512
MOE_TILE = 512
MOE_F_TILE = 1792
COMBINE_TILE = 512
COPY_UNROLL = 8
NEG_BIG = -1e30


def _params(*sem):
    return pltpu.CompilerParams(dimension_semantics=sem, vmem_limit_bytes=VMEM_LIMIT)


def _const(shape):
    nd = len(shape)
    return pl.BlockSpec(shape, lambda *_: (0,) * nd, pipeline_mode=pl.Buffered(1))


def _rms(x, g):
    return x * lax.rsqrt(jnp.mean(x * x, axis=-1, keepdims=True) + EPS) * g


def _dot(a, w):
    return jnp.dot(a.astype(BF16), w, preferred_element_type=F32)


def _sigmoid(x):
    return 1.0 / (1.0 + jnp.exp(-x))


def _silu(x):
    return x * _sigmoid(x)


def _layer_norm(x, g, b):
    mu = jnp.mean(x, axis=-1, keepdims=True)
    xc = x - mu
    var = jnp.mean(xc * xc, axis=-1, keepdims=True)
    return xc * lax.rsqrt(var + EPS) * g + b


def _head_rms(x, g):
    t, w = x.shape
    lane = lax.broadcasted_iota(jnp.int32, (t, LANES), 1)
    lo = lane < HEAD_DIM
    outs = []
    for c in range(w // LANES):
        blk = x[:, c * LANES:(c + 1) * LANES]
        sq = blk * blk
        s_lo = jnp.sum(jnp.where(lo, sq, 0.0), axis=-1, keepdims=True)
        s_hi = jnp.sum(jnp.where(lo, 0.0, sq), axis=-1, keepdims=True)
        inv = jnp.where(lo, lax.rsqrt(s_lo / HEAD_DIM + EPS), lax.rsqrt(s_hi / HEAD_DIM + EPS))
        outs.append(blk * inv)
    return jnp.concatenate(outs, axis=1) * g


def _cast_specs(arrays, steps, step_of=lambda i: i):
    specs = []
    for a in arrays:
        rows = a.shape[0]
        nblk = max(n for n in range(1, steps + 1)
                   if steps % n == 0 and rows % n == 0 and (rows // n) % BF16_ROWS == 0)
        rep = steps // nblk
        specs.append(pl.BlockSpec((rows // nblk, a.shape[1]), lambda *g, rep=rep: (step_of(*g) // rep, 0)))
    return specs


def _cast_slices(srcs, dsts):
    for src, dst in zip(srcs, dsts):
        dst[...] = src[...].astype(BF16)


def _bf16_shapes(arrays):
    return [jax.ShapeDtypeStruct(a.shape, BF16) for a in arrays]


def _rows_to_tiles(ref, x):
    t = x.shape[0]
    for s in range(ROW_TILES):
        ref[pl.ds(s, t, stride=ROW_TILES), :] = x[:, s * LANES:(s + 1) * LANES]


def _tiles_to_rows(ref, t):
    return jnp.concatenate([ref[pl.ds(s, t, stride=ROW_TILES), :] for s in range(ROW_TILES)], axis=1)


def _conv_taps(uext_ref, wdw_ref, bdw_ref, c_ref, tt):
    win_rows = CONV_ROWS + CONV_PAD

    def chunk(ci, carry):
        r0 = pl.multiple_of(ci * CONV_ROWS, CONV_ROWS)
        for lt in range(ROW_TILES):
            cols = slice(lt * LANES, (lt + 1) * LANES)
            win = uext_ref[pl.ds(r0, win_rows), cols]
            acc = jnp.broadcast_to(bdw_ref[:, cols], (CONV_ROWS, LANES))
            for b in range(SUBLANES):
                wb = win if b == 0 else pltpu.roll(win, win_rows - b, axis=0)
                for a in range(CONV_PAD // SUBLANES + 1):
                    j = a * SUBLANES + b - (CONV_PAD - CONV_CTX)
                    if 0 <= j < CONV_WIDTH:
                        acc = acc + wdw_ref[j:j + 1, cols] * wb[a * SUBLANES:a * SUBLANES + CONV_ROWS]
            c_ref[pl.ds(r0, CONV_ROWS), cols] = acc
        return carry
    lax.fori_loop(0, tt // CONV_ROWS, chunk, 0)


def _conv_prompt_kernel(x_ref, nm_ref, w1_ref, b1_ref, wdw_ref, bdw_ref, lng_ref, lnb_ref, w2_ref, b2_ref,
                        *refs, tt, n_cast):
    cast_in, (h_ref, st_ref), cast_out = refs[:n_cast], refs[n_cast:n_cast + 2], refs[n_cast + 2:2 * n_cast + 2]
    uext_ref, c_ref = refs[2 * n_cast + 2:]
    _cast_slices(cast_in, cast_out)
    t = pl.program_id(1)

    @pl.when(t == 0)
    def _():
        uext_ref[0:CONV_PAD, :] = jnp.zeros((CONV_PAD, D_MODEL), F32)

    @pl.when(t > 0)
    def _():
        uext_ref[0:CONV_PAD, :] = uext_ref[tt:tt + CONV_PAD, :]

    x = x_ref[0]
    z = _dot(_rms(x, nm_ref[...]), w1_ref[...]) + b1_ref[...]
    u = z[:, :D_MODEL] * _sigmoid(z[:, D_MODEL:])
    uext_ref[CONV_PAD:CONV_PAD + tt, :] = u
    _conv_taps(uext_ref, wdw_ref, bdw_ref, c_ref, tt)
    c = _silu(_layer_norm(c_ref[...], lng_ref[...], lnb_ref[...]))
    h_ref[0] = x + _dot(c, w2_ref[...]) + b2_ref[...]
    st_ref[0, 0] = uext_ref[CONV_PAD + tt - CONV_CTX:CONV_PAD + tt, :]


def _conv_prompt(x, nm, w1, b1, wdw, bdw, lng, lnb, w2, b2, cast=()):
    n, t, d = x.shape
    tt = TOK_TILE
    nt = t // tt
    weights = (nm, w1, b1, wdw, bdw, lng, lnb, w2, b2)
    cast_specs = _cast_specs(cast, n * nt, lambda b, i: b * nt + i)
    return pl.pallas_call(
        functools.partial(_conv_prompt_kernel, tt=tt, n_cast=len(cast)),
        grid=(n, nt),
        in_specs=[pl.BlockSpec((1, tt, d), lambda b, i: (b, i, 0))] + [_const(w.shape) for w in weights]
                 + cast_specs,
        out_specs=[pl.BlockSpec((1, tt, d), lambda b, i: (b, i, 0)),
                   pl.BlockSpec((1, 1, CONV_CTX, d), lambda b, i: (0, b, 0, 0))] + cast_specs,
        out_shape=[jax.ShapeDtypeStruct((n, t, d), F32), jax.ShapeDtypeStruct((1, n, CONV_CTX, d), F32)]
                  + _bf16_shapes(cast),
        scratch_shapes=[pltpu.VMEM((CONV_PAD + tt, d), F32), pltpu.VMEM((tt, d), F32)],
        compiler_params=_params("arbitrary", "arbitrary"),
        name="conv_prompt",
    )(x, *weights, *cast)


def _conv_sample_kernel(x_ref, past_ref, nm_ref, w1_ref, b1_ref, wdw_ref, bdw_ref, lng_ref, lnb_ref,
                        w2_ref, b2_ref, h_ref, st_ref):
    x = x_ref[...]
    z = _dot(_rms(x, nm_ref[...]), w1_ref[...]) + b1_ref[...]
    u = z[:, :D_MODEL] * _sigmoid(z[:, D_MODEL:])
    c = wdw_ref[CONV_CTX:CONV_WIDTH, :] * u + bdw_ref[...]
    for j in range(CONV_CTX):
        c = c + wdw_ref[j:j + 1, :] * past_ref[0, j]
        if j > 0:
            st_ref[0, j - 1] = past_ref[0, j]
    st_ref[0, CONV_CTX - 1] = u
    c = _silu(_layer_norm(c, lng_ref[...], lnb_ref[...]))
    h_ref[...] = x + _dot(c, w2_ref[...]) + b2_ref[...]


def _conv_sample(x, past, nm, w1, b1, wdw, bdw, lng, lnb, w2, b2):
    n, d = x.shape
    nb = 16
    weights = (nm, w1, b1, wdw, bdw, lng, lnb, w2, b2)
    state = pl.BlockSpec((1, CONV_CTX, nb, d), lambda i: (0, 0, i, 0))
    return pl.pallas_call(
        _conv_sample_kernel,
        grid=(n // nb,),
        in_specs=[pl.BlockSpec((nb, d), lambda i: (i, 0)), state] + [_const(w.shape) for w in weights],
        out_specs=[pl.BlockSpec((nb, d), lambda i: (i, 0)), state],
        out_shape=[jax.ShapeDtypeStruct((n, d), F32), jax.ShapeDtypeStruct((1, CONV_CTX, n, d), F32)],
        compiler_params=_params("arbitrary"),
        name="conv_sample",
    )(x, past, *weights)


def _ffn_kernel(h_ref, g_ref, wg_ref, wu_ref, wd_ref, *refs):
    n_cast = len(refs) // 2
    cast_in, o_ref, cast_out = refs[:n_cast], refs[n_cast], refs[n_cast + 1:]
    h = h_ref[...]
    hn = _rms(h, g_ref[...]).astype(BF16)
    a = jnp.dot(hn, wg_ref[...], preferred_element_type=F32)
    b = jnp.dot(hn, wu_ref[...], preferred_element_type=F32)
    o_ref[...] = h + _dot(_silu(a) * b, wd_ref[...])
    _cast_slices(cast_in, cast_out)


def _ffn(h, g, wg, wu, wd, cast=()):
    m, d = h.shape
    tm = min(TOK_TILE, m)
    steps = m // tm
    weights = (g, wg, wu, wd)
    cast_specs = _cast_specs(cast, steps)
    out = pl.pallas_call(
        _ffn_kernel,
        grid=(steps,),
        in_specs=[pl.BlockSpec((tm, d), lambda i: (i, 0))] + [_const(w.shape) for w in weights] + cast_specs,
        out_specs=[pl.BlockSpec((tm, d), lambda i: (i, 0))] + cast_specs,
        out_shape=[jax.ShapeDtypeStruct((m, d), F32)] + _bf16_shapes(cast),
        compiler_params=_params("arbitrary"),
        name="dense_ffn",
    )(h, *weights, *cast)
    return out if cast else out[0]


def _ple(h, p, g, w_gate, w_proj):
    return h + _sigmoid(_dot(_rms(h, g), w_gate)) * _dot(p, w_proj)


def _ple_kvq_kernel(h_ref, p_ref, gple_ref, wpg_ref, wpp_ref, gkv_ref, wk_ref, wv_ref, kn_ref,
                    gmix_ref, wq_ref, qn_ref, *refs):
    n_cast = (len(refs) - 4) // 2
    cast_in, (h3_ref, k_ref, v_ref, q_ref), cast_out = refs[:n_cast], refs[n_cast:n_cast + 4], refs[n_cast + 4:]
    _cast_slices(cast_in, cast_out)
    h3 = _ple(h_ref[...], p_ref[...], gple_ref[...], wpg_ref[...], wpp_ref[...])
    h3_ref[...] = h3
    s = _rms(h3, gkv_ref[...]).astype(BF16)
    k_ref[...] = _head_rms(jnp.dot(s, wk_ref[...], preferred_element_type=F32), kn_ref[...])
    v_ref[...] = jnp.dot(s, wv_ref[...], preferred_element_type=F32)
    q = _dot(_rms(h3, gmix_ref[...]), wq_ref[...])
    q_ref[...] = (_head_rms(q, qn_ref[...]) * (HEAD_DIM ** -0.5)).astype(BF16)


def _ple_kvq(h, p, gple, wpg, wpp, gkv, wk, wv, kn, gmix, wq, qn, cast=()):
    m, d = h.shape
    tm = min(TOK_TILE, m)
    steps = m // tm
    weights = (gple, wpg, wpp, gkv, wk, wv, kn, gmix, wq, qn)
    row = lambda w: pl.BlockSpec((tm, w), lambda i: (i, 0))
    cast_specs = _cast_specs(cast, steps)
    return pl.pallas_call(
        _ple_kvq_kernel,
        grid=(steps,),
        in_specs=[row(d), row(PLE_DIM)] + [_const(w.shape) for w in weights] + cast_specs,
        out_specs=[row(d), row(KV_DIM), row(KV_DIM), row(d)] + cast_specs,
        out_shape=[jax.ShapeDtypeStruct((m, d), F32), jax.ShapeDtypeStruct((m, KV_DIM), F32),
                   jax.ShapeDtypeStruct((m, KV_DIM), F32), jax.ShapeDtypeStruct((m, d), BF16)] + _bf16_shapes(cast),
        compiler_params=_params("arbitrary"),
        name="ple_kv_q",
    )(h, p, *weights, *cast)


def _softmax_sink(s, sink):
    m = jnp.maximum(jnp.max(s, axis=-1, keepdims=True), sink)
    e = jnp.exp(s - m)
    den = jnp.sum(e, axis=-1, keepdims=True) + jnp.exp(sink - m)
    return e / den


def _attn_prompt_kernel(q_ref, kp_ref, kc_ref, vp_ref, vc_ref, sink_ref, o_ref):
    n = pl.program_id(1)
    q = q_ref[0]
    k = jnp.concatenate([kp_ref[0], kc_ref[0]], axis=0).astype(BF16)
    v = jnp.concatenate([vp_ref[0], vc_ref[0]], axis=0).astype(BF16)
    qi = lax.broadcasted_iota(jnp.int32, (WINDOW, 2 * WINDOW), 0)
    kj = lax.broadcasted_iota(jnp.int32, (WINDOW, 2 * WINDOW), 1)
    dist = qi + WINDOW - kj
    first_key = jnp.where(n > 0, 0, WINDOW)
    mask = (dist >= 0) & (dist <= WINDOW) & (kj >= first_key)
    outs = []
    for h in range(N_HEADS):
        kh = h % N_KV_HEADS
        qh = q[:, h * HEAD_DIM:(h + 1) * HEAD_DIM]
        kk = k[:, kh * HEAD_DIM:(kh + 1) * HEAD_DIM]
        vv = v[:, kh * HEAD_DIM:(kh + 1) * HEAD_DIM]
        s = lax.dot_general(qh, kk, (((1,), (1,)), ((), ())), preferred_element_type=F32)
        s = jnp.where(mask, s, -jnp.inf)
        p = _softmax_sink(s, sink_ref[h])
        outs.append(jnp.dot(p.astype(BF16), vv, preferred_element_type=F32))
    o_ref[0] = jnp.concatenate(outs, axis=1).astype(BF16)


def _attn_prompt(q, k, v, sinks):
    n, t, d = q.shape
    nb = t // WINDOW
    cur = lambda w: pl.BlockSpec((1, WINDOW, w), lambda b, i: (b, i, 0))
    prev = lambda w: pl.BlockSpec((1, WINDOW, w), lambda b, i: (b, jnp.maximum(i - 1, 0), 0))
    return pl.pallas_call(
        _attn_prompt_kernel,
        grid=(n, nb),
        in_specs=[cur(d), prev(KV_DIM), cur(KV_DIM), prev(KV_DIM), cur(KV_DIM),
                  pl.BlockSpec(memory_space=pltpu.SMEM)],
        out_specs=cur(d),
        out_shape=jax.ShapeDtypeStruct((n, t, d), BF16),
        compiler_params=_params("arbitrary", "arbitrary"),
        name="attn_prompt",
    )(q, k, k, v, v, sinks)


def _attn_sample_kernel(q_ref, kc_ref, vc_ref, kn_ref, vn_ref, sink_ref, o_ref, ks_ref, vs_ref, *, nb):
    lane_head = lax.broadcasted_iota(jnp.int32, (N_KV_HEADS, KV_DIM), 1) // HEAD_DIM
    own = lane_head == lax.broadcasted_iota(jnp.int32, (N_KV_HEADS, KV_DIM), 0)
    sink = sink_ref[...]
    rows = []
    for i in range(nb):
        kc = kc_ref[i].T
        vc = vc_ref[i].T
        kn = kn_ref[i:i + 1, :]
        vn = vn_ref[i:i + 1, :]
        ks_ref[i, 0:WINDOW - 1, :] = kc[1:WINDOW, :]
        ks_ref[i, WINDOW - 1:WINDOW, :] = kn
        vs_ref[i, 0:WINDOW - 1, :] = vc[1:WINDOW, :]
        vs_ref[i, WINDOW - 1:WINDOW, :] = vn
        qrow = q_ref[i:i + 1, :].astype(F32)
        qx = jnp.concatenate(
            [jnp.where(own, jnp.broadcast_to(qrow[:, g * KV_DIM:(g + 1) * KV_DIM], (N_KV_HEADS, KV_DIM)), 0.0)
             for g in range(GROUP)], axis=0)
        s_c = lax.dot_general(qx.astype(BF16), kc.astype(BF16), (((1,), (1,)), ((), ())),
                              preferred_element_type=F32)
        s_n = jnp.sum(qx * kn.astype(BF16).astype(F32), axis=-1, keepdims=True)
        m = jnp.maximum(jnp.maximum(jnp.max(s_c, axis=-1, keepdims=True), s_n), sink)
        e_c = jnp.exp(s_c - m)
        e_n = jnp.exp(s_n - m)
        den = jnp.sum(e_c, axis=-1, keepdims=True) + e_n + jnp.exp(sink - m)
        ox = jnp.dot((e_c / den).astype(BF16), vc.astype(BF16), preferred_element_type=F32)
        ox = ox + (e_n / den).astype(BF16).astype(F32) * vn.astype(BF16).astype(F32)
        rows.append(jnp.concatenate(
            [jnp.sum(jnp.where(own, ox[g * N_KV_HEADS:(g + 1) * N_KV_HEADS], 0.0), axis=0, keepdims=True)
             for g in range(GROUP)], axis=1))
    o_ref[...] = jnp.concatenate(rows, axis=0).astype(BF16)


def _attn_sample(q, kc, vc, kn, vn, sinks_col):
    n, d = q.shape
    nb = 16
    row = lambda w: pl.BlockSpec((nb, w), lambda i: (i, 0))
    cache_in = pl.BlockSpec((nb, KV_DIM, WINDOW), lambda i: (i, 0, 0))
    cache = pl.BlockSpec((nb, WINDOW, KV_DIM), lambda i: (i, 0, 0))
    return pl.pallas_call(
        functools.partial(_attn_sample_kernel, nb=nb),
        grid=(n // nb,),
        in_specs=[row(d), cache_in, cache_in, row(KV_DIM), row(KV_DIM), _const(sinks_col.shape)],
        out_specs=[row(d), cache, cache],
        out_shape=[jax.ShapeDtypeStruct((n, d), BF16),
                   jax.ShapeDtypeStruct((n, WINDOW, KV_DIM), F32),
                   jax.ShapeDtypeStruct((n, WINDOW, KV_DIM), F32)],
        compiler_params=_params("arbitrary"),
        name="attn_sample",
    )(q, kc, vc, kn, vn, sinks_col)


def _oproj_router_kernel(rows_in_ref, o_ref, h_ref, wo_ref, g_ref, wr_ref, br_ref,
                         h4_ref, hn_ref, route_ref, sort_ref, cnt_ref, carry_ref, *, tm):
    del rows_in_ref
    i = pl.program_id(0)

    @pl.when(i == 0)
    def _():
        carry_ref[...] = jnp.zeros((1, LANES), F32)

    h4 = h_ref[...] + jnp.dot(o_ref[...], wo_ref[...], preferred_element_type=F32)
    h4_ref[...] = h4
    hn = _rms(h4, g_ref[...])
    _rows_to_tiles(hn_ref, hn)
    hn_hi = hn.astype(BF16)
    hn_lo = (hn - hn_hi.astype(F32)).astype(BF16)
    part_hi = jnp.dot(hn_hi, wr_ref[...], preferred_element_type=F32)
    part_lo = jnp.dot(hn_lo, wr_ref[...], preferred_element_type=F32)
    logits = (part_hi[:, :LANES] + (part_hi[:, LANES:] + part_lo[:, :LANES]) + part_lo[:, LANES:]) + br_ref[...]
    lane = lax.broadcasted_iota(jnp.int32, (tm, LANES), 1).astype(F32)
    m1 = jnp.max(logits, axis=-1, keepdims=True)
    i1 = jnp.min(jnp.where(logits == m1, lane, float(LANES)), axis=-1, keepdims=True)
    rest = jnp.where(lane == i1, -jnp.inf, logits)
    m2 = jnp.max(rest, axis=-1, keepdims=True)
    i2 = jnp.min(jnp.where(rest == m2, lane, float(LANES)), axis=-1, keepdims=True)
    e2 = jnp.exp(m2 - m1)
    w1 = 1.0 / (1.0 + e2)
    w2 = e2 / (1.0 + e2)
    oh1 = lane == i1
    oh2 = lane == i2
    sel = jnp.where(oh1, 1.0, 0.0) + jnp.where(oh2, 1.0, 0.0)
    row = lax.broadcasted_iota(jnp.int32, (tm, tm), 0)
    col = lax.broadcasted_iota(jnp.int32, (tm, tm), 1)
    tri = jnp.where(row > col, 1.0, 0.0).astype(BF16)
    before = jnp.dot(tri, sel.astype(BF16), preferred_element_type=F32) + carry_ref[...]
    r1 = jnp.sum(jnp.where(oh1, before, 0.0), axis=-1, keepdims=True)
    r2 = jnp.sum(jnp.where(oh2, before, 0.0), axis=-1, keepdims=True)
    carry_ref[...] = carry_ref[...] + jnp.sum(sel, axis=0, keepdims=True)
    cnt_ref[...] = carry_ref[...]
    fields = (i1, i2, r1, r2, w1, w2)
    route = jnp.zeros((tm, LANES), F32)
    for f, val in enumerate(fields):
        route = jnp.where(lane == float(f), val, route)
    route_ref[...] = route
    sort_ref[...] = route.T[0:SUBLANES, :]


def _oproj_router(rows_buf, o, h, wo, g, wr, br, *, first_token):
    m, d = h.shape
    tm = min(TOK_TILE, m)
    weights = (wo, g, wr, br)
    row = lambda w: pl.BlockSpec((tm, w), lambda i: (i, 0))
    first_block = first_token // tm
    return pl.pallas_call(
        functools.partial(_oproj_router_kernel, tm=tm),
        grid=(m // tm,),
        in_specs=[pl.BlockSpec(memory_space=pl.ANY), row(d), row(d)] + [_const(w.shape) for w in weights],
        out_specs=[row(d), pl.BlockSpec((tm * ROW_TILES, LANES), lambda i: (i + first_block, 0)), row(LANES),
                   pl.BlockSpec((SUBLANES, tm), lambda i: (0, i)), pl.BlockSpec((1, LANES), lambda i: (0, 0))],
        out_shape=[jax.ShapeDtypeStruct((m, d), F32), jax.ShapeDtypeStruct(rows_buf.shape, F32),
                   jax.ShapeDtypeStruct((m, LANES), F32), jax.ShapeDtypeStruct((SUBLANES, m), F32),
                   jax.ShapeDtypeStruct((1, LANES), F32)],
        scratch_shapes=[pltpu.VMEM((1, LANES), F32)],
        input_output_aliases={0: 1},
        compiler_params=_params("arbitrary"),
        name="oproj_router",
    )(rows_buf, o, h, *weights)


def _row(ref, r):
    return ref.at[pl.ds(pl.multiple_of(r * ROW_TILES, ROW_TILES), ROW_TILES), :]


def _for_rows(lo, hi, fn):
    assert (hi - lo) % COPY_UNROLL == 0

    def body(j, carry):
        for u in range(COPY_UNROLL):
            fn(lo + j * COPY_UNROLL + u)
        return carry
    lax.fori_loop(0, (hi - lo) // COPY_UNROLL, body, 0)


def _moe_kernel(te_ref, nv_ref, nr_ref, pos_ref, rows_ref, wg_ref, wu_ref, wd_ref, y_ref,
                inv_ref, xg_ref, yb_ref, xb_ref, acc_ref, gsem, ssem, isem, *, tm, nf, nt, n_pairs, pair_regions):
    i = pl.program_id(0)
    f = pl.program_id(1)
    nv = nv_ref[0]
    valid = i < nv
    slot = i % 2
    assert nf == 2

    def gather_row(tile, r, to_slot):
        pair = inv_ref[tile * tm + r]
        token = pair
        for first_pair, _, first_token in pair_regions[1:]:
            token = jnp.where(pair >= first_pair, pair - (first_pair - first_token), token)
        pltpu.make_async_copy(_row(rows_ref, token), _row(xg_ref.at[to_slot], r), gsem.at[to_slot]).start()

    def scatter_row(tile, real, r, from_slot):
        pair = jnp.where(r < real, inv_ref[tile * tm + r], n_pairs + from_slot * tm + r)
        pltpu.make_async_copy(_row(yb_ref.at[from_slot], r), _row(y_ref, pair), ssem.at[from_slot]).start()

    def wait_gather(at_slot):
        pltpu.make_async_copy(rows_ref.at[pl.ds(0, tm * ROW_TILES), :], xg_ref.at[at_slot], gsem.at[at_slot]).wait()

    def wait_scatter(at_slot):
        pltpu.make_async_copy(yb_ref.at[at_slot], y_ref.at[pl.ds(0, tm * ROW_TILES), :], ssem.at[at_slot]).wait()

    @pl.when(jnp.logical_and(i == 0, f == 0))
    def _():
        def place(p):
            inv_ref[pos_ref[p]] = p
        _for_rows(0, n_pairs, place)

        def clear_padding(t, carry):
            def clear(r, c):
                inv_ref[t * tm + r] = 0
                return c
            return lax.fori_loop(nr_ref[t], tm, clear, carry)
        lax.fori_loop(0, nv, clear_padding, 0)
        yb_ref[...] = jnp.zeros(yb_ref.shape, F32)
        spare = y_ref.at[pl.ds(n_pairs * ROW_TILES, 2 * tm * ROW_TILES), :]
        for s in range(2):
            half = spare.at[pl.ds(s * tm * ROW_TILES, tm * ROW_TILES), :]
            pltpu.make_async_copy(yb_ref.at[s], half, isem).start()
            pltpu.make_async_copy(yb_ref.at[s], half, isem).wait()
        _for_rows(0, tm, lambda r: gather_row(0, r, 0))

    @pl.when(jnp.logical_and(valid, f == 0))
    def _():
        wait_gather(slot)
        xb_ref[...] = _tiles_to_rows(xg_ref.at[slot], tm).astype(BF16)
        acc_ref[...] = jnp.zeros((tm, D_MODEL), F32)

    def matmuls(first):
        if first:
            nxt = jnp.minimum(i + 1, nv - 1)
            for r in range(tm):
                gather_row(nxt, r, 1 - slot)
        else:
            prev = jnp.maximum(i - 1, 0)
            real = jnp.where(i >= 1, nr_ref[prev], 0)
            for r in range(tm):
                scatter_row(prev, real, r, 1 - slot)
        x = xb_ref[...]
        a = jnp.dot(x, wg_ref[0], preferred_element_type=F32)
        b = jnp.dot(x, wu_ref[0], preferred_element_type=F32)
        acc_ref[...] += _dot(_silu(a) * b, wd_ref[0])

    @pl.when(jnp.logical_and(valid, f == 0))
    def _():
        matmuls(True)

    @pl.when(jnp.logical_and(valid, f > 0))
    def _():
        matmuls(False)

    @pl.when(jnp.logical_and(valid, f == nf - 1))
    def _():
        @pl.when(i >= 1)
        def _():
            wait_scatter(slot)
        _rows_to_tiles(yb_ref.at[slot], acc_ref[...])

        @pl.when(i == nv - 1)
        def _():
            _for_rows(0, tm, lambda r: scatter_row(i, nr_ref[i], r, slot))

    @pl.when(jnp.logical_and(i == nt - 1, f == nf - 1))
    def _():
        wait_gather(nv % 2)
        wait_scatter(nv % 2)
        wait_scatter((nv - 1) % 2)


def _moe(tile_expert, n_valid, tile_rows, pos, pair_regions, rows, wg, wu, wd):
    tm, tf = MOE_TILE, MOE_F_TILE
    nt, nf = tile_expert.shape[0], D_EXPERT // tf
    n_pairs = pos.shape[0]

    def fidx(i, f, nv):
        return jnp.where(i < nv[0], f, nf - 1)
    tile_buf = pltpu.VMEM((2, tm * ROW_TILES, LANES), F32)
    return pl.pallas_call(
        functools.partial(_moe_kernel, tm=tm, nf=nf, nt=nt, n_pairs=n_pairs, pair_regions=pair_regions),
        grid_spec=pltpu.PrefetchScalarGridSpec(
            num_scalar_prefetch=4,
            grid=(nt, nf),
            in_specs=[pl.BlockSpec(memory_space=pl.ANY),
                      pl.BlockSpec((1, D_MODEL, tf), lambda i, f, te, nv, *_: (te[i], 0, fidx(i, f, nv))),
                      pl.BlockSpec((1, D_MODEL, tf), lambda i, f, te, nv, *_: (te[i], 0, fidx(i, f, nv))),
                      pl.BlockSpec((1, tf, D_MODEL), lambda i, f, te, nv, *_: (te[i], fidx(i, f, nv), 0))],
            out_specs=pl.BlockSpec(memory_space=pl.ANY),
            scratch_shapes=[pltpu.SMEM((nt * tm,), jnp.int32), tile_buf, tile_buf,
                            pltpu.VMEM((tm, D_MODEL), BF16), pltpu.VMEM((tm, D_MODEL), F32),
                            pltpu.SemaphoreType.DMA((2,)), pltpu.SemaphoreType.DMA((2,)),
                            pltpu.SemaphoreType.DMA(())],
        ),
        out_shape=jax.ShapeDtypeStruct(((n_pairs + 2 * tm) * ROW_TILES, LANES), F32),
        compiler_params=_params("arbitrary", "arbitrary"),
        name="moe_grouped",
    )(tile_expert, n_valid, tile_rows, pos, rows, wg, wu, wd)


def _combine_kernel(y1_ref, y2_ref, h_ref, route_ref, p_ref, g_ref, wpg_ref, wpp_ref, o_ref, *, tb):
    route = route_ref[...]
    w1 = route[:, 4:5]
    w2 = route[:, 5:6]
    h5 = h_ref[...] + (w1 * _tiles_to_rows(y1_ref, tb) + w2 * _tiles_to_rows(y2_ref, tb))
    o_ref[...] = _ple(h5, p_ref[...], g_ref[...], wpg_ref[...], wpp_ref[...])


def _combine(y_pairs, h, route, p, g, wpg, wpp, *, first_pair):
    m, d = h.shape
    tb = min(COMBINE_TILE, m)
    weights = (g, wpg, wpp)
    row = lambda w: pl.BlockSpec((tb, w), lambda i: (i, 0))
    steps = m // tb
    first_block = first_pair // tb
    pair_rows = lambda slot: pl.BlockSpec((tb * ROW_TILES, LANES), lambda i: (i + first_block + slot * steps, 0))
    return pl.pallas_call(
        functools.partial(_combine_kernel, tb=tb),
        grid=(steps,),
        in_specs=[pair_rows(0), pair_rows(1), row(d), row(LANES),
                  pl.BlockSpec((tb, PLE_DIM), lambda i: (i + steps, 0))] + [_const(w.shape) for w in weights],
        out_specs=row(d),
        out_shape=jax.ShapeDtypeStruct((m, d), F32),
        compiler_params=_params("arbitrary"),
        name="moe_combine",
    )(y_pairs, y_pairs, h, route, p, *weights)


def _row2(v):
    return v.reshape(1, -1).astype(F32)


def kernel(x_prompt, x_sample, state_conv, cache_k, cache_v, p_prompt, p_sample, norm_mix, norm_ffn, norm_ple,
           conv_w_pw1, conv_b_pw1, conv_w_dw, conv_b_dw, conv_ln_g, conv_ln_b, conv_w_pw2, conv_b_pw2,
           norm_kv, w_k, w_v, k_norm, w_q, q_norm, sinks, w_o, ffn_w_gate, ffn_w_up, ffn_w_down,
           moe_w_router, moe_b_router, moe_w_gate, moe_w_up, moe_w_down, ple_w_gate, ple_w_proj):
    nb, t, d = x_prompt.shape
    ns = x_sample.shape[0]
    mp = nb * t
    bf = lambda w: w.astype(BF16)

    conv_w = (_row2(norm_mix[0]), bf(conv_w_pw1[0]), _row2(conv_b_pw1[0]), conv_w_dw[0], _row2(conv_b_dw[0]),
              _row2(conv_ln_g[0]), _row2(conv_ln_b[0]), bf(conv_w_pw2[0]), _row2(conv_b_pw2[0]))
    wq = w_q[0].reshape(d, N_KV_HEADS, GROUP, HEAD_DIM).transpose(0, 2, 1, 3).reshape(d, d)
    wo = w_o[0].reshape(N_KV_HEADS, GROUP, HEAD_DIM, d).transpose(1, 0, 2, 3).reshape(d, d)
    sinks_gk = sinks[0].astype(F32).reshape(N_KV_HEADS, GROUP).T.reshape(N_HEADS)
    wr = jnp.zeros((d, LANES), F32).at[:, :N_EXPERTS].set(moe_w_router[0])
    wr_hi = bf(wr)
    wr_lo = bf(wr - wr_hi.astype(F32))
    br = jnp.full((1, LANES), NEG_BIG, F32).at[0, :N_EXPERTS].set(moe_b_router[0])
    pp = p_prompt.reshape(-1, PLE_DIM)
    ps = p_sample.reshape(-1, PLE_DIM)
    flat = lambda w: w.reshape(-1, w.shape[-1])

    h1p, conv_p, ffn_g, ffn_u, ffn_d, moe_g = _conv_prompt(
        x_prompt, *conv_w, cast=(ffn_w_gate[0], ffn_w_up[0], ffn_w_down[0], flat(moe_w_gate[0])))
    ffn_w = (_row2(norm_ffn[0]), ffn_g, ffn_u, ffn_d)
    h2p, ple_g0, ple_p0, wk_bf, wv_bf, wq_bf, moe_u = _ffn(
        h1p.reshape(mp, d), *ffn_w, cast=(ple_w_gate[0], ple_w_proj[0], w_k, w_v, wq, flat(moe_w_up[0])))
    kvq_w = (_row2(norm_ple[0]), ple_g0, ple_p0, _row2(norm_kv), wk_bf, wv_bf,
             _row2(jnp.tile(k_norm, N_KV_HEADS)), _row2(norm_mix[1]), wq_bf, _row2(jnp.tile(q_norm[0], N_HEADS)))
    h3p, kp, vp, qp, wo_bf, ple_g1, ple_p1, moe_d = _ple_kvq(
        h2p, pp, *kvq_w, cast=(wo, ple_w_gate[1], ple_w_proj[1], flat(moe_w_down[0])))
    router_w = (wo_bf, _row2(norm_ffn[1]), jnp.concatenate([wr_hi, wr_lo], axis=1), br)
    ple1_w = (_row2(norm_ple[1]), ple_g1, ple_p1)
    expert_bf = [b.reshape(w.shape[1:]) for b, w in zip((moe_g, moe_u, moe_d), (moe_w_gate, moe_w_up, moe_w_down))]
    h1s, conv_s = _conv_sample(x_sample.reshape(ns, d), state_conv.transpose(0, 2, 1, 3), *conv_w)
    conv_s = conv_s.transpose(0, 2, 1, 3)
    h3s, ks, vs, qs = _ple_kvq(_ffn(h1s, *ffn_w), ps, *kvq_w)[:4]

    kp3 = kp.reshape(nb, t, KV_DIM)
    vp3 = vp.reshape(nb, t, KV_DIM)
    op = _attn_prompt(qp.reshape(nb, t, d), kp3, vp3, sinks_gk)
    feature_major = lambda c: c.transpose(0, 2, 3, 1).reshape(ns, KV_DIM, WINDOW)
    os_, k_state_s, v_state_s = _attn_sample(qs, feature_major(cache_k), feature_major(cache_v), ks, vs,
                                             sinks_gk.reshape(N_HEADS, 1))

    mt = mp + ns
    rows = jnp.zeros((mt * ROW_TILES, LANES), F32)
    h4p, rows, route_p, sort_p, cnt_p = _oproj_router(rows, op.reshape(mp, d), h3p, *router_w, first_token=0)
    h4s, rows, route_s, sort_s, cnt_s = _oproj_router(rows, os_, h3s, *router_w, first_token=mp)

    cnt_p = cnt_p[0, :N_EXPERTS].astype(jnp.int32)
    cnt_s = cnt_s[0, :N_EXPERTS].astype(jnp.int32)
    group = (cnt_p + cnt_s + MOE_TILE - 1) // MOE_TILE * MOE_TILE
    ends = jnp.cumsum(group)
    starts = ends - group
    n_tiles = (2 * (mp + ns) + N_EXPERTS * (MOE_TILE - 1)) // MOE_TILE
    n_valid = (ends[-1] // MOE_TILE).astype(jnp.int32)
    tile_start = jnp.minimum(jnp.arange(n_tiles, dtype=jnp.int32), n_valid - 1) * MOE_TILE
    tile_expert = jnp.sum(tile_start[:, None] >= ends[None, :], axis=1).astype(jnp.int32)

    def positions(sort_info, slot, base):
        idx = sort_info[slot].astype(jnp.int32)
        rank = sort_info[2 + slot].astype(jnp.int32)
        return (starts + base)[idx] + rank
    no_base = jnp.zeros_like(cnt_p)
    pos = jnp.concatenate([positions(sort_p, 0, no_base), positions(sort_p, 1, no_base),
                           positions(sort_s, 0, cnt_p), positions(sort_s, 1, cnt_p)]).astype(jnp.int32)
    pair_regions = ((0, mp, 0), (mp, mp, 0), (2 * mp, ns, mp), (2 * mp + ns, ns, mp))
    tile_ids = jnp.arange(n_tiles, dtype=jnp.int32)
    rows_left = (starts + cnt_p + cnt_s)[tile_expert] - tile_ids * MOE_TILE
    tile_rows = jnp.where(tile_ids < n_valid, jnp.clip(rows_left, 0, MOE_TILE), 0).astype(jnp.int32)
    y_pairs = _moe(tile_expert, n_valid.reshape(1), tile_rows, pos, pair_regions, rows,
                   *expert_bf)

    y_p = _combine(y_pairs, h4p, route_p, pp, *ple1_w, first_pair=0)
    y_s = _combine(y_pairs, h4s, route_s, ps, *ple1_w, first_pair=2 * mp)

    kv4 = lambda a, n: a.reshape(n, WINDOW, N_KV_HEADS, HEAD_DIM)
    return (y_p.reshape(nb, t, d), y_s.reshape(ns, 1, d), conv_p,
            kv4(kp3[:, t - WINDOW:], nb), kv4(vp3[:, t - WINDOW:], nb),
            conv_s, kv4(k_state_s, ns), kv4(v_state_s, ns))
```

```python
import functools

import jax
import jax.numpy as jnp
from jax import lax
from jax.experimental import pallas as pl
from jax.experimental.pallas import tpu as pltpu

D_MODEL = 1024
CONV_WIDTH = 31
CONV_CTX = CONV_WIDTH - 1
HEAD_DIM = 64
N_HEADS = 16
N_KV_HEADS = 4
GROUP = N_HEADS // N_KV_HEADS
KV_DIM = N_KV_HEADS * HEAD_DIM
WINDOW = 128
D_FF = 2816
N_EXPERTS = 8
D_EXPERT = 3584
PLE_DIM = 256
EPS = 1e-6

BF16 = jnp.bfloat16
F32 = jnp.float32

LANES = 128
SUBLANES = 8
BF16_ROWS = 16
ROW_TILES = D_MODEL // LANES
VMEM_LIMIT = 58 * 1024 * 1024

CONV_PAD = 32
CONV_ROWS = 128
TOK_TILE = 512
MOE_TILE = 512
MOE_F_TILE = 1792
COMBINE_TILE = 512
COPY_UNROLL = 8
NEG_BIG = -1e30


def _params(*sem):
    return pltpu.CompilerParams(dimension_semantics=sem, vmem_limit_bytes=VMEM_LIMIT)


def _const(shape):
    nd = len(shape)
    return pl.BlockSpec(shape, lambda *_: (0,) * nd, pipeline_mode=pl.Buffered(1))


def _rms(x, g):
    return x * lax.rsqrt(jnp.mean(x * x, axis=-1, keepdims=True) + EPS) * g


def _dot(a, w):
    return jnp.dot(a.astype(BF16), w, preferred_element_type=F32)


def _sigmoid(x):
    return 1.0 / (1.0 + jnp.exp(-x))


def _silu(x):
    return x * _sigmoid(x)


def _layer_norm(x, g, b):
    mu = jnp.mean(x, axis=-1, keepdims=True)
    xc = x - mu
    var = jnp.mean(xc * xc, axis=-1, keepdims=True)
    return xc * lax.rsqrt(var + EPS) * g + b


def _head_rms(x, g):
    t, w = x.shape
    lane = lax.broadcasted_iota(jnp.int32, (t, LANES), 1)
    lo = lane < HEAD_DIM
    outs = []
    for c in range(w // LANES):
        blk = x[:, c * LANES:(c + 1) * LANES]
        sq = blk * blk
        s_lo = jnp.sum(jnp.where(lo, sq, 0.0), axis=-1, keepdims=True)
        s_hi = jnp.sum(jnp.where(lo, 0.0, sq), axis=-1, keepdims=True)
        inv = jnp.where(lo, lax.rsqrt(s_lo / HEAD_DIM + EPS), lax.rsqrt(s_hi / HEAD_DIM + EPS))
        outs.append(blk * inv)
    return jnp.concatenate(outs, axis=1) * g


def _cast_specs(arrays, steps, step_of=lambda i: i):
    specs = []
    for a in arrays:
        rows = a.shape[0]
        nblk = max(n for n in range(1, steps + 1)
                   if steps % n == 0 and rows % n == 0 and (rows // n) % BF16_ROWS == 0)
        rep = steps // nblk
        specs.append(pl.BlockSpec((rows // nblk, a.shape[1]), lambda *g, rep=rep: (step_of(*g) // rep, 0)))
    return specs


def _cast_slices(srcs, dsts):
    for src, dst in zip(srcs, dsts):
        dst[...] = src[...].astype(BF16)


def _bf16_shapes(arrays):
    return [jax.ShapeDtypeStruct(a.shape, BF16) for a in arrays]


def _rows_to_tiles(ref, x):
    t = x.shape[0]
    for s in range(ROW_TILES):
        ref[pl.ds(s, t, stride=ROW_TILES), :] = x[:, s * LANES:(s + 1) * LANES]


def _tiles_to_rows(ref, t):
    return jnp.concatenate([ref[pl.ds(s, t, stride=ROW_TILES), :] for s in range(ROW_TILES)], axis=1)


def _conv_taps(uext_ref, wdw_ref, bdw_ref, c_ref, tt):
    win_rows = CONV_ROWS + CONV_PAD

    def chunk(ci, carry):
        r0 = pl.multiple_of(ci * CONV_ROWS, CONV_ROWS)
        for lt in range(ROW_TILES):
            cols = slice(lt * LANES, (lt + 1) * LANES)
            win = uext_ref[pl.ds(r0, win_rows), cols]
            acc = jnp.broadcast_to(bdw_ref[:, cols], (CONV_ROWS, LANES))
            for b in range(SUBLANES):
                wb = win if b == 0 else pltpu.roll(win, win_rows - b, axis=0)
                for a in range(CONV_PAD // SUBLANES + 1):
                    j = a * SUBLANES + b - (CONV_PAD - CONV_CTX)
                    if 0 <= j < CONV_WIDTH:
                        acc = acc + wdw_ref[j:j + 1, cols] * wb[a * SUBLANES:a * SUBLANES + CONV_ROWS]
            c_ref[pl.ds(r0, CONV_ROWS), cols] = acc
        return carry
    lax.fori_loop(0, tt // CONV_ROWS, chunk, 0)


def _conv_prompt_kernel(x_ref, nm_ref, w1_ref, b1_ref, wdw_ref, bdw_ref, lng_ref, lnb_ref, w2_ref, b2_ref,
                        *refs, tt, n_cast):
    cast_in, (h_ref, st_ref), cast_out = refs[:n_cast], refs[n_cast:n_cast + 2], refs[n_cast + 2:2 * n_cast + 2]
    uext_ref, c_ref = refs[2 * n_cast + 2:]
    _cast_slices(cast_in, cast_out)
    t = pl.program_id(1)

    @pl.when(t == 0)
    def _():
        uext_ref[0:CONV_PAD, :] = jnp.zeros((CONV_PAD, D_MODEL), F32)

    @pl.when(t > 0)
    def _():
        uext_ref[0:CONV_PAD, :] = uext_ref[tt:tt + CONV_PAD, :]

    x = x_ref[0]
    z = _dot(_rms(x, nm_ref[...]), w1_ref[...]) + b1_ref[...]
    u = z[:, :D_MODEL] * _sigmoid(z[:, D_MODEL:])
    uext_ref[CONV_PAD:CONV_PAD + tt, :] = u
    _conv_taps(uext_ref, wdw_ref, bdw_ref, c_ref, tt)
    c = _silu(_layer_norm(c_ref[...], lng_ref[...], lnb_ref[...]))
    h_ref[0] = x + _dot(c, w2_ref[...]) + b2_ref[...]
    st_ref[0, 0] = uext_ref[CONV_PAD + tt - CONV_CTX:CONV_PAD + tt, :]


def _conv_prompt(x, nm, w1, b1, wdw, bdw, lng, lnb, w2, b2, cast=()):
    n, t, d = x.shape
    tt = TOK_TILE
    nt = t // tt
    weights = (nm, w1, b1, wdw, bdw, lng, lnb, w2, b2)
    cast_specs = _cast_specs(cast, n * nt, lambda b, i: b * nt + i)
    return pl.pallas_call(
        functools.partial(_conv_prompt_kernel, tt=tt, n_cast=len(cast)),
        grid=(n, nt),
        in_specs=[pl.BlockSpec((1, tt, d), lambda b, i: (b, i, 0))] + [_const(w.shape) for w in weights]
                 + cast_specs,
        out_specs=[pl.BlockSpec((1, tt, d), lambda b, i: (b, i, 0)),
                   pl.BlockSpec((1, 1, CONV_CTX, d), lambda b, i: (0, b, 0, 0))] + cast_specs,
        out_shape=[jax.ShapeDtypeStruct((n, t, d), F32), jax.ShapeDtypeStruct((1, n, CONV_CTX, d), F32)]
                  + _bf16_shapes(cast),
        scratch_shapes=[pltpu.VMEM((CONV_PAD + tt, d), F32), pltpu.VMEM((tt, d), F32)],
        compiler_params=_params("arbitrary", "arbitrary"),
        name="conv_prompt",
    )(x, *weights, *cast)


def _conv_sample_kernel(x_ref, past_ref, nm_ref, w1_ref, b1_ref, wdw_ref, bdw_ref, lng_ref, lnb_ref,
                        w2_ref, b2_ref, h_ref, st_ref):
    x = x_ref[...]
    z = _dot(_rms(x, nm_ref[...]), w1_ref[...]) + b1_ref[...]
    u = z[:, :D_MODEL] * _sigmoid(z[:, D_MODEL:])
    c = wdw_ref[CONV_CTX:CONV_WIDTH, :] * u + bdw_ref[...]
    for j in range(CONV_CTX):
        c = c + wdw_ref[j:j + 1, :] * past_ref[0, j]
        if j > 0:
            st_ref[0, j - 1] = past_ref[0, j]
    st_ref[0, CONV_CTX - 1] = u
    c = _silu(_layer_norm(c, lng_ref[...], lnb_ref[...]))
    h_ref[...] = x + _dot(c, w2_ref[...]) + b2_ref[...]


def _conv_sample(x, past, nm, w1, b1, wdw, bdw, lng, lnb, w2, b2):
    n, d = x.shape
    nb = 16
    weights = (nm, w1, b1, wdw, bdw, lng, lnb, w2, b2)
    state = pl.BlockSpec((1, CONV_CTX, nb, d), lambda i: (0, 0, i, 0))
    return pl.pallas_call(
        _conv_sample_kernel,
        grid=(n // nb,),
        in_specs=[pl.BlockSpec((nb, d), lambda i: (i, 0)), state] + [_const(w.shape) for w in weights],
        out_specs=[pl.BlockSpec((nb, d), lambda i: (i, 0)), state],
        out_shape=[jax.ShapeDtypeStruct((n, d), F32), jax.ShapeDtypeStruct((1, CONV_CTX, n, d), F32)],
        compiler_params=_params("arbitrary"),
        name="conv_sample",
    )(x, past, *weights)


def _ffn_kernel(h_ref, g_ref, wg_ref, wu_ref, wd_ref, *refs):
    n_cast = len(refs) // 2
    cast_in, o_ref, cast_out = refs[:n_cast], refs[n_cast], refs[n_cast + 1:]
    h = h_ref[...]
    hn = _rms(h, g_ref[...]).astype(BF16)
    a = jnp.dot(hn, wg_ref[...], preferred_element_type=F32)
    b = jnp.dot(hn, wu_ref[...], preferred_element_type=F32)
    o_ref[...] = h + _dot(_silu(a) * b, wd_ref[...])
    _cast_slices(cast_in, cast_out)


def _ffn(h, g, wg, wu, wd, cast=()):
    m, d = h.shape
    tm = min(TOK_TILE, m)
    steps = m // tm
    weights = (g, wg, wu, wd)
    cast_specs = _cast_specs(cast, steps)
    out = pl.pallas_call(
        _ffn_kernel,
        grid=(steps,),
        in_specs=[pl.BlockSpec((tm, d), lambda i: (i, 0))] + [_const(w.shape) for w in weights] + cast_specs,
        out_specs=[pl.BlockSpec((tm, d), lambda i: (i, 0))] + cast_specs,
        out_shape=[jax.ShapeDtypeStruct((m, d), F32)] + _bf16_shapes(cast),
        compiler_params=_params("arbitrary"),
        name="dense_ffn",
    )(h, *weights, *cast)
    return out if cast else out[0]


def _ple(h, p, g, w_gate, w_proj):
    return h + _sigmoid(_dot(_rms(h, g), w_gate)) * _dot(p, w_proj)


def _ple_kvq_kernel(h_ref, p_ref, gple_ref, wpg_ref, wpp_ref, gkv_ref, wk_ref, wv_ref, kn_ref,
                    gmix_ref, wq_ref, qn_ref, *refs):
    n_cast = (len(refs) - 4) // 2
    cast_in, (h3_ref, k_ref, v_ref, q_ref), cast_out = refs[:n_cast], refs[n_cast:n_cast + 4], refs[n_cast + 4:]
    _cast_slices(cast_in, cast_out)
    h3 = _ple(h_ref[...], p_ref[...], gple_ref[...], wpg_ref[...], wpp_ref[...])
    h3_ref[...] = h3
    s = _rms(h3, gkv_ref[...]).astype(BF16)
    k_ref[...] = _head_rms(jnp.dot(s, wk_ref[...], preferred_element_type=F32), kn_ref[...])
    v_ref[...] = jnp.dot(s, wv_ref[...], preferred_element_type=F32)
    q = _dot(_rms(h3, gmix_ref[...]), wq_ref[...])
    q_ref[...] = (_head_rms(q, qn_ref[...]) * (HEAD_DIM ** -0.5)).astype(BF16)


def _ple_kvq(h, p, gple, wpg, wpp, gkv, wk, wv, kn, gmix, wq, qn, cast=()):
    m, d = h.shape
    tm = min(TOK_TILE, m)
    steps = m // tm
    weights = (gple, wpg, wpp, gkv, wk, wv, kn, gmix, wq, qn)
    row = lambda w: pl.BlockSpec((tm, w), lambda i: (i, 0))
    cast_specs = _cast_specs(cast, steps)
    return pl.pallas_call(
        _ple_kvq_kernel,
        grid=(steps,),
        in_specs=[row(d), row(PLE_DIM)] + [_const(w.shape) for w in weights] + cast_specs,
        out_specs=[row(d), row(KV_DIM), row(KV_DIM), row(d)] + cast_specs,
        out_shape=[jax.ShapeDtypeStruct((m, d), F32), jax.ShapeDtypeStruct((m, KV_DIM), F32),
                   jax.ShapeDtypeStruct((m, KV_DIM), F32), jax.ShapeDtypeStruct((m, d), BF16)] + _bf16_shapes(cast),
        compiler_params=_params("arbitrary"),
        name="ple_kv_q",
    )(h, p, *weights, *cast)


def _softmax_sink(s, sink):
    m = jnp.maximum(jnp.max(s, axis=-1, keepdims=True), sink)
    e = jnp.exp(s - m)
    den = jnp.sum(e, axis=-1, keepdims=True) + jnp.exp(sink - m)
    return e / den


def _attn_prompt_kernel(q_ref, kp_ref, kc_ref, vp_ref, vc_ref, sink_ref, o_ref):
    n = pl.program_id(1)
    q = q_ref[0]
    k = jnp.concatenate([kp_ref[0], kc_ref[0]], axis=0).astype(BF16)
    v = jnp.concatenate([vp_ref[0], vc_ref[0]], axis=0).astype(BF16)
    qi = lax.broadcasted_iota(jnp.int32, (WINDOW, 2 * WINDOW), 0)
    kj = lax.broadcasted_iota(jnp.int32, (WINDOW, 2 * WINDOW), 1)
    dist = qi + WINDOW - kj
    first_key = jnp.where(n > 0, 0, WINDOW)
    mask = (dist >= 0) & (dist <= WINDOW) & (kj >= first_key)
    outs = []
    for h in range(N_HEADS):
        kh = h % N_KV_HEADS
        qh = q[:, h * HEAD_DIM:(h + 1) * HEAD_DIM]
        kk = k[:, kh * HEAD_DIM:(kh + 1) * HEAD_DIM]
        vv = v[:, kh * HEAD_DIM:(kh + 1) * HEAD_DIM]
        s = lax.dot_general(qh, kk, (((1,), (1,)), ((), ())), preferred_element_type=F32)
        s = jnp.where(mask, s, -jnp.inf)
        p = _softmax_sink(s, sink_ref[h])
        outs.append(jnp.dot(p.astype(BF16), vv, preferred_element_type=F32))
    o_ref[0] = jnp.concatenate(outs, axis=1).astype(BF16)


def _attn_prompt(q, k, v, sinks):
    n, t, d = q.shape
    nb = t // WINDOW
    cur = lambda w: pl.BlockSpec((1, WINDOW, w), lambda b, i: (b, i, 0))
    prev = lambda w: pl.BlockSpec((1, WINDOW, w), lambda b, i: (b, jnp.maximum(i - 1, 0), 0))
    return pl.pallas_call(
        _attn_prompt_kernel,
        grid=(n, nb),
        in_specs=[cur(d), prev(KV_DIM), cur(KV_DIM), prev(KV_DIM), cur(KV_DIM),
                  pl.BlockSpec(memory_space=pltpu.SMEM)],
        out_specs=cur(d),
        out_shape=jax.ShapeDtypeStruct((n, t, d), BF16),
        compiler_params=_params("arbitrary", "arbitrary"),
        name="attn_prompt",
    )(q, k, k, v, v, sinks)


def _attn_sample_kernel(q_ref, kc_ref, vc_ref, kn_ref, vn_ref, sink_ref, o_ref, ks_ref, vs_ref, *, nb):
    lane_head = lax.broadcasted_iota(jnp.int32, (N_KV_HEADS, KV_DIM), 1) // HEAD_DIM
    own = lane_head == lax.broadcasted_iota(jnp.int32, (N_KV_HEADS, KV_DIM), 0)
    sink = sink_ref[...]
    rows = []
    for i in range(nb):
        kc = kc_ref[i].T
        vc = vc_ref[i].T
        kn = kn_ref[i:i + 1, :]
        vn = vn_ref[i:i + 1, :]
        ks_ref[i, 0:WINDOW - 1, :] = kc[1:WINDOW, :]
        ks_ref[i, WINDOW - 1:WINDOW, :] = kn
        vs_ref[i, 0:WINDOW - 1, :] = vc[1:WINDOW, :]
        vs_ref[i, WINDOW - 1:WINDOW, :] = vn
        qrow = q_ref[i:i + 1, :].astype(F32)
        qx = jnp.concatenate(
            [jnp.where(own, jnp.broadcast_to(qrow[:, g * KV_DIM:(g + 1) * KV_DIM], (N_KV_HEADS, KV_DIM)), 0.0)
             for g in range(GROUP)], axis=0)
        s_c = lax.dot_general(qx.astype(BF16), kc.astype(BF16), (((1,), (1,)), ((), ())),
                              preferred_element_type=F32)
        s_n = jnp.sum(qx * kn.astype(BF16).astype(F32), axis=-1, keepdims=True)
        m = jnp.maximum(jnp.maximum(jnp.max(s_c, axis=-1, keepdims=True), s_n), sink)
        e_c = jnp.exp(s_c - m)
        e_n = jnp.exp(s_n - m)
        den = jnp.sum(e_c, axis=-1, keepdims=True) + e_n + jnp.exp(sink - m)
        ox = jnp.dot((e_c / den).astype(BF16), vc.astype(BF16), preferred_element_type=F32)
        ox = ox + (e_n / den).astype(BF16).astype(F32) * vn.astype(BF16).astype(F32)
        rows.append(jnp.concatenate(
            [jnp.sum(jnp.where(own, ox[g * N_KV_HEADS:(g + 1) * N_KV_HEADS], 0.0), axis=0, keepdims=True)
             for g in range(GROUP)], axis=1))
    o_ref[...] = jnp.concatenate(rows, axis=0).astype(BF16)


def _attn_sample(q, kc, vc, kn, vn, sinks_col):
    n, d = q.shape
    nb = 16
    row = lambda w: pl.BlockSpec((nb, w), lambda i: (i, 0))
    cache_in = pl.BlockSpec((nb, KV_DIM, WINDOW), lambda i: (i, 0, 0))
    cache = pl.BlockSpec((nb, WINDOW, KV_DIM), lambda i: (i, 0, 0))
    return pl.pallas_call(
        functools.partial(_attn_sample_kernel, nb=nb),
        grid=(n // nb,),
        in_specs=[row(d), cache_in, cache_in, row(KV_DIM), row(KV_DIM), _const(sinks_col.shape)],
        out_specs=[row(d), cache, cache],
        out_shape=[jax.ShapeDtypeStruct((n, d), BF16),
                   jax.ShapeDtypeStruct((n, WINDOW, KV_DIM), F32),
                   jax.ShapeDtypeStruct((n, WINDOW, KV_DIM), F32)],
        compiler_params=_params("arbitrary"),
        name="attn_sample",
    )(q, kc, vc, kn, vn, sinks_col)


def _oproj_router_kernel(rows_in_ref, o_ref, h_ref, wo_ref, g_ref, wr_ref, br_ref,
                         h4_ref, hn_ref, route_ref, sort_ref, cnt_ref, carry_ref, *, tm):
    del rows_in_ref
    i = pl.program_id(0)

    @pl.when(i == 0)
    def _():
        carry_ref[...] = jnp.zeros((1, LANES), F32)

    h4 = h_ref[...] + jnp.dot(o_ref[...], wo_ref[...], preferred_element_type=F32)
    h4_ref[...] = h4
    hn = _rms(h4, g_ref[...])
    _rows_to_tiles(hn_ref, hn)
    hn_hi = hn.astype(BF16)
    hn_lo = (hn - hn_hi.astype(F32)).astype(BF16)
    part_hi = jnp.dot(hn_hi, wr_ref[...], preferred_element_type=F32)
    part_lo = jnp.dot(hn_lo, wr_ref[...], preferred_element_type=F32)
    logits = (part_hi[:, :LANES] + (part_hi[:, LANES:] + part_lo[:, :LANES]) + part_lo[:, LANES:]) + br_ref[...]
    lane = lax.broadcasted_iota(jnp.int32, (tm, LANES), 1).astype(F32)
    m1 = jnp.max(logits, axis=-1, keepdims=True)
    i1 = jnp.min(jnp.where(logits == m1, lane, float(LANES)), axis=-1, keepdims=True)
    rest = jnp.where(lane == i1, -jnp.inf, logits)
    m2 = jnp.max(rest, axis=-1, keepdims=True)
    i2 = jnp.min(jnp.where(rest == m2, lane, float(LANES)), axis=-1, keepdims=True)
    e2 = jnp.exp(m2 - m1)
    w1 = 1.0 / (1.0 + e2)
    w2 = e2 / (1.0 + e2)
    oh1 = lane == i1
    oh2 = lane == i2
    sel = jnp.where(oh1, 1.0, 0.0) + jnp.where(oh2, 1.0, 0.0)
    row = lax.broadcasted_iota(jnp.int32, (tm, tm), 0)
    col = lax.broadcasted_iota(jnp.int32, (tm, tm), 1)
    tri = jnp.where(row > col, 1.0, 0.0).astype(BF16)
    before = jnp.dot(tri, sel.astype(BF16), preferred_element_type=F32) + carry_ref[...]
    r1 = jnp.sum(jnp.where(oh1, before, 0.0), axis=-1, keepdims=True)
    r2 = jnp.sum(jnp.where(oh2, before, 0.0), axis=-1, keepdims=True)
    carry_ref[...] = carry_ref[...] + jnp.sum(sel, axis=0, keepdims=True)
    cnt_ref[...] = carry_ref[...]
    fields = (i1, i2, r1, r2, w1, w2)
    route = jnp.zeros((tm, LANES), F32)
    for f, val in enumerate(fields):
        route = jnp.where(lane == float(f), val, route)
    route_ref[...] = route
    sort_ref[...] = route.T[0:SUBLANES, :]


def _oproj_router(rows_buf, o, h, wo, g, wr, br, *, first_token):
    m, d = h.shape
    tm = min(TOK_TILE, m)
    weights = (wo, g, wr, br)
    row = lambda w: pl.BlockSpec((tm, w), lambda i: (i, 0))
    first_block = first_token // tm
    return pl.pallas_call(
        functools.partial(_oproj_router_kernel, tm=tm),
        grid=(m // tm,),
        in_specs=[pl.BlockSpec(memory_space=pl.ANY), row(d), row(d)] + [_const(w.shape) for w in weights],
        out_specs=[row(d), pl.BlockSpec((tm * ROW_TILES, LANES), lambda i: (i + first_block, 0)), row(LANES),
                   pl.BlockSpec((SUBLANES, tm), lambda i: (0, i)), pl.BlockSpec((1, LANES), lambda i: (0, 0))],
        out_shape=[jax.ShapeDtypeStruct((m, d), F32), jax.ShapeDtypeStruct(rows_buf.shape, F32),
                   jax.ShapeDtypeStruct((m, LANES), F32), jax.ShapeDtypeStruct((SUBLANES, m), F32),
                   jax.ShapeDtypeStruct((1, LANES), F32)],
        scratch_shapes=[pltpu.VMEM((1, LANES), F32)],
        input_output_aliases={0: 1},
        compiler_params=_params("arbitrary"),
        name="oproj_router",
    )(rows_buf, o, h, *weights)


def _row(ref, r):
    return ref.at[pl.ds(pl.multiple_of(r * ROW_TILES, ROW_TILES), ROW_TILES), :]


def _for_rows(lo, hi, fn):
    assert (hi - lo) % COPY_UNROLL == 0

    def body(j, carry):
        for u in range(COPY_UNROLL):
            fn(lo + j * COPY_UNROLL + u)
        return carry
    lax.fori_loop(0, (hi - lo) // COPY_UNROLL, body, 0)


def _moe_kernel(te_ref, nv_ref, nr_ref, pos_ref, rows_ref, wg_ref, wu_ref, wd_ref, y_ref,
                inv_ref, xg_ref, yb_ref, xb_ref, acc_ref, gsem, ssem, isem, *, tm, nf, nt, n_pairs, pair_regions):
    i = pl.program_id(0)
    f = pl.program_id(1)
    nv = nv_ref[0]
    valid = i < nv
    slot = i % 2
    assert nf == 2

    def gather_row(tile, r, to_slot):
        pair = inv_ref[tile * tm + r]
        token = pair
        for first_pair, _, first_token in pair_regions[1:]:
            token = jnp.where(pair >= first_pair, pair - (first_pair - first_token), token)
        pltpu.make_async_copy(_row(rows_ref, token), _row(xg_ref.at[to_slot], r), gsem.at[to_slot]).start()

    def scatter_row(tile, real, r, from_slot):
        pair = jnp.where(r < real, inv_ref[tile * tm + r], n_pairs + from_slot * tm + r)
        pltpu.make_async_copy(_row(yb_ref.at[from_slot], r), _row(y_ref, pair), ssem.at[from_slot]).start()

    def wait_gather(at_slot):
        pltpu.make_async_copy(rows_ref.at[pl.ds(0, tm * ROW_TILES), :], xg_ref.at[at_slot], gsem.at[at_slot]).wait()

    def wait_scatter(at_slot):
        pltpu.make_async_copy(yb_ref.at[at_slot], y_ref.at[pl.ds(0, tm * ROW_TILES), :], ssem.at[at_slot]).wait()

    @pl.when(jnp.logical_and(i == 0, f == 0))
    def _():
        def place(p):
            inv_ref[pos_ref[p]] = p
        _for_rows(0, n_pairs, place)

        def clear_padding(t, carry):
            def clear(r, c):
                inv_ref[t * tm + r] = 0
                return c
            return lax.fori_loop(nr_ref[t], tm, clear, carry)
        lax.fori_loop(0, nv, clear_padding, 0)
        yb_ref[...] = jnp.zeros(yb_ref.shape, F32)
        spare = y_ref.at[pl.ds(n_pairs * ROW_TILES, 2 * tm * ROW_TILES), :]
        for s in range(2):
            half = spare.at[pl.ds(s * tm * ROW_TILES, tm * ROW_TILES), :]
            pltpu.make_async_copy(yb_ref.at[s], half, isem).start()
            pltpu.make_async_copy(yb_ref.at[s], half, isem).wait()
        _for_rows(0, tm, lambda r: gather_row(0, r, 0))

    @pl.when(jnp.logical_and(valid, f == 0))
    def _():
        wait_gather(slot)
        xb_ref[...] = _tiles_to_rows(xg_ref.at[slot], tm).astype(BF16)
        acc_ref[...] = jnp.zeros((tm, D_MODEL), F32)

    def matmuls(first):
        if first:
            nxt = jnp.minimum(i + 1, nv - 1)
            for r in range(tm):
                gather_row(nxt, r, 1 - slot)
        else:
            prev = jnp.maximum(i - 1, 0)
            real = jnp.where(i >= 1, nr_ref[prev], 0)
            for r in range(tm):
                scatter_row(prev, real, r, 1 - slot)
        x = xb_ref[...]
        a = jnp.dot(x, wg_ref[0], preferred_element_type=F32)
        b = jnp.dot(x, wu_ref[0], preferred_element_type=F32)
        acc_ref[...] += _dot(_silu(a) * b, wd_ref[0])

    @pl.when(jnp.logical_and(valid, f == 0))
    def _():
        matmuls(True)

    @pl.when(jnp.logical_and(valid, f > 0))
    def _():
        matmuls(False)

    @pl.when(jnp.logical_and(valid, f == nf - 1))
    def _():
        @pl.when(i >= 1)
        def _():
            wait_scatter(slot)
        _rows_to_tiles(yb_ref.at[slot], acc_ref[...])

        @pl.when(i == nv - 1)
        def _():
            _for_rows(0, tm, lambda r: scatter_row(i, nr_ref[i], r, slot))

    @pl.when(jnp.logical_and(i == nt - 1, f == nf - 1))
    def _():
        wait_gather(nv % 2)
        wait_scatter(nv % 2)
        wait_scatter((nv - 1) % 2)


def _moe(tile_expert, n_valid, tile_rows, pos, pair_regions, rows, wg, wu, wd):
    tm, tf = MOE_TILE, MOE_F_TILE
    nt, nf = tile_expert.shape[0], D_EXPERT // tf
    n_pairs = pos.shape[0]

    def fidx(i, f, nv):
        return jnp.where(i < nv[0], f, nf - 1)
    tile_buf = pltpu.VMEM((2, tm * ROW_TILES, LANES), F32)
    return pl.pallas_call(
        functools.partial(_moe_kernel, tm=tm, nf=nf, nt=nt, n_pairs=n_pairs, pair_regions=pair_regions),
        grid_spec=pltpu.PrefetchScalarGridSpec(
            num_scalar_prefetch=4,
            grid=(nt, nf),
            in_specs=[pl.BlockSpec(memory_space=pl.ANY),
                      pl.BlockSpec((1, D_MODEL, tf), lambda i, f, te, nv, *_: (te[i], 0, fidx(i, f, nv))),
                      pl.BlockSpec((1, D_MODEL, tf), lambda i, f, te, nv, *_: (te[i], 0, fidx(i, f, nv))),
                      pl.BlockSpec((1, tf, D_MODEL), lambda i, f, te, nv, *_: (te[i], fidx(i, f, nv), 0))],
            out_specs=pl.BlockSpec(memory_space=pl.ANY),
            scratch_shapes=[pltpu.SMEM((nt * tm,), jnp.int32), tile_buf, tile_buf,
                            pltpu.VMEM((tm, D_MODEL), BF16), pltpu.VMEM((tm, D_MODEL), F32),
                            pltpu.SemaphoreType.DMA((2,)), pltpu.SemaphoreType.DMA((2,)),
                            pltpu.SemaphoreType.DMA(())],
        ),
        out_shape=jax.ShapeDtypeStruct(((n_pairs + 2 * tm) * ROW_TILES, LANES), F32),
        compiler_params=_params("arbitrary", "arbitrary"),
        name="moe_grouped",
    )(tile_expert, n_valid, tile_rows, pos, rows, wg, wu, wd)


def _combine_kernel(y1_ref, y2_ref, h_ref, route_ref, p_ref, g_ref, wpg_ref, wpp_ref, o_ref, *, tb):
    route = route_ref[...]
    w1 = route[:, 4:5]
    w2 = route[:, 5:6]
    h5 = h_ref[...] + (w1 * _tiles_to_rows(y1_ref, tb) + w2 * _tiles_to_rows(y2_ref, tb))
    o_ref[...] = _ple(h5, p_ref[...], g_ref[...], wpg_ref[...], wpp_ref[...])


def _combine(y_pairs, h, route, p, g, wpg, wpp, *, first_pair):
    m, d = h.shape
    tb = min(COMBINE_TILE, m)
    weights = (g, wpg, wpp)
    row = lambda w: pl.BlockSpec((tb, w), lambda i: (i, 0))
    steps = m // tb
    first_block = first_pair // tb
    pair_rows = lambda slot: pl.BlockSpec((tb * ROW_TILES, LANES), lambda i: (i + first_block + slot * steps, 0))
    return pl.pallas_call(
        functools.partial(_combine_kernel, tb=tb),
        grid=(steps,),
        in_specs=[pair_rows(0), pair_rows(1), row(d), row(LANES),
                  pl.BlockSpec((tb, PLE_DIM), lambda i: (i + steps, 0))] + [_const(w.shape) for w in weights],
        out_specs=row(d),
        out_shape=jax.ShapeDtypeStruct((m, d), F32),
        compiler_params=_params("arbitrary"),
        name="moe_combine",
    )(y_pairs, y_pairs, h, route, p, *weights)


def _row2(v):
    return v.reshape(1, -1).astype(F32)


def kernel(x_prompt, x_sample, state_conv, cache_k, cache_v, p_prompt, p_sample, norm_mix, norm_ffn, norm_ple,
           conv_w_pw1, conv_b_pw1, conv_w_dw, conv_b_dw, conv_ln_g, conv_ln_b, conv_w_pw2, conv_b_pw2,
           norm_kv, w_k, w_v, k_norm, w_q, q_norm, sinks, w_o, ffn_w_gate, ffn_w_up, ffn_w_down,
           moe_w_router, moe_b_router, moe_w_gate, moe_w_up, moe_w_down, ple_w_gate, ple_w_proj):
    nb, t, d = x_prompt.shape
    ns = x_sample.shape[0]
    mp = nb * t
    bf = lambda w: w.astype(BF16)

    conv_w = (_row2(norm_mix[0]), bf(conv_w_pw1[0]), _row2(conv_b_pw1[0]), conv_w_dw[0], _row2(conv_b_dw[0]),
              _row2(conv_ln_g[0]), _row2(conv_ln_b[0]), bf(conv_w_pw2[0]), _row2(conv_b_pw2[0]))
    wq = w_q[0].reshape(d, N_KV_HEADS, GROUP, HEAD_DIM).transpose(0, 2, 1, 3).reshape(d, d)
    wo = w_o[0].reshape(N_KV_HEADS, GROUP, HEAD_DIM, d).transpose(1, 0, 2, 3).reshape(d, d)
    sinks_gk = sinks[0].astype(F32).reshape(N_KV_HEADS, GROUP).T.reshape(N_HEADS)
    wr = jnp.zeros((d, LANES), F32).at[:, :N_EXPERTS].set(moe_w_router[0])
    wr_hi = bf(wr)
    wr_lo = bf(wr - wr_hi.astype(F32))
    br = jnp.full((1, LANES), NEG_BIG, F32).at[0, :N_EXPERTS].set(moe_b_router[0])
    pp = p_prompt.reshape(-1, PLE_DIM)
    ps = p_sample.reshape(-1, PLE_DIM)
    flat = lambda w: w.reshape(-1, w.shape[-1])

    h1p, conv_p, ffn_g, ffn_u, ffn_d, moe_g, moe_d = _conv_prompt(
        x_prompt, *conv_w,
        cast=(ffn_w_gate[0], ffn_w_up[0], ffn_w_down[0], flat(moe_w_gate[0]), flat(moe_w_down[0])))
    ffn_w = (_row2(norm_ffn[0]), ffn_g, ffn_u, ffn_d)
    h2p, ple_g0, ple_p0, wk_bf, wv_bf, wq_bf, moe_u = _ffn(
        h1p.reshape(mp, d), *ffn_w, cast=(ple_w_gate[0], ple_w_proj[0], w_k, w_v, wq, flat(moe_w_up[0])))
    kvq_w = (_row2(norm_ple[0]), ple_g0, ple_p0, _row2(norm_kv), wk_bf, wv_bf,
             _row2(jnp.tile(k_norm, N_KV_HEADS)), _row2(norm_mix[1]), wq_bf, _row2(jnp.tile(q_norm[0], N_HEADS)))
    h3p, kp, vp, qp, wo_bf, ple_g1, ple_p1 = _ple_kvq(
        h2p, pp, *kvq_w, cast=(wo, ple_w_gate[1], ple_w_proj[1]))
    router_w = (wo_bf, _row2(norm_ffn[1]), jnp.concatenate([wr_hi, wr_lo], axis=1), br)
    ple1_w = (_row2(norm_ple[1]), ple_g1, ple_p1)
    expert_bf = [b.reshape(w.shape[1:]) for b, w in zip((moe_g, moe_u, moe_d), (moe_w_gate, moe_w_up, moe_w_down))]
    h1s, conv_s = _conv_sample(x_sample.reshape(ns, d), state_conv.transpose(0, 2, 1, 3), *conv_w)
    conv_s = conv_s.transpose(0, 2, 1, 3)
    h3s, ks, vs, qs = _ple_kvq(_ffn(h1s, *ffn_w), ps, *kvq_w)[:4]

    kp3 = kp.reshape(nb, t, KV_DIM)
    vp3 = vp.reshape(nb, t, KV_DIM)
    op = _attn_prompt(qp.reshape(nb, t, d), kp3, vp3, sinks_gk)
    feature_major = lambda c: c.transpose(0, 2, 3, 1).reshape(ns, KV_DIM, WINDOW)
    os_, k_state_s, v_state_s = _attn_sample(qs, feature_major(cache_k), feature_major(cache_v), ks, vs,
                                             sinks_gk.reshape(N_HEADS, 1))

    mt = mp + ns
    rows = jnp.zeros((mt * ROW_TILES, LANES), F32)
    h4p, rows, route_p, sort_p, cnt_p = _oproj_router(rows, op.reshape(mp, d), h3p, *router_w, first_token=0)
    h4s, rows, route_s, sort_s, cnt_s = _oproj_router(rows, os_, h3s, *router_w, first_token=mp)

    cnt_p = cnt_p[0, :N_EXPERTS].astype(jnp.int32)
    cnt_s = cnt_s[0, :N_EXPERTS].astype(jnp.int32)
    group = (cnt_p + cnt_s + MOE_TILE - 1) // MOE_TILE * MOE_TILE
    ends = jnp.cumsum(group)
    starts = ends - group
    n_tiles = (2 * (mp + ns) + N_EXPERTS * (MOE_TILE - 1)) // MOE_TILE
    n_valid = (ends[-1] // MOE_TILE).astype(jnp.int32)
    tile_start = jnp.minimum(jnp.arange(n_tiles, dtype=jnp.int32), n_valid - 1) * MOE_TILE
    tile_expert = jnp.sum(tile_start[:, None] >= ends[None, :], axis=1).astype(jnp.int32)

    def positions(sort_info, slot, base):
        idx = sort_info[slot].astype(jnp.int32)
        rank = sort_info[2 + slot].astype(jnp.int32)
        return (starts + base)[idx] + rank
    no_base = jnp.zeros_like(cnt_p)
    pos = jnp.concatenate([positions(sort_p, 0, no_base), positions(sort_p, 1, no_base),
                           positions(sort_s, 0, cnt_p), positions(sort_s, 1, cnt_p)]).astype(jnp.int32)
    pair_regions = ((0, mp, 0), (mp, mp, 0), (2 * mp, ns, mp), (2 * mp + ns, ns, mp))
    tile_ids = jnp.arange(n_tiles, dtype=jnp.int32)
    rows_left = (starts + cnt_p + cnt_s)[tile_expert] - tile_ids * MOE_TILE
    tile_rows = jnp.where(tile_ids < n_valid, jnp.clip(rows_left, 0, MOE_TILE), 0).astype(jnp.int32)
    y_pairs = _moe(tile_expert, n_valid.reshape(1), tile_rows, pos, pair_regions, rows,
                   *expert_bf)

    y_p = _combine(y_pairs, h4p, route_p, pp, *ple1_w, first_pair=0)
    y_s = _combine(y_pairs, h4s, route_s, ps, *ple1_w, first_pair=2 * mp)

    kv4 = lambda a, n: a.reshape(n, WINDOW, N_KV_HEADS, HEAD_DIM)
    return (y_p.reshape(nb, t, d), y_s.reshape(ns, 1, d), conv_p,
            kv4(kp3[:, t - WINDOW:], nb), kv4(vp3[:, t - WINDOW:], nb),
            conv_s, kv4(k_state_s, ns), kv4(v_state_s, ns))
```

```python
import functools

import jax
import jax.numpy as jnp
from jax import lax
from jax.experimental import pallas as pl
from jax.experimental.pallas import tpu as pltpu

D_MODEL = 1024
CONV_WIDTH = 31
CONV_CTX = CONV_WIDTH - 1
HEAD_DIM = 64
N_HEADS = 16
N_KV_HEADS = 4
GROUP = N_HEADS // N_KV_HEADS
KV_DIM = N_KV_HEADS * HEAD_DIM
WINDOW = 128
D_FF = 2816
N_EXPERTS = 8
D_EXPERT = 3584
PLE_DIM = 256
EPS = 1e-6

BF16 = jnp.bfloat16
F32 = jnp.float32

LANES = 128
SUBLANES = 8
ROW_TILES = D_MODEL // LANES
VMEM_LIMIT = 56 * 1024 * 1024

CONV_PAD = 32
CONV_ROWS = 128
TOK_TILE = 512
FFN_TILE = 256
MOE_TILE = 512
MOE_F_TILE = 1792
COMBINE_TILE = 512
COPY_UNROLL = 8
NEG_BIG = -1e30


def _params(*sem):
    return pltpu.CompilerParams(dimension_semantics=sem, vmem_limit_bytes=VMEM_LIMIT)


def _const(shape):
    nd = len(shape)
    return pl.BlockSpec(shape, lambda *_: (0,) * nd, pipeline_mode=pl.Buffered(1))


def _rms(x, g):
    return x * lax.rsqrt(jnp.mean(x * x, axis=-1, keepdims=True) + EPS) * g


def _dot(a, w):
    return jnp.dot(a.astype(BF16), w, preferred_element_type=F32)


def _sigmoid(x):
    return 1.0 / (1.0 + jnp.exp(-x))


def _silu(x):
    return x * _sigmoid(x)


def _layer_norm(x, g, b):
    mu = jnp.mean(x, axis=-1, keepdims=True)
    xc = x - mu
    var = jnp.mean(xc * xc, axis=-1, keepdims=True)
    return xc * lax.rsqrt(var + EPS) * g + b


def _head_rms(x, g):
    t, w = x.shape
    lane = lax.broadcasted_iota(jnp.int32, (t, LANES), 1)
    lo = lane < HEAD_DIM
    outs = []
    for c in range(w // LANES):
        blk = x[:, c * LANES:(c + 1) * LANES]
        sq = blk * blk
        s_lo = jnp.sum(jnp.where(lo, sq, 0.0), axis=-1, keepdims=True)
        s_hi = jnp.sum(jnp.where(lo, 0.0, sq), axis=-1, keepdims=True)
        inv = jnp.where(lo, lax.rsqrt(s_lo / HEAD_DIM + EPS), lax.rsqrt(s_hi / HEAD_DIM + EPS))
        outs.append(blk * inv)
    return jnp.concatenate(outs, axis=1) * g


def _rows_to_tiles(ref, x):
    t = x.shape[0]
    for s in range(ROW_TILES):
        ref[pl.ds(s, t, stride=ROW_TILES), :] = x[:, s * LANES:(s + 1) * LANES]


def _tiles_to_rows(ref, t):
    return jnp.concatenate([ref[pl.ds(s, t, stride=ROW_TILES), :] for s in range(ROW_TILES)], axis=1)


def _conv_taps(uext_ref, wdw_ref, bdw_ref, c_ref, tt):
    win_rows = CONV_ROWS + CONV_PAD

    def chunk(ci, carry):
        r0 = pl.multiple_of(ci * CONV_ROWS, CONV_ROWS)
        for lt in range(ROW_TILES):
            cols = slice(lt * LANES, (lt + 1) * LANES)
            win = uext_ref[pl.ds(r0, win_rows), cols]
            acc = jnp.broadcast_to(bdw_ref[:, cols], (CONV_ROWS, LANES))
            for b in range(SUBLANES):
                wb = win if b == 0 else pltpu.roll(win, win_rows - b, axis=0)
                for a in range(CONV_PAD // SUBLANES + 1):
                    j = a * SUBLANES + b - (CONV_PAD - CONV_CTX)
                    if 0 <= j < CONV_WIDTH:
                        acc = acc + wdw_ref[j:j + 1, cols] * wb[a * SUBLANES:a * SUBLANES + CONV_ROWS]
            c_ref[pl.ds(r0, CONV_ROWS), cols] = acc
        return carry
    lax.fori_loop(0, tt // CONV_ROWS, chunk, 0)


def _conv_prompt_kernel(x_ref, nm_ref, w1_ref, b1_ref, wdw_ref, bdw_ref, lng_ref, lnb_ref, w2_ref, b2_ref,
                        h_ref, st_ref, uext_ref, c_ref, *, tt):
    t = pl.program_id(1)

    @pl.when(t == 0)
    def _():
        uext_ref[0:CONV_PAD, :] = jnp.zeros((CONV_PAD, D_MODEL), F32)

    @pl.when(t > 0)
    def _():
        uext_ref[0:CONV_PAD, :] = uext_ref[tt:tt + CONV_PAD, :]

    x = x_ref[0]
    z = _dot(_rms(x, nm_ref[...]), w1_ref[...]) + b1_ref[...]
    u = z[:, :D_MODEL] * _sigmoid(z[:, D_MODEL:])
    uext_ref[CONV_PAD:CONV_PAD + tt, :] = u
    _conv_taps(uext_ref, wdw_ref, bdw_ref, c_ref, tt)
    c = _silu(_layer_norm(c_ref[...], lng_ref[...], lnb_ref[...]))
    h_ref[0] = x + _dot(c, w2_ref[...]) + b2_ref[...]
    st_ref[0, 0] = uext_ref[CONV_PAD + tt - CONV_CTX:CONV_PAD + tt, :]


def _conv_prompt(x, nm, w1, b1, wdw, bdw, lng, lnb, w2, b2):
    n, t, d = x.shape
    tt = TOK_TILE
    weights = (nm, w1, b1, wdw, bdw, lng, lnb, w2, b2)
    return pl.pallas_call(
        functools.partial(_conv_prompt_kernel, tt=tt),
        grid=(n, t // tt),
        in_specs=[pl.BlockSpec((1, tt, d), lambda b, i: (b, i, 0))] + [_const(w.shape) for w in weights],
        out_specs=[pl.BlockSpec((1, tt, d), lambda b, i: (b, i, 0)),
                   pl.BlockSpec((1, 1, CONV_CTX, d), lambda b, i: (0, b, 0, 0))],
        out_shape=[jax.ShapeDtypeStruct((n, t, d), F32), jax.ShapeDtypeStruct((1, n, CONV_CTX, d), F32)],
        scratch_shapes=[pltpu.VMEM((CONV_PAD + tt, d), F32), pltpu.VMEM((tt, d), F32)],
        compiler_params=_params("arbitrary", "arbitrary"),
        name="conv_prompt",
    )(x, *weights)


def _conv_sample_kernel(x_ref, past_ref, nm_ref, w1_ref, b1_ref, wdw_ref, bdw_ref, lng_ref, lnb_ref,
                        w2_ref, b2_ref, h_ref, st_ref):
    x = x_ref[...]
    z = _dot(_rms(x, nm_ref[...]), w1_ref[...]) + b1_ref[...]
    u = z[:, :D_MODEL] * _sigmoid(z[:, D_MODEL:])
    c = wdw_ref[CONV_CTX:CONV_WIDTH, :] * u + bdw_ref[...]
    for j in range(CONV_CTX):
        c = c + wdw_ref[j:j + 1, :] * past_ref[0, j]
        if j > 0:
            st_ref[0, j - 1] = past_ref[0, j]
    st_ref[0, CONV_CTX - 1] = u
    c = _silu(_layer_norm(c, lng_ref[...], lnb_ref[...]))
    h_ref[...] = x + _dot(c, w2_ref[...]) + b2_ref[...]


def _conv_sample(x, past, nm, w1, b1, wdw, bdw, lng, lnb, w2, b2):
    n, d = x.shape
    nb = 16
    weights = (nm, w1, b1, wdw, bdw, lng, lnb, w2, b2)
    state = pl.BlockSpec((1, CONV_CTX, nb, d), lambda i: (0, 0, i, 0))
    return pl.pallas_call(
        _conv_sample_kernel,
        grid=(n // nb,),
        in_specs=[pl.BlockSpec((nb, d), lambda i: (i, 0)), state] + [_const(w.shape) for w in weights],
        out_specs=[pl.BlockSpec((nb, d), lambda i: (i, 0)), state],
        out_shape=[jax.ShapeDtypeStruct((n, d), F32), jax.ShapeDtypeStruct((1, CONV_CTX, n, d), F32)],
        compiler_params=_params("arbitrary"),
        name="conv_sample",
    )(x, past, *weights)


def _ffn_kernel(h_ref, g_ref, wg_ref, wu_ref, wd_ref, *refs):
    n_cast = len(refs) // 2
    cast_in, o_ref, cast_out = refs[:n_cast], refs[n_cast], refs[n_cast + 1:]
    h = h_ref[...]
    hn = _rms(h, g_ref[...]).astype(BF16)
    a = jnp.dot(hn, wg_ref[...], preferred_element_type=F32)
    b = jnp.dot(hn, wu_ref[...], preferred_element_type=F32)
    o_ref[...] = h + _dot(_silu(a) * b, wd_ref[...])
    for src, dst in zip(cast_in, cast_out):
        dst[...] = src[...].astype(BF16)


def _ffn(h, g, wg, wu, wd, cast=()):
    m, d = h.shape
    tm = min(FFN_TILE if cast else TOK_TILE, m)
    steps = m // tm
    weights = (g, wg, wu, wd)
    cast_specs = [pl.BlockSpec((c.shape[0] // steps, c.shape[1]), lambda i: (i, 0)) for c in cast]
    out = pl.pallas_call(
        _ffn_kernel,
        grid=(steps,),
        in_specs=[pl.BlockSpec((tm, d), lambda i: (i, 0))] + [_const(w.shape) for w in weights] + cast_specs,
        out_specs=[pl.BlockSpec((tm, d), lambda i: (i, 0))] + cast_specs,
        out_shape=[jax.ShapeDtypeStruct((m, d), F32)] + [jax.ShapeDtypeStruct(c.shape, BF16) for c in cast],
        compiler_params=_params("arbitrary"),
        name="dense_ffn",
    )(h, *weights, *cast)
    return out if cast else out[0]


def _ple(h, p, g, w_gate, w_proj):
    return h + _sigmoid(_dot(_rms(h, g), w_gate)) * _dot(p, w_proj)


def _ple_kvq_kernel(h_ref, p_ref, gple_ref, wpg_ref, wpp_ref, gkv_ref, wk_ref, wv_ref, kn_ref,
                    gmix_ref, wq_ref, qn_ref, h3_ref, k_ref, v_ref, q_ref):
    h3 = _ple(h_ref[...], p_ref[...], gple_ref[...], wpg_ref[...], wpp_ref[...])
    h3_ref[...] = h3
    s = _rms(h3, gkv_ref[...]).astype(BF16)
    k_ref[...] = _head_rms(jnp.dot(s, wk_ref[...], preferred_element_type=F32), kn_ref[...])
    v_ref[...] = jnp.dot(s, wv_ref[...], preferred_element_type=F32)
    q = _dot(_rms(h3, gmix_ref[...]), wq_ref[...])
    q_ref[...] = (_head_rms(q, qn_ref[...]) * (HEAD_DIM ** -0.5)).astype(BF16)


def _ple_kvq(h, p, gple, wpg, wpp, gkv, wk, wv, kn, gmix, wq, qn):
    m, d = h.shape
    tm = min(TOK_TILE, m)
    weights = (gple, wpg, wpp, gkv, wk, wv, kn, gmix, wq, qn)
    row = lambda w: pl.BlockSpec((tm, w), lambda i: (i, 0))
    return pl.pallas_call(
        _ple_kvq_kernel,
        grid=(m // tm,),
        in_specs=[row(d), row(PLE_DIM)] + [_const(w.shape) for w in weights],
        out_specs=[row(d), row(KV_DIM), row(KV_DIM), row(d)],
        out_shape=[jax.ShapeDtypeStruct((m, d), F32), jax.ShapeDtypeStruct((m, KV_DIM), F32),
                   jax.ShapeDtypeStruct((m, KV_DIM), F32), jax.ShapeDtypeStruct((m, d), BF16)],
        compiler_params=_params("arbitrary"),
        name="ple_kv_q",
    )(h, p, *weights)


def _softmax_sink(s, sink):
    m = jnp.maximum(jnp.max(s, axis=-1, keepdims=True), sink)
    e = jnp.exp(s - m)
    den = jnp.sum(e, axis=-1, keepdims=True) + jnp.exp(sink - m)
    return e / den


def _attn_prompt_kernel(q_ref, kp_ref, kc_ref, vp_ref, vc_ref, sink_ref, o_ref):
    n = pl.program_id(1)
    q = q_ref[0]
    k = jnp.concatenate([kp_ref[0], kc_ref[0]], axis=0).astype(BF16)
    v = jnp.concatenate([vp_ref[0], vc_ref[0]], axis=0).astype(BF16)
    qi = lax.broadcasted_iota(jnp.int32, (WINDOW, 2 * WINDOW), 0)
    kj = lax.broadcasted_iota(jnp.int32, (WINDOW, 2 * WINDOW), 1)
    dist = qi + WINDOW - kj
    first_key = jnp.where(n > 0, 0, WINDOW)
    mask = (dist >= 0) & (dist <= WINDOW) & (kj >= first_key)
    outs = []
    for h in range(N_HEADS):
        kh = h % N_KV_HEADS
        qh = q[:, h * HEAD_DIM:(h + 1) * HEAD_DIM]
        kk = k[:, kh * HEAD_DIM:(kh + 1) * HEAD_DIM]
        vv = v[:, kh * HEAD_DIM:(kh + 1) * HEAD_DIM]
        s = lax.dot_general(qh, kk, (((1,), (1,)), ((), ())), preferred_element_type=F32)
        s = jnp.where(mask, s, -jnp.inf)
        p = _softmax_sink(s, sink_ref[h])
        outs.append(jnp.dot(p.astype(BF16), vv, preferred_element_type=F32))
    o_ref[0] = jnp.concatenate(outs, axis=1).astype(BF16)


def _attn_prompt(q, k, v, sinks):
    n, t, d = q.shape
    nb = t // WINDOW
    cur = lambda w: pl.BlockSpec((1, WINDOW, w), lambda b, i: (b, i, 0))
    prev = lambda w: pl.BlockSpec((1, WINDOW, w), lambda b, i: (b, jnp.maximum(i - 1, 0), 0))
    return pl.pallas_call(
        _attn_prompt_kernel,
        grid=(n, nb),
        in_specs=[cur(d), prev(KV_DIM), cur(KV_DIM), prev(KV_DIM), cur(KV_DIM),
                  pl.BlockSpec(memory_space=pltpu.SMEM)],
        out_specs=cur(d),
        out_shape=jax.ShapeDtypeStruct((n, t, d), BF16),
        compiler_params=_params("arbitrary", "arbitrary"),
        name="attn_prompt",
    )(q, k, k, v, v, sinks)


def _attn_sample_kernel(q_ref, kc_ref, vc_ref, kn_ref, vn_ref, sink_ref, o_ref, ks_ref, vs_ref, *, nb):
    lane_head = lax.broadcasted_iota(jnp.int32, (N_KV_HEADS, KV_DIM), 1) // HEAD_DIM
    own = lane_head == lax.broadcasted_iota(jnp.int32, (N_KV_HEADS, KV_DIM), 0)
    sink = sink_ref[...]
    rows = []
    for i in range(nb):
        kc = kc_ref[i].T
        vc = vc_ref[i].T
        kn = kn_ref[i:i + 1, :]
        vn = vn_ref[i:i + 1, :]
        ks_ref[i, 0:WINDOW - 1, :] = kc[1:WINDOW, :]
        ks_ref[i, WINDOW - 1:WINDOW, :] = kn
        vs_ref[i, 0:WINDOW - 1, :] = vc[1:WINDOW, :]
        vs_ref[i, WINDOW - 1:WINDOW, :] = vn
        qrow = q_ref[i:i + 1, :].astype(F32)
        qx = jnp.concatenate(
            [jnp.where(own, jnp.broadcast_to(qrow[:, g * KV_DIM:(g + 1) * KV_DIM], (N_KV_HEADS, KV_DIM)), 0.0)
             for g in range(GROUP)], axis=0)
        s_c = lax.dot_general(qx.astype(BF16), kc.astype(BF16), (((1,), (1,)), ((), ())),
                              preferred_element_type=F32)
        s_n = jnp.sum(qx * kn.astype(BF16).astype(F32), axis=-1, keepdims=True)
        m = jnp.maximum(jnp.maximum(jnp.max(s_c, axis=-1, keepdims=True), s_n), sink)
        e_c = jnp.exp(s_c - m)
        e_n = jnp.exp(s_n - m)
        den = jnp.sum(e_c, axis=-1, keepdims=True) + e_n + jnp.exp(sink - m)
        ox = jnp.dot((e_c / den).astype(BF16), vc.astype(BF16), preferred_element_type=F32)
        ox = ox + (e_n / den).astype(BF16).astype(F32) * vn.astype(BF16).astype(F32)
        rows.append(jnp.concatenate(
            [jnp.sum(jnp.where(own, ox[g * N_KV_HEADS:(g + 1) * N_KV_HEADS], 0.0), axis=0, keepdims=True)
             for g in range(GROUP)], axis=1))
    o_ref[...] = jnp.concatenate(rows, axis=0).astype(BF16)


def _attn_sample(q, kc, vc, kn, vn, sinks_col):
    n, d = q.shape
    nb = 16
    row = lambda w: pl.BlockSpec((nb, w), lambda i: (i, 0))
    cache_in = pl.BlockSpec((nb, KV_DIM, WINDOW), lambda i: (i, 0, 0))
    cache = pl.BlockSpec((nb, WINDOW, KV_DIM), lambda i: (i, 0, 0))
    return pl.pallas_call(
        functools.partial(_attn_sample_kernel, nb=nb),
        grid=(n // nb,),
        in_specs=[row(d), cache_in, cache_in, row(KV_DIM), row(KV_DIM), _const(sinks_col.shape)],
        out_specs=[row(d), cache, cache],
        out_shape=[jax.ShapeDtypeStruct((n, d), BF16),
                   jax.ShapeDtypeStruct((n, WINDOW, KV_DIM), F32),
                   jax.ShapeDtypeStruct((n, WINDOW, KV_DIM), F32)],
        compiler_params=_params("arbitrary"),
        name="attn_sample",
    )(q, kc, vc, kn, vn, sinks_col)


def _oproj_router_kernel(rows_in_ref, o_ref, h_ref, wo_ref, g_ref, wr_ref, br_ref,
                         h4_ref, hn_ref, route_ref, sort_ref, cnt_ref, carry_ref, *, tm):
    del rows_in_ref
    i = pl.program_id(0)

    @pl.when(i == 0)
    def _():
        carry_ref[...] = jnp.zeros((1, LANES), F32)

    h4 = h_ref[...] + jnp.dot(o_ref[...], wo_ref[...], preferred_element_type=F32)
    h4_ref[...] = h4
    hn = _rms(h4, g_ref[...])
    _rows_to_tiles(hn_ref, hn)
    hn_hi = hn.astype(BF16)
    hn_lo = (hn - hn_hi.astype(F32)).astype(BF16)
    part_hi = jnp.dot(hn_hi, wr_ref[...], preferred_element_type=F32)
    part_lo = jnp.dot(hn_lo, wr_ref[...], preferred_element_type=F32)
    logits = (part_hi[:, :LANES] + (part_hi[:, LANES:] + part_lo[:, :LANES]) + part_lo[:, LANES:]) + br_ref[...]
    lane = lax.broadcasted_iota(jnp.int32, (tm, LANES), 1).astype(F32)
    m1 = jnp.max(logits, axis=-1, keepdims=True)
    i1 = jnp.min(jnp.where(logits == m1, lane, float(LANES)), axis=-1, keepdims=True)
    rest = jnp.where(lane == i1, -jnp.inf, logits)
    m2 = jnp.max(rest, axis=-1, keepdims=True)
    i2 = jnp.min(jnp.where(rest == m2, lane, float(LANES)), axis=-1, keepdims=True)
    e2 = jnp.exp(m2 - m1)
    w1 = 1.0 / (1.0 + e2)
    w2 = e2 / (1.0 + e2)
    oh1 = lane == i1
    oh2 = lane == i2
    sel = jnp.where(oh1, 1.0, 0.0) + jnp.where(oh2, 1.0, 0.0)
    row = lax.broadcasted_iota(jnp.int32, (tm, tm), 0)
    col = lax.broadcasted_iota(jnp.int32, (tm, tm), 1)
    tri = jnp.where(row > col, 1.0, 0.0).astype(BF16)
    before = jnp.dot(tri, sel.astype(BF16), preferred_element_type=F32) + carry_ref[...]
    r1 = jnp.sum(jnp.where(oh1, before, 0.0), axis=-1, keepdims=True)
    r2 = jnp.sum(jnp.where(oh2, before, 0.0), axis=-1, keepdims=True)
    carry_ref[...] = carry_ref[...] + jnp.sum(sel, axis=0, keepdims=True)
    cnt_ref[...] = carry_ref[...]
    fields = (i1, i2, r1, r2, w1, w2)
    route = jnp.zeros((tm, LANES), F32)
    for f, val in enumerate(fields):
        route = jnp.where(lane == float(f), val, route)
    route_ref[...] = route
    sort_ref[...] = route.T[0:SUBLANES, :]


def _oproj_router(rows_buf, o, h, wo, g, wr, br, *, first_token):
    m, d = h.shape
    tm = min(TOK_TILE, m)
    weights = (wo, g, wr, br)
    row = lambda w: pl.BlockSpec((tm, w), lambda i: (i, 0))
    first_block = first_token // tm
    return pl.pallas_call(
        functools.partial(_oproj_router_kernel, tm=tm),
        grid=(m // tm,),
        in_specs=[pl.BlockSpec(memory_space=pl.ANY), row(d), row(d)] + [_const(w.shape) for w in weights],
        out_specs=[row(d), pl.BlockSpec((tm * ROW_TILES, LANES), lambda i: (i + first_block, 0)), row(LANES),
                   pl.BlockSpec((SUBLANES, tm), lambda i: (0, i)), pl.BlockSpec((1, LANES), lambda i: (0, 0))],
        out_shape=[jax.ShapeDtypeStruct((m, d), F32), jax.ShapeDtypeStruct(rows_buf.shape, F32),
                   jax.ShapeDtypeStruct((m, LANES), F32), jax.ShapeDtypeStruct((SUBLANES, m), F32),
                   jax.ShapeDtypeStruct((1, LANES), F32)],
        scratch_shapes=[pltpu.VMEM((1, LANES), F32)],
        input_output_aliases={0: 1},
        compiler_params=_params("arbitrary"),
        name="oproj_router",
    )(rows_buf, o, h, *weights)


def _row(ref, r):
    return ref.at[pl.ds(pl.multiple_of(r * ROW_TILES, ROW_TILES), ROW_TILES), :]


def _for_rows(lo, hi, fn):
    assert (hi - lo) % COPY_UNROLL == 0

    def body(j, carry):
        for u in range(COPY_UNROLL):
            fn(lo + j * COPY_UNROLL + u)
        return carry
    lax.fori_loop(0, (hi - lo) // COPY_UNROLL, body, 0)


def _moe_kernel(te_ref, nv_ref, nr_ref, pos_ref, rows_ref, wg_ref, wu_ref, wd_ref, y_ref,
                inv_ref, xg_ref, yb_ref, xb_ref, acc_ref, gsem, ssem, isem, *, tm, nf, nt, n_pairs, pair_regions):
    i = pl.program_id(0)
    f = pl.program_id(1)
    nv = nv_ref[0]
    valid = i < nv
    slot = i % 2
    assert nf == 2

    def gather_row(tile, r, to_slot):
        pair = inv_ref[tile * tm + r]
        token = pair
        for first_pair, _, first_token in pair_regions[1:]:
            token = jnp.where(pair >= first_pair, pair - (first_pair - first_token), token)
        pltpu.make_async_copy(_row(rows_ref, token), _row(xg_ref.at[to_slot], r), gsem.at[to_slot]).start()

    def scatter_row(tile, real, r, from_slot):
        pair = jnp.where(r < real, inv_ref[tile * tm + r], n_pairs + from_slot * tm + r)
        pltpu.make_async_copy(_row(yb_ref.at[from_slot], r), _row(y_ref, pair), ssem.at[from_slot]).start()

    def wait_gather(at_slot):
        pltpu.make_async_copy(rows_ref.at[pl.ds(0, tm * ROW_TILES), :], xg_ref.at[at_slot], gsem.at[at_slot]).wait()

    def wait_scatter(at_slot):
        pltpu.make_async_copy(yb_ref.at[at_slot], y_ref.at[pl.ds(0, tm * ROW_TILES), :], ssem.at[at_slot]).wait()

    @pl.when(jnp.logical_and(i == 0, f == 0))
    def _():
        def place(p):
            inv_ref[pos_ref[p]] = p
        _for_rows(0, n_pairs, place)

        def clear_padding(t, carry):
            def clear(r, c):
                inv_ref[t * tm + r] = 0
                return c
            return lax.fori_loop(nr_ref[t], tm, clear, carry)
        lax.fori_loop(0, nv, clear_padding, 0)
        yb_ref[...] = jnp.zeros(yb_ref.shape, F32)
        spare = y_ref.at[pl.ds(n_pairs * ROW_TILES, 2 * tm * ROW_TILES), :]
        for s in range(2):
            half = spare.at[pl.ds(s * tm * ROW_TILES, tm * ROW_TILES), :]
            pltpu.make_async_copy(yb_ref.at[s], half, isem).start()
            pltpu.make_async_copy(yb_ref.at[s], half, isem).wait()
        _for_rows(0, tm, lambda r: gather_row(0, r, 0))

    @pl.when(jnp.logical_and(valid, f == 0))
    def _():
        wait_gather(slot)
        xb_ref[...] = _tiles_to_rows(xg_ref.at[slot], tm).astype(BF16)
        acc_ref[...] = jnp.zeros((tm, D_MODEL), F32)

    def matmuls(first, rows):
        if first:
            nxt = jnp.minimum(i + 1, nv - 1)
            for r in range(tm):
                gather_row(nxt, r, 1 - slot)
        else:
            prev = jnp.maximum(i - 1, 0)
            real = jnp.where(i >= 1, nr_ref[prev], 0)
            for r in range(tm):
                scatter_row(prev, real, r, 1 - slot)
        x = xb_ref[0:rows, :]
        a = jnp.dot(x, wg_ref[0], preferred_element_type=F32)
        b = jnp.dot(x, wu_ref[0], preferred_element_type=F32)
        acc_ref[0:rows, :] += _dot(_silu(a) * b, wd_ref[0])

    half_full = nr_ref[jnp.minimum(i, nt - 1)] <= tm // 2
    for first in (True, False):
        for rows, which in ((tm // 2, half_full), (tm, jnp.logical_not(half_full))):
            @pl.when(jnp.logical_and(jnp.logical_and(valid, which), (f == 0) if first else (f > 0)))
            def _(first=first, rows=rows):
                matmuls(first, rows)

    @pl.when(jnp.logical_and(valid, f == nf - 1))
    def _():
        @pl.when(i >= 1)
        def _():
            wait_scatter(slot)
        _rows_to_tiles(yb_ref.at[slot], acc_ref[...])

        @pl.when(i == nv - 1)
        def _():
            _for_rows(0, tm, lambda r: scatter_row(i, nr_ref[i], r, slot))

    @pl.when(jnp.logical_and(i == nt - 1, f == nf - 1))
    def _():
        wait_gather(nv % 2)
        wait_scatter(nv % 2)
        wait_scatter((nv - 1) % 2)


def _moe(tile_expert, n_valid, tile_rows, pos, pair_regions, rows, wg, wu, wd):
    tm, tf = MOE_TILE, MOE_F_TILE
    nt, nf = tile_expert.shape[0], D_EXPERT // tf
    n_pairs = pos.shape[0]

    def fidx(i, f, nv):
        return jnp.where(i < nv[0], f, nf - 1)
    tile_buf = pltpu.VMEM((2, tm * ROW_TILES, LANES), F32)
    return pl.pallas_call(
        functools.partial(_moe_kernel, tm=tm, nf=nf, nt=nt, n_pairs=n_pairs, pair_regions=pair_regions),
        grid_spec=pltpu.PrefetchScalarGridSpec(
            num_scalar_prefetch=4,
            grid=(nt, nf),
            in_specs=[pl.BlockSpec(memory_space=pl.ANY),
                      pl.BlockSpec((1, D_MODEL, tf), lambda i, f, te, nv, *_: (te[i], 0, fidx(i, f, nv))),
                      pl.BlockSpec((1, D_MODEL, tf), lambda i, f, te, nv, *_: (te[i], 0, fidx(i, f, nv))),
                      pl.BlockSpec((1, tf, D_MODEL), lambda i, f, te, nv, *_: (te[i], fidx(i, f, nv), 0))],
            out_specs=pl.BlockSpec(memory_space=pl.ANY),
            scratch_shapes=[pltpu.SMEM((nt * tm,), jnp.int32), tile_buf, tile_buf,
                            pltpu.VMEM((tm, D_MODEL), BF16), pltpu.VMEM((tm, D_MODEL), F32),
                            pltpu.SemaphoreType.DMA((2,)), pltpu.SemaphoreType.DMA((2,)),
                            pltpu.SemaphoreType.DMA(())],
        ),
        out_shape=jax.ShapeDtypeStruct(((n_pairs + 2 * tm) * ROW_TILES, LANES), F32),
        compiler_params=_params("arbitrary", "arbitrary"),
        name="moe_grouped",
    )(tile_expert, n_valid, tile_rows, pos, rows, wg, wu, wd)


def _combine_kernel(y1_ref, y2_ref, h_ref, route_ref, p_ref, g_ref, wpg_ref, wpp_ref, o_ref, *, tb):
    route = route_ref[...]
    w1 = route[:, 4:5]
    w2 = route[:, 5:6]
    h5 = h_ref[...] + (w1 * _tiles_to_rows(y1_ref, tb) + w2 * _tiles_to_rows(y2_ref, tb))
    o_ref[...] = _ple(h5, p_ref[...], g_ref[...], wpg_ref[...], wpp_ref[...])


def _combine(y_pairs, h, route, p, g, wpg, wpp, *, first_pair):
    m, d = h.shape
    tb = min(COMBINE_TILE, m)
    weights = (g, wpg, wpp)
    row = lambda w: pl.BlockSpec((tb, w), lambda i: (i, 0))
    steps = m // tb
    first_block = first_pair // tb
    pair_rows = lambda slot: pl.BlockSpec((tb * ROW_TILES, LANES), lambda i: (i + first_block + slot * steps, 0))
    return pl.pallas_call(
        functools.partial(_combine_kernel, tb=tb),
        grid=(steps,),
        in_specs=[pair_rows(0), pair_rows(1), row(d), row(LANES),
                  pl.BlockSpec((tb, PLE_DIM), lambda i: (i + steps, 0))] + [_const(w.shape) for w in weights],
        out_specs=row(d),
        out_shape=jax.ShapeDtypeStruct((m, d), F32),
        compiler_params=_params("arbitrary"),
        name="moe_combine",
    )(y_pairs, y_pairs, h, route, p, *weights)


def _row2(v):
    return v.reshape(1, -1).astype(F32)


def kernel(x_prompt, x_sample, state_conv, cache_k, cache_v, p_prompt, p_sample, norm_mix, norm_ffn, norm_ple,
           conv_w_pw1, conv_b_pw1, conv_w_dw, conv_b_dw, conv_ln_g, conv_ln_b, conv_w_pw2, conv_b_pw2,
           norm_kv, w_k, w_v, k_norm, w_q, q_norm, sinks, w_o, ffn_w_gate, ffn_w_up, ffn_w_down,
           moe_w_router, moe_b_router, moe_w_gate, moe_w_up, moe_w_down, ple_w_gate, ple_w_proj):
    nb, t, d = x_prompt.shape
    ns = x_sample.shape[0]
    mp = nb * t
    bf = lambda w: w.astype(BF16)

    conv_w = (_row2(norm_mix[0]), bf(conv_w_pw1[0]), _row2(conv_b_pw1[0]), conv_w_dw[0], _row2(conv_b_dw[0]),
              _row2(conv_ln_g[0]), _row2(conv_ln_b[0]), bf(conv_w_pw2[0]), _row2(conv_b_pw2[0]))
    ffn_w = (_row2(norm_ffn[0]), bf(ffn_w_gate[0]), bf(ffn_w_up[0]), bf(ffn_w_down[0]))
    wq = w_q[0].reshape(d, N_KV_HEADS, GROUP, HEAD_DIM).transpose(0, 2, 1, 3).reshape(d, d)
    wo = w_o[0].reshape(N_KV_HEADS, GROUP, HEAD_DIM, d).transpose(1, 0, 2, 3).reshape(d, d)
    sinks_gk = sinks[0].astype(F32).reshape(N_KV_HEADS, GROUP).T.reshape(N_HEADS)
    kvq_w = (_row2(norm_ple[0]), bf(ple_w_gate[0]), bf(ple_w_proj[0]), _row2(norm_kv), bf(w_k), bf(w_v),
             _row2(jnp.tile(k_norm, N_KV_HEADS)), _row2(norm_mix[1]), bf(wq), _row2(jnp.tile(q_norm[0], N_HEADS)))
    wr = jnp.zeros((d, LANES), F32).at[:, :N_EXPERTS].set(moe_w_router[0])
    wr_hi = bf(wr)
    wr_lo = bf(wr - wr_hi.astype(F32))
    br = jnp.full((1, LANES), NEG_BIG, F32).at[0, :N_EXPERTS].set(moe_b_router[0])
    router_w = (bf(wo), _row2(norm_ffn[1]), jnp.concatenate([wr_hi, wr_lo], axis=1), br)
    ple1_w = (_row2(norm_ple[1]), bf(ple_w_gate[1]), bf(ple_w_proj[1]))

    h1p, conv_p = _conv_prompt(x_prompt, *conv_w)
    pp = p_prompt.reshape(-1, PLE_DIM)
    ps = p_sample.reshape(-1, PLE_DIM)
    expert_w = (moe_w_gate[0], moe_w_up[0], moe_w_down[0])
    h2p, *expert_bf = _ffn(h1p.reshape(mp, d), *ffn_w, cast=tuple(w.reshape(-1, w.shape[-1]) for w in expert_w))
    expert_bf = [b.reshape(w.shape) for b, w in zip(expert_bf, expert_w)]
    h3p, kp, vp, qp = _ple_kvq(h2p, pp, *kvq_w)
    h1s, conv_s = _conv_sample(x_sample.reshape(ns, d), state_conv.transpose(0, 2, 1, 3), *conv_w)
    conv_s = conv_s.transpose(0, 2, 1, 3)
    h3s, ks, vs, qs = _ple_kvq(_ffn(h1s, *ffn_w), ps, *kvq_w)

    kp3 = kp.reshape(nb, t, KV_DIM)
    vp3 = vp.reshape(nb, t, KV_DIM)
    op = _attn_prompt(qp.reshape(nb, t, d), kp3, vp3, sinks_gk)
    feature_major = lambda c: c.transpose(0, 2, 3, 1).reshape(ns, KV_DIM, WINDOW)
    os_, k_state_s, v_state_s = _attn_sample(qs, feature_major(cache_k), feature_major(cache_v), ks, vs,
                                             sinks_gk.reshape(N_HEADS, 1))

    mt = mp + ns
    rows = jnp.zeros((mt * ROW_TILES, LANES), F32)
    h4p, rows, route_p, sort_p, cnt_p = _oproj_router(rows, op.reshape(mp, d), h3p, *router_w, first_token=0)
    h4s, rows, route_s, sort_s, cnt_s = _oproj_router(rows, os_, h3s, *router_w, first_token=mp)

    cnt_p = cnt_p[0, :N_EXPERTS].astype(jnp.int32)
    cnt_s = cnt_s[0, :N_EXPERTS].astype(jnp.int32)
    group = (cnt_p + cnt_s + MOE_TILE - 1) // MOE_TILE * MOE_TILE
    ends = jnp.cumsum(group)
    starts = ends - group
    n_tiles = (2 * (mp + ns) + N_EXPERTS * (MOE_TILE - 1)) // MOE_TILE
    n_valid = (ends[-1] // MOE_TILE).astype(jnp.int32)
    tile_start = jnp.minimum(jnp.arange(n_tiles, dtype=jnp.int32), n_valid - 1) * MOE_TILE
    tile_expert = jnp.sum(tile_start[:, None] >= ends[None, :], axis=1).astype(jnp.int32)

    def positions(sort_info, slot, base):
        idx = sort_info[slot].astype(jnp.int32)
        rank = sort_info[2 + slot].astype(jnp.int32)
        return (starts + base)[idx] + rank
    no_base = jnp.zeros_like(cnt_p)
    pos = jnp.concatenate([positions(sort_p, 0, no_base), positions(sort_p, 1, no_base),
                           positions(sort_s, 0, cnt_p), positions(sort_s, 1, cnt_p)]).astype(jnp.int32)
    pair_regions = ((0, mp, 0), (mp, mp, 0), (2 * mp, ns, mp), (2 * mp + ns, ns, mp))
    tile_ids = jnp.arange(n_tiles, dtype=jnp.int32)
    rows_left = (starts + cnt_p + cnt_s)[tile_expert] - tile_ids * MOE_TILE
    tile_rows = jnp.where(tile_ids < n_valid, jnp.clip(rows_left, 0, MOE_TILE), 0).astype(jnp.int32)
    y_pairs = _moe(tile_expert, n_valid.reshape(1), tile_rows, pos, pair_regions, rows,
                   *expert_bf)

    y_p = _combine(y_pairs, h4p, route_p, pp, *ple1_w, first_pair=0)
    y_s = _combine(y_pairs, h4s, route_s, ps, *ple1_w, first_pair=2 * mp)

    kv4 = lambda a, n: a.reshape(n, WINDOW, N_KV_HEADS, HEAD_DIM)
    return (y_p.reshape(nb, t, d), y_s.reshape(ns, 1, d), conv_p,
            kv4(kp3[:, t - WINDOW:], nb), kv4(vp3[:, t - WINDOW:], nb),
            conv_s, kv4(k_state_s, ns), kv4(v_state_s, ns))
```
